```python
import math
import jax
import jax.numpy as jnp
from jax import lax
import numpy as np

D_MODEL = 1024
BATCH = 4
SEQ = 8192
DEPTH = 1

CTX_LEN = 256
GRID_W = 64
EPS = 1e-6
N_MOD = 9
D_FF = 2816
GLA_HEADS = 4
GLA_DK = D_MODEL // 8
GLA_DV = D_MODEL // 4
GLA_RANK = 16
GLA_TAU = 16.0
GLA_CHUNK = 64
DIFF_HEADS = 8
DIFF_DH = D_MODEL // 16
Q_BLOCK = 128
ROPE_BASE = 10000.0
GLA_QK = GLA_HEADS * GLA_DK
GLA_V = GLA_HEADS * GLA_DV
DIFF_QK = DIFF_HEADS * 2 * DIFF_DH
DIFF_V = DIFF_HEADS * 2 * DIFF_DH
IN_SPLITS = (GLA_QK, GLA_QK, GLA_V, GLA_V, 2 * GLA_RANK, DIFF_QK, DIFF_QK, DIFF_V, 2 * D_MODEL)
IN_COLS = 2 * GLA_QK + 2 * GLA_V + 2 * GLA_RANK + 2 * DIFF_QK + DIFF_V + 2 * D_MODEL

kernel_name = 'hybrid_gla_diffattn_macaron_dit'


def _rmsnorm(x, g):
    xf = x.astype(jnp.float32)
    y = xf * lax.rsqrt(jnp.mean(xf * xf, axis=-1, keepdims=True) + EPS)
    return (y * g.astype(jnp.float32)).astype(x.dtype)


def _modnorm(h, g, shift, scale):
    return _rmsnorm(h, g) * (1.0 + scale) + shift


def _swiglu(h, w1, w3, w2):
    return (jax.nn.silu(h @ w1) * (h @ w3)) @ w2


def _split_cols(p):
    idx = np.cumsum(IN_SPLITS)[:-1].tolist()
    return jnp.split(p, idx, axis=-1)


def _axial_rope(t, row, col):
    half = DIFF_DH // 2
    quarter = half // 2
    inv = ROPE_BASE ** (-np.arange(quarter, dtype=np.float32) / quarter)

    def rot(u, pos):
        ang = pos.astype(jnp.float32)[:, None] * inv[None, :]
        cos = jnp.cos(ang)[None, :, None, None, :].astype(u.dtype)
        sin = jnp.sin(ang)[None, :, None, None, :].astype(u.dtype)
        u1, u2 = u[..., :quarter], u[..., quarter:]
        return jnp.concatenate([u1 * cos - u2 * sin, u2 * cos + u1 * sin], axis=-1)

    return jnp.concatenate([rot(t[..., :half], row), rot(t[..., half:], col)], axis=-1)


def _gla_scan(q, k, v, logf, s0):
    b_, L, h, _ = q.shape
    dv = v.shape[-1]
    n = L // GLA_CHUNK

    def chunks(t):
        return t.astype(jnp.float32).reshape(b_, n, GLA_CHUNK, h, t.shape[-1]).transpose(1, 0, 3, 2, 4)

    tril = jnp.tril(jnp.ones((GLA_CHUNK, GLA_CHUNK), dtype=bool))[:, :, None]

    def step(state, inp):
        qc, kc, vc, gc = inp
        cum = jnp.cumsum(gc, axis=2)
        o_inter = jnp.einsum('bhtd,bhde->bhte', qc * jnp.exp(cum), state)
        rel = jnp.where(tril, cum[:, :, :, None, :] - cum[:, :, None, :, :], -jnp.inf)
        attn = jnp.einsum('bhtd,bhsd,bhtsd->bhts', qc, kc, jnp.exp(rel))
        o_intra = jnp.einsum('bhts,bhse->bhte', attn, vc)
        last = cum[:, :, -1:, :]
        new_state = (jnp.exp(last[:, :, 0, :])[..., None] * state
                     + jnp.einsum('bhsd,bhse->bhde', kc * jnp.exp(last - cum), vc))
        return new_state, o_inter + o_intra

    state, o = lax.scan(step, s0, (chunks(q), chunks(k), chunks(v), chunks(logf)))
    o = o.transpose(1, 0, 3, 2, 4).reshape(b_, L, h, dv)
    return o, state


def _gla_direction(q, k, v, lr, w2, bias, s0, reverse):
    logf = jax.nn.log_sigmoid((lr @ w2 + bias).astype(jnp.float32)) / GLA_TAU
    logf = logf.reshape(q.shape)
    if reverse:
        q, k, v, logf = jnp.flip(q, 1), jnp.flip(k, 1), jnp.flip(v, 1), jnp.flip(logf, 1)
    o, s = _gla_scan(q, k, v, logf, s0)
    if reverse:
        o = jnp.flip(o, 1)
    return o.astype(v.dtype), s


def _gla_heads(q, k, v, L):
    b_ = q.shape[0]
    return (q.reshape(b_, L, GLA_HEADS, GLA_DK) * (GLA_DK ** -0.5),
            k.reshape(b_, L, GLA_HEADS, GLA_DK),
            v.reshape(b_, L, GLA_HEADS, GLA_DV))


def _gla_head_out(o, gate, norm_g, w_branch):
    b_, L = o.shape[:2]
    y = _rmsnorm(o, norm_g.reshape(GLA_HEADS, GLA_DV)).reshape(b_, L, GLA_V)
    return (y * jax.nn.silu(gate)) @ w_branch


def _diff_block(q, k, v, lam):
    s = jnp.einsum('bqhcd,bkhcd->bhcqk', q, k).astype(jnp.float32) * (DIFF_DH ** -0.5)
    p = jax.nn.softmax(s, axis=-1)
    a = p[:, :, 0] - lam * p[:, :, 1]
    return jnp.einsum('bhqk,bkhe->bqhe', a.astype(v.dtype), v)


def _diff_latent(q, k, v, lam):
    b_, L = q.shape[:2]
    nb = L // Q_BLOCK
    qb = q.reshape((b_, nb, Q_BLOCK) + q.shape[2:]).swapaxes(0, 1)
    o = lax.map(lambda qq: _diff_block(qq, k, v, lam), qb)
    return o.swapaxes(0, 1).reshape((b_, L) + o.shape[3:])


def _diff_head_out(o, norm_g, lam_init, w_branch):
    b_, L = o.shape[:2]
    y = _rmsnorm(o, norm_g.reshape(DIFF_HEADS, 2 * DIFF_DH)).reshape(b_, L, DIFF_V)
    return (y * (1.0 - lam_init)) @ w_branch


def setup_inputs(seed: int = 0) -> dict:
    key = jax.random.key(seed)
    ks = jax.random.split(key, 30)
    f32 = jnp.float32

    def nrm(k, shape, scale):
        return jax.random.normal(k, shape, f32) * scale

    def gain(k, shape):
        return 1.0 + 0.02 * jax.random.normal(k, shape, f32)

    L = DEPTH
    return {
        'x': nrm(ks[0], (BATCH, SEQ, D_MODEL), 1.0),
        'c': nrm(ks[1], (BATCH, D_MODEL), 1.0),
        'ctx': nrm(ks[2], (BATCH, CTX_LEN, D_MODEL), 1.0),
        'c_ctx': nrm(ks[3], (D_MODEL,), 1.0),
        'w_ada': nrm(ks[4], (L, D_MODEL, N_MOD * D_MODEL), 0.5 * D_MODEL ** -0.5),
        'b_ada': nrm(ks[5], (L, N_MOD * D_MODEL), 0.02),
        'ffn1_norm': gain(ks[6], (L, D_MODEL)),
        'ffn1_w1': nrm(ks[7], (L, D_MODEL, D_FF), D_MODEL ** -0.5),
        'ffn1_w3': nrm(ks[8], (L, D_MODEL, D_FF), D_MODEL ** -0.5),
        'ffn1_w2': nrm(ks[9], (L, D_FF, D_MODEL), D_FF ** -0.5),
        'mix_norm': gain(ks[10], (L, D_MODEL)),
        'w_in': nrm(ks[11], (L, D_MODEL, IN_COLS), D_MODEL ** -0.5),
        'gla_gate_w2': nrm(ks[12], (L, 2, GLA_RANK, GLA_QK), GLA_RANK ** -0.5),
        'gla_gate_b': nrm(ks[13], (L, 2, GLA_QK), 0.1),
        'gla_out_norm': gain(ks[14], (L, GLA_V)),
        'diff_lambda': nrm(ks[15], (L, 4, DIFF_DH), 0.1),
        'diff_out_norm': gain(ks[16], (L, DIFF_V)),
        'w_branch_gla': nrm(ks[17], (L, GLA_V, D_MODEL), GLA_V ** -0.5),
        'w_branch_diff': nrm(ks[18], (L, DIFF_V, D_MODEL), DIFF_V ** -0.5),
        'w_out': nrm(ks[19], (L, D_MODEL, D_MODEL), D_MODEL ** -0.5),
        'ffn2_norm': gain(ks[20], (L, D_MODEL)),
        'ffn2_w1': nrm(ks[21], (L, D_MODEL, D_FF), D_MODEL ** -0.5),
        'ffn2_w3': nrm(ks[22], (L, D_MODEL, D_FF), D_MODEL ** -0.5),
        'ffn2_w2': nrm(ks[23], (L, D_FF, D_MODEL), D_FF ** -0.5),
        'final_norm': gain(ks[24], (D_MODEL,)),
    }


def reference(x, c, ctx, c_ctx, w_ada, b_ada, ffn1_norm, ffn1_w1, ffn1_w3, ffn1_w2, mix_norm, w_in,
              gla_gate_w2, gla_gate_b, gla_out_norm, diff_lambda, diff_out_norm, w_branch_gla,
              w_branch_diff, w_out, ffn2_norm, ffn2_w1, ffn2_w3, ffn2_w2, final_norm):
    b_, seq, _ = x.shape
    n_ctx = ctx.shape[1]
    rows = seq // GRID_W
    t = jnp.arange(rows * GRID_W, dtype=jnp.int32)
    t_row, t_col = t // GRID_W, t % GRID_W
    s0 = jnp.zeros((b_, GLA_HEADS, GLA_DK, GLA_DV), jnp.float32)
    xl, xc = x, ctx
    for i in range(DEPTH):
        last = i == DEPTH - 1
        lam_init = 0.8 - 0.6 * math.exp(-0.3 * i)
        ml = jnp.split((jax.nn.silu(c) @ w_ada[i] + b_ada[i])[:, None, :], N_MOD, axis=-1)
        mc = jnp.split((jax.nn.silu(c_ctx) @ w_ada[i] + b_ada[i])[None, None, :], N_MOD, axis=-1)

        xl = xl + 0.5 * ml[2] * _swiglu(_modnorm(xl, ffn1_norm[i], ml[0], ml[1]), ffn1_w1[i], ffn1_w3[i], ffn1_w2[i])
        xc = xc + 0.5 * mc[2] * _swiglu(_modnorm(xc, ffn1_norm[i], mc[0], mc[1]), ffn1_w1[i], ffn1_w3[i], ffn1_w2[i])

        gq_l, gk_l, gv_l, gg_l, lr_l, dq_l, dk_l, dv_l, mg_l = _split_cols(
            _modnorm(xl, mix_norm[i], ml[3], ml[4]) @ w_in[i])
        gq_c, gk_c, gv_c, gg_c, lr_c, dq_c, dk_c, dv_c, mg_c = _split_cols(
            _modnorm(xc, mix_norm[i], mc[3], mc[4]) @ w_in[i])

        gq_c, gk_c, gv_c = _gla_heads(gq_c, gk_c, gv_c, n_ctx)
        gq_l, gk_l, gv_l = _gla_heads(gq_l, gk_l, gv_l, seq)
        oc_f, sc_f = _gla_direction(gq_c, gk_c, gv_c, lr_c[..., :GLA_RANK], gla_gate_w2[i, 0], gla_gate_b[i, 0], s0, False)
        oc_b, sc_b = _gla_direction(gq_c, gk_c, gv_c, lr_c[..., GLA_RANK:], gla_gate_w2[i, 1], gla_gate_b[i, 1], s0, True)
        ol_f, _ = _gla_direction(gq_l, gk_l, gv_l, lr_l[..., :GLA_RANK], gla_gate_w2[i, 0], gla_gate_b[i, 0], sc_f, False)
        ol_b, _ = _gla_direction(gq_l, gk_l, gv_l, lr_l[..., GLA_RANK:], gla_gate_w2[i, 1], gla_gate_b[i, 1], sc_b, True)
        y_gla_l = _gla_head_out(ol_f + ol_b, gg_l, gla_out_norm[i], w_branch_gla[i])

        lam = (jnp.exp(jnp.sum(diff_lambda[i, 0] * diff_lambda[i, 1]))
               - jnp.exp(jnp.sum(diff_lambda[i, 2] * diff_lambda[i, 3])) + lam_init)
        dq_l = _axial_rope(dq_l.reshape(b_, seq, DIFF_HEADS, 2, DIFF_DH), t_row, t_col)
        dk_l = _axial_rope(dk_l.reshape(b_, seq, DIFF_HEADS, 2, DIFF_DH), t_row, t_col)
        dv_l = dv_l.reshape(b_, seq, DIFF_HEADS, 2 * DIFF_DH)
        dk_c = dk_c.reshape(b_, n_ctx, DIFF_HEADS, 2, DIFF_DH)
        dv_c = dv_c.reshape(b_, n_ctx, DIFF_HEADS, 2 * DIFF_DH)
        k_all = jnp.concatenate([dk_c, dk_l], axis=1)
        v_all = jnp.concatenate([dv_c, dv_l], axis=1)
        y_diff_l = _diff_head_out(_diff_latent(dq_l, k_all, v_all, lam), diff_out_norm[i], lam_init, w_branch_diff[i])

        ga_l, gb_l = jnp.split(jax.nn.sigmoid(mg_l), 2, axis=-1)
        xl = xl + ml[5] * ((ga_l * y_gla_l + gb_l * y_diff_l) @ w_out[i])

        if not last:
            y_gla_c = _gla_head_out(oc_f + oc_b, gg_c, gla_out_norm[i], w_branch_gla[i])
            od_c = _diff_block(dq_c.reshape(b_, n_ctx, DIFF_HEADS, 2, DIFF_DH), dk_c, dv_c, lam)
            y_diff_c = _diff_head_out(od_c, diff_out_norm[i], lam_init, w_branch_diff[i])
            ga_c, gb_c = jnp.split(jax.nn.sigmoid(mg_c), 2, axis=-1)
            xc = xc + mc[5] * ((ga_c * y_gla_c + gb_c * y_diff_c) @ w_out[i])
            xc = xc + 0.5 * mc[8] * _swiglu(_modnorm(xc, ffn2_norm[i], mc[6], mc[7]), ffn2_w1[i], ffn2_w3[i], ffn2_w2[i])

        xl = xl + 0.5 * ml[8] * _swiglu(_modnorm(xl, ffn2_norm[i], ml[6], ml[7]), ffn2_w1[i], ffn2_w3[i], ffn2_w2[i])
    return _rmsnorm(xl, final_norm)
```

```python
import functools
import math

import jax
import jax.numpy as jnp
import numpy as np
from jax import lax
from jax.experimental import pallas as pl
from jax.experimental.pallas import tpu as pltpu

F32 = jnp.float32
BF16 = jnp.bfloat16

EPS = 1e-6
GRID_W = 64
N_MOD = 9
MOD_ROWS = 16
GLA_HEADS = 4
GLA_RANK = 16
GLA_TAU = 16.0
GLA_CHUNK = 64
DIFF_HEADS = 8
DIFF_DH = 64
ROPE_BASE = 10000.0
LANES = 128

TOKEN_TILE = 512
ATTN_TQ = 512
ATTN_TK = 256
VMEM_LIMIT = 52 * 1024 * 1024


def _resident(shape):
    zeros = (0,) * len(shape)
    return pl.BlockSpec(shape, lambda *_: zeros, pipeline_mode=pl.Buffered(1))


def _params(semantics):
    return pltpu.CompilerParams(dimension_semantics=semantics, vmem_limit_bytes=VMEM_LIMIT)


def _dot(a, b):
    return jnp.dot(a, b, preferred_element_type=F32)


def _dot_nt(a, b):
    return lax.dot_general(a, b, (((1,), (1,)), ((), ())), preferred_element_type=F32)


def _dot_tn(a, b):
    return lax.dot_general(a, b, (((0,), (0,)), ((), ())), preferred_element_type=F32)


def _sigmoid(x):
    return 1.0 / (1.0 + jnp.exp(-x))


def _rms(x):
    return x * lax.rsqrt(jnp.mean(x * x, axis=-1, keepdims=True) + EPS)


def _modnorm(x, gain, shift, scale1):
    return (_rms(x) * gain) * scale1 + shift


def _mod_kernel(c_ref, w_ref, b_ref, o_ref):
    k = pl.program_id(0)
    c = c_ref[...]
    s = c * _sigmoid(c)
    acc = jnp.zeros(o_ref.shape, F32)
    w = w_ref[...]
    s_parts = _split3(s)
    w_parts = _split3(w)
    for i in range(3):
        for j in range(3 - i):
            acc = acc + _dot(s_parts[i], w_parts[j])
    val = acc + b_ref[...]
    add = jnp.where(k % 3 == 1, 1.0, 0.0).astype(F32)
    mul = jnp.where((k == 2) | (k == 8), 0.5, 1.0).astype(F32)
    o_ref[...] = (val + add) * mul


def _split3(x):
    hi = x.astype(BF16)
    r1 = x - hi.astype(F32)
    mid = r1.astype(BF16)
    lo = (r1 - mid.astype(F32)).astype(BF16)
    return hi, mid, lo


def _mod_table(c_rows, w_ada, b_ada):
    rows, d = c_rows.shape
    return pl.pallas_call(
        _mod_kernel,
        out_shape=jax.ShapeDtypeStruct((rows, N_MOD * d), F32),
        grid=(N_MOD,),
        in_specs=[
            pl.BlockSpec((rows, d), lambda k: (0, 0)),
            pl.BlockSpec((d, d), lambda k: (0, k)),
            pl.BlockSpec((1, d), lambda k: (0, k)),
        ],
        out_specs=pl.BlockSpec((rows, d), lambda k: (0, k)),
        compiler_params=_params(("arbitrary",)),
        name="mod_table",
    )(c_rows, w_ada, b_ada)


def _ffn_kernel(x_ref, mod_ref, g_ref, w1_ref, w3_ref, w2_ref, *rest, row0, f_chunk, final):
    if final:
        gf_ref, o_ref = rest
    else:
        (o_ref,) = rest
    x = x_ref[0]
    mod = mod_ref[0, 0]
    shift, scale1, gate = mod[row0:row0 + 1], mod[row0 + 1:row0 + 2], mod[row0 + 2:row0 + 3]
    h = _modnorm(x, g_ref[...], shift, scale1).astype(BF16)
    acc = jnp.zeros(x.shape, F32)
    f_total = w1_ref.shape[1]
    for f0 in range(0, f_total, f_chunk):
        a = _dot(h, w1_ref[:, f0:f0 + f_chunk])
        b = _dot(h, w3_ref[:, f0:f0 + f_chunk])
        u = (a * _sigmoid(a) * b).astype(BF16)
        acc = acc + _dot(u, w2_ref[f0:f0 + f_chunk, :])
    y = x + gate * acc
    if final:
        y = _rms(y) * gf_ref[...]
    o_ref[0] = y


def _ffn(xs, modt, gain, w1, w3, w2, *, row0, n_lat_tiles, n_tiles, final_gain=None):
    b_, _, d = xs.shape
    f = w1.shape[1]
    tm = TOKEN_TILE
    final = final_gain is not None
    kern = functools.partial(_ffn_kernel, row0=row0, f_chunk=f // 2, final=final)
    in_specs = [
        pl.BlockSpec((1, tm, d), lambda b, j: (b, j, 0)),
        pl.BlockSpec((1, 1, MOD_ROWS, d), lambda b, j: (b, (j >= n_lat_tiles).astype(jnp.int32), 0, 0)),
        _resident((1, d)),
        _resident((d, f)),
        _resident((d, f)),
        _resident((f, d)),
    ]
    args = [xs, modt, gain, w1, w3, w2]
    if final:
        in_specs.append(_resident((1, d)))
        args.append(final_gain)
    return pl.pallas_call(
        kern,
        out_shape=jax.ShapeDtypeStruct((b_, n_tiles * tm, d), F32),
        grid=(b_, n_tiles),
        in_specs=in_specs,
        out_specs=pl.BlockSpec((1, tm, d), lambda b, j: (b, j, 0)),
        compiler_params=_params(("parallel", "parallel")),
        name="ffn_final" if final else "ffn",
    )(*args)


def _proj_gla_kernel(x_ref, mod_ref, g_ref, wq_ref, wk_ref, wv_ref, wg_ref, wlr_ref, w2_ref, b2_ref,
                     q_ref, k_ref, v_ref, sg_ref, lf_ref, *, q_scale):
    x = x_ref[0]
    mod = mod_ref[0, 0]
    h = _modnorm(x, g_ref[...], mod[3:4], mod[4:5]).astype(BF16)
    q_ref[0] = _dot(h, wq_ref[...]) * q_scale
    k_ref[0] = _dot(h, wk_ref[...])
    v_ref[0] = _dot(h, wv_ref[...]).astype(BF16)
    gg = _dot(h, wg_ref[...])
    sg_ref[0] = gg * _sigmoid(gg)
    lr = _dot(h, wlr_ref[...]).astype(BF16)
    z = _dot(lr, w2_ref[...]) + b2_ref[...]
    lf_ref[0] = (jnp.minimum(z, 0.0) - jnp.log(1.0 + jnp.exp(-jnp.abs(z)))) * (1.0 / GLA_TAU)


def _proj_gla(xs, modt, gain, wq, wk, wv, wg, wlr, w2cat, b2cat, *, n_lat_tiles, n_tiles):
    b_, lp, d = xs.shape
    tm = TOKEN_TILE
    qk = wq.shape[1]
    vv = wv.shape[1]
    row = lambda width: pl.BlockSpec((1, tm, width), lambda b, j: (b, j, 0))
    kern = functools.partial(_proj_gla_kernel, q_scale=(qk // GLA_HEADS) ** -0.5)
    return pl.pallas_call(
        kern,
        out_shape=(
            jax.ShapeDtypeStruct((b_, lp, qk), F32),
            jax.ShapeDtypeStruct((b_, lp, qk), F32),
            jax.ShapeDtypeStruct((b_, lp, vv), BF16),
            jax.ShapeDtypeStruct((b_, lp, vv), F32),
            jax.ShapeDtypeStruct((b_, lp, 2 * qk), F32),
        ),
        grid=(b_, n_tiles),
        in_specs=[
            row(d),
            pl.BlockSpec((1, 1, MOD_ROWS, d), lambda b, j: (b, (j >= n_lat_tiles).astype(jnp.int32), 0, 0)),
            _resident((1, d)),
            _resident(wq.shape), _resident(wk.shape), _resident(wv.shape), _resident(wg.shape),
            _resident(wlr.shape), _resident(w2cat.shape), _resident(b2cat.shape),
        ],
        out_specs=(row(qk), row(qk), row(vv), row(vv), row(2 * qk)),
        compiler_params=_params(("parallel", "parallel")),
        name="proj_gla",
    )(xs, modt, gain, wq, wk, wv, wg, wlr, w2cat, b2cat)


def _rope(u, cos, sin_signed, first_half):
    out = []
    for g in range(u.shape[1] // LANES):
        ug = u[:, g * LANES:(g + 1) * LANES]
        partner = jnp.where(first_half, pltpu.roll(ug, LANES - 16, 1), pltpu.roll(ug, 16, 1))
        out.append(ug * cos + partner * sin_signed)
    return jnp.concatenate(out, axis=1)


def _proj_dif_kernel(x_ref, mod_ref, g_ref, cos_ref, sin_ref, wq_ref, wk_ref, wv_ref, wm_ref,
                     q_ref, k_ref, v_ref, sm_ref, *, q_scale):
    x = x_ref[0]
    mod = mod_ref[0, 0]
    h = _modnorm(x, g_ref[...], mod[3:4], mod[4:5]).astype(BF16)
    cos = cos_ref[...]
    sin_signed = sin_ref[...]
    lane = lax.broadcasted_iota(jnp.int32, cos.shape, 1)
    first_half = (lane % 32) < 16
    q_ref[0] = (_rope(_dot(h, wq_ref[...]), cos, sin_signed, first_half) * q_scale).astype(BF16)
    k_ref[0] = _rope(_dot(h, wk_ref[...]), cos, sin_signed, first_half).astype(BF16)
    v_ref[0] = _dot(h, wv_ref[...]).astype(BF16)
    sm_ref[0] = _sigmoid(_dot(h, wm_ref[...]))


def _proj_dif(xs, modt, gain, cos_t, sin_t, wq, wk, wv, wm, *, n_lat_tiles, n_tiles):
    b_, lp, d = xs.shape
    tm = TOKEN_TILE
    row = lambda width: pl.BlockSpec((1, tm, width), lambda b, j: (b, j, 0))
    tab = pl.BlockSpec((tm, LANES), lambda b, j: (j, 0))
    kern = functools.partial(_proj_dif_kernel, q_scale=DIFF_DH ** -0.5)
    return pl.pallas_call(
        kern,
        out_shape=(
            jax.ShapeDtypeStruct((b_, lp, wq.shape[1]), BF16),
            jax.ShapeDtypeStruct((b_, lp, wk.shape[1]), BF16),
            jax.ShapeDtypeStruct((b_, lp, wv.shape[1]), BF16),
            jax.ShapeDtypeStruct((b_, lp, wm.shape[1]), F32),
        ),
        grid=(b_, n_tiles),
        in_specs=[
            row(d),
            pl.BlockSpec((1, 1, MOD_ROWS, d), lambda b, j: (b, (j >= n_lat_tiles).astype(jnp.int32), 0, 0)),
            _resident((1, d)),
            tab, tab,
            _resident(wq.shape), _resident(wk.shape), _resident(wv.shape), _resident(wm.shape),
        ],
        out_specs=(row(wq.shape[1]), row(wk.shape[1]), row(wv.shape[1]), row(wm.shape[1])),
        compiler_params=_params(("parallel", "parallel")),
        name="proj_dif",
    )(xs, modt, gain, cos_t, sin_t, wq, wk, wv, wm)


def _gla_direction(q, k, v, logf, tri, strict_keep, ref_row, end_row, s_ref, o_ref):
    parts = _split3(logf)
    cum = _dot(tri, parts[0]) + _dot(tri, parts[1]) + _dot(tri, parts[2])
    ref = cum[ref_row:ref_row + 1]
    total = cum[end_row:end_row + 1]
    q_state = (q * jnp.exp(cum)).astype(BF16)
    q_in = (q * jnp.exp(cum - ref)).astype(BF16)
    k_in = (k * jnp.exp(ref - cum)).astype(BF16)
    k_out = (k * jnp.exp(total - cum)).astype(BF16)
    decay = jnp.exp(total)
    dk = q.shape[1] // GLA_HEADS
    dv = v.shape[1] // GLA_HEADS
    for hd in range(GLA_HEADS):
        ks = slice(hd * dk, (hd + 1) * dk)
        vs = slice(hd * dv, (hd + 1) * dv)
        state_t = s_ref[hd]
        vh = v[:, vs]
        attn = jnp.where(strict_keep, _dot_nt(q_in[:, ks], k_in[:, ks]), 0.0).astype(BF16)
        o_ref[0, :, vs] = _dot_nt(q_state[:, ks], state_t.astype(BF16)) + _dot(attn, vh)
        s_ref[hd] = state_t * decay[:, ks] + _dot_tn(vh, k_out[:, ks])


def _gla_kernel(qf_ref, kf_ref, vf_ref, lf_ref, qb_ref, kb_ref, vb_ref, lb_ref,
                of_ref, ob_ref, sf_ref, sb_ref):
    @pl.when(pl.program_id(1) == 0)
    def _():
        sf_ref[...] = jnp.zeros(sf_ref.shape, F32)
        sb_ref[...] = jnp.zeros(sb_ref.shape, F32)

    c = qf_ref.shape[1]
    row = lax.broadcasted_iota(jnp.int32, (c, c), 0)
    col = lax.broadcasted_iota(jnp.int32, (c, c), 1)
    lower = col <= row
    upper = col >= row
    _gla_direction(qf_ref[0], kf_ref[0], vf_ref[0], lf_ref[0],
                   jnp.where(lower, 1.0, 0.0).astype(BF16), lower, c // 2 - 1, c - 1, sf_ref, of_ref)
    _gla_direction(qb_ref[0], kb_ref[0], vb_ref[0], lb_ref[0],
                   jnp.where(upper, 1.0, 0.0).astype(BF16), upper, c // 2, 0, sb_ref, ob_ref)


def _gla(gq, gk, gv, logf, *, n_lat_chunks, n_ctx_chunks):
    b_, lp, qk = gq.shape
    vv = gv.shape[2]
    c = GLA_CHUNK
    n = n_lat_chunks + n_ctx_chunks
    fwd = lambda b, s: (b, (s + n_lat_chunks) % n, 0)
    bwd = lambda b, s: (b, n - 1 - s, 0)
    lf_spec = pl.BlockSpec((1, c, qk), lambda b, s: (b, (s + n_lat_chunks) % n, 0))
    lb_spec = pl.BlockSpec((1, c, qk), lambda b, s: (b, n - 1 - s, 1))
    return pl.pallas_call(
        _gla_kernel,
        out_shape=(jax.ShapeDtypeStruct((b_, lp, vv), F32), jax.ShapeDtypeStruct((b_, lp, vv), F32)),
        grid=(b_, n),
        in_specs=[
            pl.BlockSpec((1, c, qk), fwd), pl.BlockSpec((1, c, qk), fwd), pl.BlockSpec((1, c, vv), fwd), lf_spec,
            pl.BlockSpec((1, c, qk), bwd), pl.BlockSpec((1, c, qk), bwd), pl.BlockSpec((1, c, vv), bwd), lb_spec,
        ],
        out_specs=(pl.BlockSpec((1, c, vv), fwd), pl.BlockSpec((1, c, vv), bwd)),
        scratch_shapes=[
            pltpu.VMEM((GLA_HEADS, vv // GLA_HEADS, qk // GLA_HEADS), F32),
            pltpu.VMEM((GLA_HEADS, vv // GLA_HEADS, qk // GLA_HEADS), F32),
        ],
        compiler_params=_params(("parallel", "arbitrary")),
        name="gla_scan",
    )(gq, gk, gv, logf, gq, gk, gv, logf)


def _attn_kernel(q_ref, k_ref, v_ref, lam_ref, o_ref, m_ref, l_ref, acc_ref, *, tk, n_kblocks, lam_init):
    tq = q_ref.shape[1]
    q = q_ref[0]
    lane = lax.broadcasted_iota(jnp.int32, q.shape, 1)
    zero = jnp.zeros_like(q)
    qz = jnp.concatenate([jnp.where(lane < DIFF_DH, q, zero), jnp.where(lane >= DIFF_DH, q, zero)], axis=0)
    m_ref[...] = jnp.full(m_ref.shape, -jnp.inf, F32)
    l_ref[...] = jnp.zeros(l_ref.shape, F32)
    acc_ref[...] = jnp.zeros(acc_ref.shape, F32)

    def step(j, carry):
        start = pl.multiple_of(j * tk, tk)
        kb = k_ref[0, pl.ds(start, tk), :]
        vb = v_ref[0, pl.ds(start, tk), :]
        s = _dot_nt(qz, kb)
        m_prev = m_ref[...]
        m_new = jnp.maximum(m_prev, jnp.max(s, axis=-1, keepdims=True))
        alpha = jnp.exp(m_prev - m_new)
        p = jnp.exp(s - m_new[:, :1])
        l_ref[...] = alpha * l_ref[...] + jnp.sum(p, axis=-1, keepdims=True)
        acc_ref[...] = alpha * acc_ref[...] + _dot(p.astype(BF16), vb)
        m_ref[...] = m_new
        return carry

    lax.fori_loop(0, n_kblocks, step, 0)

    lp = lam_ref[...]
    lam = (jnp.exp(jnp.sum(lp[0:1] * lp[1:2], axis=-1, keepdims=True))
           - jnp.exp(jnp.sum(lp[2:3] * lp[3:4], axis=-1, keepdims=True)) + lam_init)
    o = acc_ref[...] / l_ref[...]
    o_ref[0] = o[:tq] - lam * o[tq:]


def _attn(dq, dk, dv, lam_p, *, seq, n_keys, lam_init):
    b_, lp, width = dq.shape
    tq, tk = ATTN_TQ, ATTN_TK
    heads = width // LANES
    kern = functools.partial(_attn_kernel, tk=tk, n_kblocks=n_keys // tk, lam_init=lam_init)
    return pl.pallas_call(
        kern,
        out_shape=jax.ShapeDtypeStruct((b_, seq, width), F32),
        grid=(b_, heads, seq // tq),
        in_specs=[
            pl.BlockSpec((1, tq, LANES), lambda b, h, i: (b, i, h)),
            pl.BlockSpec((1, lp, LANES), lambda b, h, i: (b, 0, h)),
            pl.BlockSpec((1, lp, LANES), lambda b, h, i: (b, 0, h)),
            pl.BlockSpec(lam_p.shape, lambda b, h, i: (0, 0)),
        ],
        out_specs=pl.BlockSpec((1, tq, LANES), lambda b, h, i: (b, i, h)),
        scratch_shapes=[
            pltpu.VMEM((2 * tq, LANES), F32),
            pltpu.VMEM((2 * tq, LANES), F32),
            pltpu.VMEM((2 * tq, LANES), F32),
        ],
        compiler_params=_params(("parallel", "parallel", "parallel")),
        name="diff_attn",
    )(dq, dk, dv, lam_p)


def _head_rms(x, gain, width):
    parts = []
    for h0 in range(0, x.shape[1], width):
        parts.append(_rms(x[:, h0:h0 + width]))
    return jnp.concatenate(parts, axis=1) * gain


def _merge_kernel(x_ref, mod_ref, of_ref, ob_ref, sg_ref, od_ref, sm_ref, gg_ref, gd_ref,
                  wa_ref, wb_ref, wo_ref, o_ref, *, dv_gla, dv_diff, diff_scale):
    d = x_ref.shape[2]
    ya = _head_rms(of_ref[0] + ob_ref[0], gg_ref[...], dv_gla) * sg_ref[0]
    ya = _dot(ya.astype(BF16), wa_ref[...])
    yb = _head_rms(od_ref[0], gd_ref[...], dv_diff) * diff_scale
    yb = _dot(yb.astype(BF16), wb_ref[...])
    sm = sm_ref[0]
    merged = (sm[:, :d] * ya + sm[:, d:] * yb).astype(BF16)
    gate = mod_ref[0, 0][5:6]
    o_ref[0] = x_ref[0] + gate * _dot(merged, wo_ref[...])


def _merge(x1, modt, o_f, o_b, sgg, o_d, smg, g_gla, g_diff, wa, wb, wo, *, n_lat_tiles, diff_scale):
    b_, _, d = x1.shape
    tm = TOKEN_TILE
    row = lambda width: pl.BlockSpec((1, tm, width), lambda b, j: (b, j, 0))
    kern = functools.partial(_merge_kernel, dv_gla=o_f.shape[2] // GLA_HEADS,
                             dv_diff=o_d.shape[2] // DIFF_HEADS, diff_scale=diff_scale)
    return pl.pallas_call(
        kern,
        out_shape=jax.ShapeDtypeStruct((b_, n_lat_tiles * tm, d), F32),
        grid=(b_, n_lat_tiles),
        in_specs=[
            row(d),
            pl.BlockSpec((1, 1, MOD_ROWS, d), lambda b, j: (b, 0, 0, 0)),
            row(o_f.shape[2]), row(o_b.shape[2]), row(sgg.shape[2]), row(o_d.shape[2]), row(smg.shape[2]),
            _resident(g_gla.shape), _resident(g_diff.shape),
            _resident(wa.shape), _resident(wb.shape), _resident(wo.shape),
        ],
        out_specs=row(d),
        compiler_params=_params(("parallel", "parallel")),
        name="merge",
    )(x1, modt, o_f, o_b, sgg, o_d, smg, g_gla, g_diff, wa, wb, wo)


def _rope_tables(seq, lp):
    half = DIFF_DH // 2
    quarter = half // 2
    inv = ROPE_BASE ** (-np.arange(quarter, dtype=np.float32) / quarter)
    t = jnp.arange(seq, dtype=jnp.int32)
    ang_row = (t // GRID_W).astype(F32)[:, None] * inv[None, :]
    ang_col = (t % GRID_W).astype(F32)[:, None] * inv[None, :]
    ang = jnp.concatenate([ang_row, ang_row, ang_col, ang_col], axis=1)
    sign = np.tile(np.concatenate([-np.ones(quarter), np.ones(quarter)]), 2).astype(np.float32)
    cos = jnp.tile(jnp.cos(ang), (1, LANES // DIFF_DH))
    sin = jnp.tile(jnp.sin(ang) * sign[None, :], (1, LANES // DIFF_DH))
    pad = lp - seq
    cos = jnp.concatenate([cos, jnp.ones((pad, LANES), F32)], axis=0)
    sin = jnp.concatenate([sin, jnp.zeros((pad, LANES), F32)], axis=0)
    return cos, sin


def kernel(x, c, ctx, c_ctx, w_ada, b_ada, ffn1_norm, ffn1_w1, ffn1_w3, ffn1_w2, mix_norm, w_in, gla_gate_w2, gla_gate_b, gla_out_norm, diff_lambda, diff_out_norm, w_branch_gla, w_branch_diff, w_out, ffn2_norm, ffn2_w1, ffn2_w3, ffn2_w2, final_norm):
    b_, seq, d = x.shape
    n_ctx = ctx.shape[1]
    assert w_ada.shape[0] == 1, "single-layer stack"
    tm = TOKEN_TILE
    assert seq % tm == 0 and seq % ATTN_TQ == 0 and n_ctx <= tm
    assert seq % GLA_CHUNK == 0 and n_ctx % GLA_CHUNK == 0 and (seq + n_ctx) % ATTN_TK == 0
    n_lat_tiles = seq // tm
    n_tiles = n_lat_tiles + 1
    lp = n_tiles * tm
    lam_init = 0.8 - 0.6 * math.exp(0.0)

    gla_qk = GLA_HEADS * (d // 8)
    gla_v = GLA_HEADS * (d // 4)
    dif = DIFF_HEADS * 2 * DIFF_DH
    splits = np.cumsum([gla_qk, gla_qk, gla_v, gla_v, 2 * GLA_RANK, dif, dif, dif, 2 * d])
    assert w_in.shape[2] == splits[-1]
    wi = w_in[0].astype(BF16)
    w_gq, w_gk, w_gv, w_gg, w_lr, w_dq, w_dk, w_dv, w_mg = jnp.split(wi, splits[:-1].tolist(), axis=1)
    w_lr = jnp.pad(w_lr, ((0, 0), (0, LANES - 2 * GLA_RANK)))
    w2cat = jnp.zeros((LANES, 2 * gla_qk), F32)
    w2cat = w2cat.at[:GLA_RANK, :gla_qk].set(gla_gate_w2[0, 0])
    w2cat = w2cat.at[GLA_RANK:2 * GLA_RANK, gla_qk:].set(gla_gate_w2[0, 1]).astype(BF16)
    b2cat = gla_gate_b[0].reshape(1, 2 * gla_qk)

    rows = -(-(b_ + 1) // 8) * 8
    c_rows = jnp.concatenate([c, c_ctx[None, :], jnp.zeros((rows - b_ - 1, d), F32)], axis=0)
    mods = _mod_table(c_rows, w_ada[0], b_ada).reshape(rows, N_MOD, d)
    modt = jnp.stack([mods[:b_], jnp.broadcast_to(mods[b_], (b_, N_MOD, d))], axis=1)
    modt = jnp.pad(modt, ((0, 0), (0, 0), (0, MOD_ROWS - N_MOD), (0, 0)))

    xs = jnp.concatenate([x, ctx, jnp.zeros((b_, lp - seq - n_ctx, d), F32)], axis=1)
    tiles = dict(n_lat_tiles=n_lat_tiles, n_tiles=n_tiles)

    x1 = _ffn(xs, modt, ffn1_norm, ffn1_w1[0].astype(BF16), ffn1_w3[0].astype(BF16),
              ffn1_w2[0].astype(BF16), row0=0, **tiles)

    gq, gk, gv, sgg, logf = _proj_gla(x1, modt, mix_norm, w_gq, w_gk, w_gv, w_gg, w_lr, w2cat, b2cat, **tiles)
    cos_t, sin_t = _rope_tables(seq, lp)
    dq, dk, dv, smg = _proj_dif(x1, modt, mix_norm, cos_t, sin_t, w_dq, w_dk, w_dv, w_mg, **tiles)

    o_f, o_b = _gla(gq, gk, gv, logf, n_lat_chunks=seq // GLA_CHUNK, n_ctx_chunks=n_ctx // GLA_CHUNK)

    lam_p = jnp.pad(diff_lambda[0], ((0, 4), (0, LANES - DIFF_DH)))
    o_d = _attn(dq, dk, dv, lam_p, seq=seq, n_keys=seq + n_ctx, lam_init=lam_init)

    x2 = _merge(x1, modt, o_f, o_b, sgg, o_d, smg, gla_out_norm, diff_out_norm,
                w_branch_gla[0].astype(BF16), w_branch_diff[0].astype(BF16), w_out[0].astype(BF16),
                n_lat_tiles=n_lat_tiles, diff_scale=1.0 - lam_init)

    return _ffn(x2, modt, ffn2_norm, ffn2_w1[0].astype(BF16), ffn2_w3[0].astype(BF16),
                ffn2_w2[0].astype(BF16), row0=6, n_lat_tiles=n_lat_tiles, n_tiles=n_lat_tiles,
                final_gain=final_norm.reshape(1, d))
```

```python
import functools
import math

import jax
import jax.numpy as jnp
import numpy as np
from jax import lax
from jax.experimental import pallas as pl
from jax.experimental.pallas import tpu as pltpu

F32 = jnp.float32
BF16 = jnp.bfloat16

EPS = 1e-6
GRID_W = 64
N_MOD = 9
MOD_ROWS = 16
GLA_HEADS = 4
GLA_RANK = 16
GLA_TAU = 16.0
GLA_CHUNK = 64
DIFF_HEADS = 8
DIFF_DH = 64
ROPE_BASE = 10000.0
LANES = 128
LOG2_E = 1.4426950408889634

TOKEN_TILE = 512
ATTN_TQ = 512
ATTN_TK = 256
ATTN_QGROUP = 256
VMEM_LIMIT = 52 * 1024 * 1024


def _resident(shape):
    zeros = (0,) * len(shape)
    return pl.BlockSpec(shape, lambda *_: zeros, pipeline_mode=pl.Buffered(1))


def _params(semantics):
    return pltpu.CompilerParams(dimension_semantics=semantics, vmem_limit_bytes=VMEM_LIMIT)


def _dot(a, b):
    return jnp.dot(a, b, preferred_element_type=F32)


def _dot_nt(a, b):
    return lax.dot_general(a, b, (((1,), (1,)), ((), ())), preferred_element_type=F32)


def _dot_tn(a, b):
    return lax.dot_general(a, b, (((0,), (0,)), ((), ())), preferred_element_type=F32)


def _sigmoid(x):
    return 1.0 / (1.0 + jnp.exp(-x))


def _rms(x):
    return x * lax.rsqrt(jnp.mean(x * x, axis=-1, keepdims=True) + EPS)


def _modnorm(x, gain, shift, scale1):
    return (_rms(x) * gain) * scale1 + shift


def _mod_kernel(c_ref, w_ref, b_ref, o_ref):
    k = pl.program_id(0)
    c = c_ref[...]
    s = c * _sigmoid(c)
    acc = jnp.zeros(o_ref.shape, F32)
    w = w_ref[...]
    s_parts = _split3(s)
    w_parts = _split3(w)
    for i in range(3):
        for j in range(3 - i):
            acc = acc + _dot(s_parts[i], w_parts[j])
    val = acc + b_ref[...]
    add = jnp.where(k % 3 == 1, 1.0, 0.0).astype(F32)
    mul = jnp.where((k == 2) | (k == 8), 0.5, 1.0).astype(F32)
    o_ref[...] = (val + add) * mul


def _split3(x):
    hi = x.astype(BF16)
    r1 = x - hi.astype(F32)
    mid = r1.astype(BF16)
    lo = (r1 - mid.astype(F32)).astype(BF16)
    return hi, mid, lo


def _mod_table(c_rows, w_ada, b_ada):
    rows, d = c_rows.shape
    return pl.pallas_call(
        _mod_kernel,
        out_shape=jax.ShapeDtypeStruct((rows, N_MOD * d), F32),
        grid=(N_MOD,),
        in_specs=[
            pl.BlockSpec((rows, d), lambda k: (0, 0)),
            pl.BlockSpec((d, d), lambda k: (0, k)),
            pl.BlockSpec((1, d), lambda k: (0, k)),
        ],
        out_specs=pl.BlockSpec((rows, d), lambda k: (0, k)),
        compiler_params=_params(("arbitrary",)),
        name="mod_table",
    )(c_rows, w_ada, b_ada)


def _ffn_kernel(x_ref, mod_ref, g_ref, w1_ref, w3_ref, w2_ref, *rest, row0, f_chunk, final):
    if final:
        gf_ref, o_ref = rest
    else:
        (o_ref,) = rest
    x = x_ref[0]
    mod = mod_ref[0, 0]
    shift, scale1, gate = mod[row0:row0 + 1], mod[row0 + 1:row0 + 2], mod[row0 + 2:row0 + 3]
    h = _modnorm(x, g_ref[...], shift, scale1).astype(BF16)
    acc = jnp.zeros(x.shape, F32)
    f_total = w1_ref.shape[1]
    for f0 in range(0, f_total, f_chunk):
        a = _dot(h, w1_ref[:, f0:f0 + f_chunk])
        b = _dot(h, w3_ref[:, f0:f0 + f_chunk])
        u = (a * _sigmoid(a) * b).astype(BF16)
        acc = acc + _dot(u, w2_ref[f0:f0 + f_chunk, :])
    y = x + gate * acc
    if final:
        y = _rms(y) * gf_ref[...]
    o_ref[0] = y


def _ffn(xs, modt, gain, w1, w3, w2, *, row0, n_lat_tiles, n_tiles, final_gain=None):
    b_, _, d = xs.shape
    f = w1.shape[1]
    tm = TOKEN_TILE
    final = final_gain is not None
    kern = functools.partial(_ffn_kernel, row0=row0, f_chunk=f // 2, final=final)
    in_specs = [
        pl.BlockSpec((1, tm, d), lambda b, j: (b, j, 0)),
        pl.BlockSpec((1, 1, MOD_ROWS, d), lambda b, j: (b, (j >= n_lat_tiles).astype(jnp.int32), 0, 0)),
        _resident((1, d)),
        _resident((d, f)),
        _resident((d, f)),
        _resident((f, d)),
    ]
    args = [xs, modt, gain, w1, w3, w2]
    if final:
        in_specs.append(_resident((1, d)))
        args.append(final_gain)
    return pl.pallas_call(
        kern,
        out_shape=jax.ShapeDtypeStruct((b_, n_tiles * tm, d), F32),
        grid=(b_, n_tiles),
        in_specs=in_specs,
        out_specs=pl.BlockSpec((1, tm, d), lambda b, j: (b, j, 0)),
        compiler_params=_params(("parallel", "parallel")),
        name="ffn_final" if final else "ffn",
    )(*args)


def _proj_gla_kernel(x_ref, mod_ref, g_ref, wq_ref, wk_ref, wv_ref, wg_ref, wlr_ref, w2_ref, b2_ref,
                     q_ref, k_ref, v_ref, sg_ref, lf_ref, *, q_scale):
    x = x_ref[0]
    mod = mod_ref[0, 0]
    h = _modnorm(x, g_ref[...], mod[3:4], mod[4:5]).astype(BF16)
    q_ref[0] = _dot(h, wq_ref[...]) * q_scale
    k_ref[0] = _dot(h, wk_ref[...])
    v_ref[0] = _dot(h, wv_ref[...]).astype(BF16)
    gg = _dot(h, wg_ref[...])
    sg_ref[0] = gg * _sigmoid(gg)
    lr = _dot(h, wlr_ref[...]).astype(BF16)
    z = _dot(lr, w2_ref[...]) + b2_ref[...]
    lf_ref[0] = (jnp.minimum(z, 0.0) - jnp.log(1.0 + jnp.exp(-jnp.abs(z)))) * (1.0 / GLA_TAU)


def _proj_gla(xs, modt, gain, wq, wk, wv, wg, wlr, w2cat, b2cat, *, n_lat_tiles, n_tiles):
    b_, lp, d = xs.shape
    tm = TOKEN_TILE
    qk = wq.shape[1]
    vv = wv.shape[1]
    row = lambda width: pl.BlockSpec((1, tm, width), lambda b, j: (b, j, 0))
    kern = functools.partial(_proj_gla_kernel, q_scale=(qk // GLA_HEADS) ** -0.5)
    return pl.pallas_call(
        kern,
        out_shape=(
            jax.ShapeDtypeStruct((b_, lp, qk), F32),
            jax.ShapeDtypeStruct((b_, lp, qk), F32),
            jax.ShapeDtypeStruct((b_, lp, vv), BF16),
            jax.ShapeDtypeStruct((b_, lp, vv), F32),
            jax.ShapeDtypeStruct((b_, lp, 2 * qk), F32),
        ),
        grid=(b_, n_tiles),
        in_specs=[
            row(d),
            pl.BlockSpec((1, 1, MOD_ROWS, d), lambda b, j: (b, (j >= n_lat_tiles).astype(jnp.int32), 0, 0)),
            _resident((1, d)),
            _resident(wq.shape), _resident(wk.shape), _resident(wv.shape), _resident(wg.shape),
            _resident(wlr.shape), _resident(w2cat.shape), _resident(b2cat.shape),
        ],
        out_specs=(row(qk), row(qk), row(vv), row(vv), row(2 * qk)),
        compiler_params=_params(("parallel", "parallel")),
        name="proj_gla",
    )(xs, modt, gain, wq, wk, wv, wg, wlr, w2cat, b2cat)


def _rope(u, cos, sin_signed, first_half):
    out = []
    for g in range(u.shape[1] // LANES):
        ug = u[:, g * LANES:(g + 1) * LANES]
        partner = jnp.where(first_half, pltpu.roll(ug, LANES - 16, 1), pltpu.roll(ug, 16, 1))
        out.append(ug * cos + partner * sin_signed)
    return jnp.concatenate(out, axis=1)


def _proj_dif_kernel(x_ref, mod_ref, g_ref, cos_ref, sin_ref, tab_t_ref, wq_t_ref, wk_ref, wv_t_ref, wm_ref,
                     q_t_ref, k_ref, v_t_ref, sm_ref, *, q_scale):
    x = x_ref[0]
    mod = mod_ref[0, 0]
    h = _modnorm(x, g_ref[...], mod[3:4], mod[4:5]).astype(BF16)
    q_t = _dot_nt(wq_t_ref[...], h)
    quarter = DIFF_DH // 4
    tab = tab_t_ref[...]
    for base in range(0, q_t.shape[0], 2 * quarter):
        t0 = base % DIFF_DH
        cos = tab[t0:t0 + quarter]
        sin = tab[t0 + quarter:t0 + 2 * quarter]
        u1 = q_t[base:base + quarter]
        u2 = q_t[base + quarter:base + 2 * quarter]
        q_t_ref[0, base:base + quarter, :] = ((u1 * cos - u2 * sin) * q_scale).astype(BF16)
        q_t_ref[0, base + quarter:base + 2 * quarter, :] = ((u2 * cos + u1 * sin) * q_scale).astype(BF16)
    cos = cos_ref[...]
    sin_signed = sin_ref[...]
    lane = lax.broadcasted_iota(jnp.int32, cos.shape, 1)
    first_half = (lane % 32) < 16
    k_ref[0] = _rope(_dot(h, wk_ref[...]), cos, sin_signed, first_half).astype(BF16)
    v_t_ref[0] = _dot_nt(wv_t_ref[...], h).astype(BF16)
    sm_ref[0] = _sigmoid(_dot(h, wm_ref[...]))


def _proj_dif(xs, modt, gain, cos_t, sin_t, tab_t, wq_t, wk, wv_t, wm, *, n_lat_tiles, n_tiles):
    b_, lp, d = xs.shape
    tm = TOKEN_TILE
    row = lambda width: pl.BlockSpec((1, tm, width), lambda b, j: (b, j, 0))
    col = lambda height: pl.BlockSpec((1, height, tm), lambda b, j: (b, 0, j))
    tab = pl.BlockSpec((tm, LANES), lambda b, j: (j, 0))
    kern = functools.partial(_proj_dif_kernel, q_scale=DIFF_DH ** -0.5 * LOG2_E)
    return pl.pallas_call(
        kern,
        out_shape=(
            jax.ShapeDtypeStruct((b_, wq_t.shape[0], lp), BF16),
            jax.ShapeDtypeStruct((b_, lp, wk.shape[1]), BF16),
            jax.ShapeDtypeStruct((b_, wv_t.shape[0], lp), BF16),
            jax.ShapeDtypeStruct((b_, lp, wm.shape[1]), F32),
        ),
        grid=(b_, n_tiles),
        in_specs=[
            row(d),
            pl.BlockSpec((1, 1, MOD_ROWS, d), lambda b, j: (b, (j >= n_lat_tiles).astype(jnp.int32), 0, 0)),
            _resident((1, d)),
            tab, tab,
            pl.BlockSpec((DIFF_DH, tm), lambda b, j: (0, j)),
            _resident(wq_t.shape), _resident(wk.shape), _resident(wv_t.shape), _resident(wm.shape),
        ],
        out_specs=(col(wq_t.shape[0]), row(wk.shape[1]), col(wv_t.shape[0]), row(wm.shape[1])),
        compiler_params=_params(("parallel", "parallel")),
        name="proj_dif",
    )(xs, modt, gain, cos_t, sin_t, tab_t, wq_t, wk, wv_t, wm)


def _gla_direction(q, k, v, logf, tri, strict_keep, ref_row, end_row, s_ref, o_ref):
    parts = _split3(logf)
    cum = _dot(tri, parts[0]) + _dot(tri, parts[1]) + _dot(tri, parts[2])
    ref = cum[ref_row:ref_row + 1]
    total = cum[end_row:end_row + 1]
    q_state = (q * jnp.exp(cum)).astype(BF16)
    q_in = (q * jnp.exp(cum - ref)).astype(BF16)
    k_in = (k * jnp.exp(ref - cum)).astype(BF16)
    k_out = (k * jnp.exp(total - cum)).astype(BF16)
    decay = jnp.exp(total)
    dk = q.shape[1] // GLA_HEADS
    dv = v.shape[1] // GLA_HEADS
    for hd in range(GLA_HEADS):
        ks = slice(hd * dk, (hd + 1) * dk)
        vs = slice(hd * dv, (hd + 1) * dv)
        state_t = s_ref[hd]
        vh = v[:, vs]
        attn = jnp.where(strict_keep, _dot_nt(q_in[:, ks], k_in[:, ks]), 0.0).astype(BF16)
        o_ref[0, :, vs] = _dot_nt(q_state[:, ks], state_t.astype(BF16)) + _dot(attn, vh)
        s_ref[hd] = state_t * decay[:, ks] + _dot_tn(vh, k_out[:, ks])


def _gla_kernel(qf_ref, kf_ref, vf_ref, lf_ref, qb_ref, kb_ref, vb_ref, lb_ref,
                of_ref, ob_ref, sf_ref, sb_ref):
    @pl.when(pl.program_id(1) == 0)
    def _():
        sf_ref[...] = jnp.zeros(sf_ref.shape, F32)
        sb_ref[...] = jnp.zeros(sb_ref.shape, F32)

    c = qf_ref.shape[1]
    row = lax.broadcasted_iota(jnp.int32, (c, c), 0)
    col = lax.broadcasted_iota(jnp.int32, (c, c), 1)
    lower = col <= row
    upper = col >= row
    _gla_direction(qf_ref[0], kf_ref[0], vf_ref[0], lf_ref[0],
                   jnp.where(lower, 1.0, 0.0).astype(BF16), lower, c // 2 - 1, c - 1, sf_ref, of_ref)
    _gla_direction(qb_ref[0], kb_ref[0], vb_ref[0], lb_ref[0],
                   jnp.where(upper, 1.0, 0.0).astype(BF16), upper, c // 2, 0, sb_ref, ob_ref)


def _gla(gq, gk, gv, logf, *, n_lat_chunks, n_ctx_chunks):
    b_, lp, qk = gq.shape
    vv = gv.shape[2]
    c = GLA_CHUNK
    n = n_lat_chunks + n_ctx_chunks
    fwd = lambda b, s: (b, (s + n_lat_chunks) % n, 0)
    bwd = lambda b, s: (b, n - 1 - s, 0)
    lf_spec = pl.BlockSpec((1, c, qk), lambda b, s: (b, (s + n_lat_chunks) % n, 0))
    lb_spec = pl.BlockSpec((1, c, qk), lambda b, s: (b, n - 1 - s, 1))
    return pl.pallas_call(
        _gla_kernel,
        out_shape=(jax.ShapeDtypeStruct((b_, lp, vv), F32), jax.ShapeDtypeStruct((b_, lp, vv), F32)),
        grid=(b_, n),
        in_specs=[
            pl.BlockSpec((1, c, qk), fwd), pl.BlockSpec((1, c, qk), fwd), pl.BlockSpec((1, c, vv), fwd), lf_spec,
            pl.BlockSpec((1, c, qk), bwd), pl.BlockSpec((1, c, qk), bwd), pl.BlockSpec((1, c, vv), bwd), lb_spec,
        ],
        out_specs=(pl.BlockSpec((1, c, vv), fwd), pl.BlockSpec((1, c, vv), bwd)),
        scratch_shapes=[
            pltpu.VMEM((GLA_HEADS, vv // GLA_HEADS, qk // GLA_HEADS), F32),
            pltpu.VMEM((GLA_HEADS, vv // GLA_HEADS, qk // GLA_HEADS), F32),
        ],
        compiler_params=_params(("parallel", "arbitrary")),
        name="gla_scan",
    )(gq, gk, gv, logf, gq, gk, gv, logf)


def _attn_kernel(q_t_ref, k_ref, v_t_ref, lam_ref, o_ref, m_ref, l_ref, acc_ref, s_ref, p_ref, a_ref,
                 *, tk, n_kblocks, lam_init):
    tq = q_t_ref.shape[2]
    q_t = q_t_ref[0]
    row = lax.broadcasted_iota(jnp.int32, q_t.shape, 0)
    zero = jnp.zeros_like(q_t)
    qz_t = jnp.concatenate([jnp.where(row < DIFF_DH, q_t, zero), jnp.where(row >= DIFF_DH, q_t, zero)], axis=1)
    m_ref[...] = jnp.full(m_ref.shape, -jnp.inf, F32)
    l_ref[...] = jnp.zeros(l_ref.shape, F32)
    acc_ref[...] = jnp.zeros(acc_ref.shape, F32)

    def scores(j, slot):
        kb = k_ref[0, pl.ds(pl.multiple_of(j * tk, tk), tk), :]
        s_ref[slot] = _dot(kb, qz_t)

    def softmax(slot):
        s_t = s_ref[slot]
        m_prev = m_ref[...]
        m_new = jnp.maximum(m_prev, jnp.max(s_t, axis=0, keepdims=True))
        alpha = jnp.exp2(m_prev - m_new)
        p_t = jnp.exp2(s_t - m_new)
        l_ref[...] = alpha * l_ref[...] + jnp.sum(p_t, axis=0, keepdims=True)
        m_ref[...] = m_new
        a_ref[slot] = alpha
        p_ref[slot] = p_t.astype(BF16)

    def weighted_values(j, slot):
        vb_t = v_t_ref[0, :, pl.ds(pl.multiple_of(j * tk, tk), tk)]
        acc_ref[...] = a_ref[slot] * acc_ref[...] + _dot(vb_t, p_ref[slot])

    def stage(j, slot, first=False, last=False):
        if not last:
            scores(j + 1, 1 - slot)
        if not first:
            weighted_values(j - 1, 1 - slot)
        softmax(slot)

    scores(0, 0)
    stage(0, 0, first=True, last=n_kblocks == 1)
    n_pairs = max(n_kblocks - 2, 0) // 2

    def pair(i, carry):
        stage(2 * i + 1, 1)
        stage(2 * i + 2, 0)
        return carry

    lax.fori_loop(0, n_pairs, pair, 0)
    for j in range(2 * n_pairs + 1, n_kblocks):
        stage(j, j % 2, last=j == n_kblocks - 1)
    weighted_values(n_kblocks - 1, (n_kblocks - 1) % 2)

    lp = lam_ref[...]
    lam = (jnp.exp(jnp.sum(lp[0:1] * lp[1:2], axis=-1, keepdims=True))
           - jnp.exp(jnp.sum(lp[2:3] * lp[3:4], axis=-1, keepdims=True)) + lam_init)
    o_t = acc_ref[...] / l_ref[...]
    o_ref[0] = (o_t[:, :tq] - lam * o_t[:, tq:]).T


def _attn(dq_t, dk, dv_t, lam_p, *, seq, n_keys, lam_init):
    b_, lp, width = dk.shape
    tq, tk = ATTN_TQ, ATTN_TK
    heads = width // LANES
    kern = functools.partial(_attn_kernel, tk=tk, n_kblocks=n_keys // tk, lam_init=lam_init)
    return pl.pallas_call(
        kern,
        out_shape=jax.ShapeDtypeStruct((b_, seq, width), F32),
        grid=(b_, heads, seq // tq),
        in_specs=[
            pl.BlockSpec((1, LANES, tq), lambda b, h, i: (b, h, i)),
            pl.BlockSpec((1, lp, LANES), lambda b, h, i: (b, 0, h)),
            pl.BlockSpec((1, LANES, lp), lambda b, h, i: (b, h, 0)),
            pl.BlockSpec(lam_p.shape, lambda b, h, i: (0, 0)),
        ],
        out_specs=pl.BlockSpec((1, tq, LANES), lambda b, h, i: (b, i, h)),
        scratch_shapes=[
            pltpu.VMEM((1, 2 * tq), F32),
            pltpu.VMEM((1, 2 * tq), F32),
            pltpu.VMEM((LANES, 2 * tq), F32),
            pltpu.VMEM((2, tk, 2 * tq), F32),
            pltpu.VMEM((2, tk, 2 * tq), BF16),
            pltpu.VMEM((2, 1, 2 * tq), F32),
        ],
        compiler_params=_params(("parallel", "parallel", "parallel")),
        name="diff_attn",
    )(dq_t, dk, dv_t, lam_p)


def _head_rms(x, gain, width):
    parts = []
    for h0 in range(0, x.shape[1], width):
        parts.append(_rms(x[:, h0:h0 + width]))
    return jnp.concatenate(parts, axis=1) * gain


def _merge_kernel(x_ref, mod_ref, of_ref, ob_ref, sg_ref, od_ref, sm_ref, gg_ref, gd_ref,
                  wa_ref, wb_ref, wo_ref, o_ref, *, dv_gla, dv_diff, diff_scale):
    d = x_ref.shape[2]
    ya = _head_rms(of_ref[0] + ob_ref[0], gg_ref[...], dv_gla) * sg_ref[0]
    ya = _dot(ya.astype(BF16), wa_ref[...])
    yb = _head_rms(od_ref[0], gd_ref[...], dv_diff) * diff_scale
    yb = _dot(yb.astype(BF16), wb_ref[...])
    sm = sm_ref[0]
    merged = (sm[:, :d] * ya + sm[:, d:] * yb).astype(BF16)
    gate = mod_ref[0, 0][5:6]
    o_ref[0] = x_ref[0] + gate * _dot(merged, wo_ref[...])


def _merge(x1, modt, o_f, o_b, sgg, o_d, smg, g_gla, g_diff, wa, wb, wo, *, n_lat_tiles, diff_scale):
    b_, _, d = x1.shape
    tm = TOKEN_TILE
    row = lambda width: pl.BlockSpec((1, tm, width), lambda b, j: (b, j, 0))
    kern = functools.partial(_merge_kernel, dv_gla=o_f.shape[2] // GLA_HEADS,
                             dv_diff=o_d.shape[2] // DIFF_HEADS, diff_scale=diff_scale)
    return pl.pallas_call(
        kern,
        out_shape=jax.ShapeDtypeStruct((b_, n_lat_tiles * tm, d), F32),
        grid=(b_, n_lat_tiles),
        in_specs=[
            row(d),
            pl.BlockSpec((1, 1, MOD_ROWS, d), lambda b, j: (b, 0, 0, 0)),
            row(o_f.shape[2]), row(o_b.shape[2]), row(sgg.shape[2]), row(o_d.shape[2]), row(smg.shape[2]),
            _resident(g_gla.shape), _resident(g_diff.shape),
            _resident(wa.shape), _resident(wb.shape), _resident(wo.shape),
        ],
        out_specs=row(d),
        compiler_params=_params(("parallel", "parallel")),
        name="merge",
    )(x1, modt, o_f, o_b, sgg, o_d, smg, g_gla, g_diff, wa, wb, wo)


def _rope_tables(seq, lp):
    half = DIFF_DH // 2
    quarter = half // 2
    inv = ROPE_BASE ** (-np.arange(quarter, dtype=np.float32) / quarter)
    t = jnp.arange(seq, dtype=jnp.int32)
    ang_row = (t // GRID_W).astype(F32)[:, None] * inv[None, :]
    ang_col = (t % GRID_W).astype(F32)[:, None] * inv[None, :]
    ang = jnp.concatenate([ang_row, ang_row, ang_col, ang_col], axis=1)
    sign = np.tile(np.concatenate([-np.ones(quarter), np.ones(quarter)]), 2).astype(np.float32)
    cos = jnp.tile(jnp.cos(ang), (1, LANES // DIFF_DH))
    sin = jnp.tile(jnp.sin(ang) * sign[None, :], (1, LANES // DIFF_DH))
    pad = lp - seq
    cos = jnp.concatenate([cos, jnp.ones((pad, LANES), F32)], axis=0)
    sin = jnp.concatenate([sin, jnp.zeros((pad, LANES), F32)], axis=0)
    tab_t = jnp.concatenate([jnp.cos(ang_row), jnp.sin(ang_row), jnp.cos(ang_col), jnp.sin(ang_col)], axis=1).T
    ident = np.repeat(np.array([1.0, 0.0, 1.0, 0.0], np.float32), quarter)[:, None]
    tab_t = jnp.concatenate([tab_t, jnp.broadcast_to(ident, (DIFF_DH, pad))], axis=1)
    return cos, sin, tab_t


def kernel(x, c, ctx, c_ctx, w_ada, b_ada, ffn1_norm, ffn1_w1, ffn1_w3, ffn1_w2, mix_norm, w_in, gla_gate_w2, gla_gate_b, gla_out_norm, diff_lambda, diff_out_norm, w_branch_gla, w_branch_diff, w_out, ffn2_norm, ffn2_w1, ffn2_w3, ffn2_w2, final_norm):
    b_, seq, d = x.shape
    n_ctx = ctx.shape[1]
    assert w_ada.shape[0] == 1, "single-layer stack"
    tm = TOKEN_TILE
    assert seq % tm == 0 and seq % ATTN_TQ == 0 and n_ctx <= tm
    assert seq % GLA_CHUNK == 0 and n_ctx % GLA_CHUNK == 0 and (seq + n_ctx) % ATTN_TK == 0
    n_lat_tiles = seq // tm
    n_tiles = n_lat_tiles + 1
    lp = n_tiles * tm
    lam_init = 0.8 - 0.6 * math.exp(0.0)

    gla_qk = GLA_HEADS * (d // 8)
    gla_v = GLA_HEADS * (d // 4)
    dif = DIFF_HEADS * 2 * DIFF_DH
    splits = np.cumsum([gla_qk, gla_qk, gla_v, gla_v, 2 * GLA_RANK, dif, dif, dif, 2 * d])
    assert w_in.shape[2] == splits[-1]
    wi = w_in[0].astype(BF16)
    w_gq, w_gk, w_gv, w_gg, w_lr, w_dq, w_dk, w_dv, w_mg = jnp.split(wi, splits[:-1].tolist(), axis=1)
    w_lr = jnp.pad(w_lr, ((0, 0), (0, LANES - 2 * GLA_RANK)))
    w2cat = jnp.zeros((LANES, 2 * gla_qk), F32)
    w2cat = w2cat.at[:GLA_RANK, :gla_qk].set(gla_gate_w2[0, 0])
    w2cat = w2cat.at[GLA_RANK:2 * GLA_RANK, gla_qk:].set(gla_gate_w2[0, 1]).astype(BF16)
    b2cat = gla_gate_b[0].reshape(1, 2 * gla_qk)

    rows = -(-(b_ + 1) // 8) * 8
    c_rows = jnp.concatenate([c, c_ctx[None, :], jnp.zeros((rows - b_ - 1, d), F32)], axis=0)
    mods = _mod_table(c_rows, w_ada[0], b_ada).reshape(rows, N_MOD, d)
    modt = jnp.stack([mods[:b_], jnp.broadcast_to(mods[b_], (b_, N_MOD, d))], axis=1)
    modt = jnp.pad(modt, ((0, 0), (0, 0), (0, MOD_ROWS - N_MOD), (0, 0)))

    xs = jnp.concatenate([x, ctx, jnp.zeros((b_, lp - seq - n_ctx, d), F32)], axis=1)
    tiles = dict(n_lat_tiles=n_lat_tiles, n_tiles=n_tiles)

    x1 = _ffn(xs, modt, ffn1_norm, ffn1_w1[0].astype(BF16), ffn1_w3[0].astype(BF16),
              ffn1_w2[0].astype(BF16), row0=0, **tiles)

    gq, gk, gv, sgg, logf = _proj_gla(x1, modt, mix_norm, w_gq, w_gk, w_gv, w_gg, w_lr, w2cat, b2cat, **tiles)
    cos_t, sin_t, tab_t = _rope_tables(seq, lp)
    dq, dk, dv, smg = _proj_dif(x1, modt, mix_norm, cos_t, sin_t, tab_t, w_dq.T, w_dk, w_dv.T, w_mg, **tiles)

    o_f, o_b = _gla(gq, gk, gv, logf, n_lat_chunks=seq // GLA_CHUNK, n_ctx_chunks=n_ctx // GLA_CHUNK)

    lam_p = jnp.pad(diff_lambda[0], ((0, 4), (0, LANES - DIFF_DH)))
    o_d = _attn(dq, dk, dv, lam_p, seq=seq, n_keys=seq + n_ctx, lam_init=lam_init)

    x2 = _merge(x1, modt, o_f, o_b, sgg, o_d, smg, gla_out_norm, diff_out_norm,
                w_branch_gla[0].astype(BF16), w_branch_diff[0].astype(BF16), w_out[0].astype(BF16),
                n_lat_tiles=n_lat_tiles, diff_scale=1.0 - lam_init)

    return _ffn(x2, modt, ffn2_norm, ffn2_w1[0].astype(BF16), ffn2_w3[0].astype(BF16),
                ffn2_w2[0].astype(BF16), row0=6, n_lat_tiles=n_lat_tiles, n_tiles=n_lat_tiles,
                final_gain=final_norm.reshape(1, d))
```

```python
import functools
import math

import jax
import jax.numpy as jnp
import numpy as np
from jax import lax
from jax.experimental import pallas as pl
from jax.experimental.pallas import tpu as pltpu

F32 = jnp.float32
BF16 = jnp.bfloat16

EPS = 1e-6
GRID_W = 64
N_MOD = 9
MOD_ROWS = 16
GLA_HEADS = 4
GLA_RANK = 16
GLA_TAU = 16.0
GLA_CHUNK = 64
DIFF_HEADS = 8
DIFF_DH = 64
ROPE_BASE = 10000.0
LANES = 128
BF16_ROWS = 16
LOG2_E = 1.4426950408889634

TOKEN_TILE = 512
ATTN_TQ = 512
ATTN_TK = 768
ATTN_QGROUP = 256
VMEM_LIMIT = 52 * 1024 * 1024


def _resident(shape):
    zeros = (0,) * len(shape)
    return pl.BlockSpec(shape, lambda *_: zeros, pipeline_mode=pl.Buffered(1))


def _params(semantics):
    return pltpu.CompilerParams(dimension_semantics=semantics, vmem_limit_bytes=VMEM_LIMIT)


def _dot(a, b):
    return jnp.dot(a, b, preferred_element_type=F32)


def _dot_nt(a, b):
    return lax.dot_general(a, b, (((1,), (1,)), ((), ())), preferred_element_type=F32)


def _dot_tn(a, b):
    return lax.dot_general(a, b, (((0,), (0,)), ((), ())), preferred_element_type=F32)


def _sigmoid(x):
    return 1.0 / (1.0 + jnp.exp(-x))


def _rms(x):
    return x * lax.rsqrt(jnp.mean(x * x, axis=-1, keepdims=True) + EPS)


def _modnorm(x, gain, shift, scale1):
    return (_rms(x) * gain) * scale1 + shift


def _mod_kernel(c_ref, w_ref, b_ref, o_ref):
    k = pl.program_id(0)
    c = c_ref[...]
    s = c * _sigmoid(c)
    acc = jnp.zeros(o_ref.shape, F32)
    w = w_ref[...]
    s_parts = _split3(s)
    w_parts = _split3(w)
    for i in range(3):
        for j in range(3 - i):
            acc = acc + _dot(s_parts[i], w_parts[j])
    val = acc + b_ref[...]
    add = jnp.where(k % 3 == 1, 1.0, 0.0).astype(F32)
    mul = jnp.where((k == 2) | (k == 8), 0.5, 1.0).astype(F32)
    o_ref[...] = (val + add) * mul


def _split3(x):
    hi = x.astype(BF16)
    r1 = x - hi.astype(F32)
    mid = r1.astype(BF16)
    lo = (r1 - mid.astype(F32)).astype(BF16)
    return hi, mid, lo


def _mod_table(c_rows, w_ada, b_ada):
    rows, d = c_rows.shape
    return pl.pallas_call(
        _mod_kernel,
        out_shape=jax.ShapeDtypeStruct((rows, N_MOD * d), F32),
        grid=(N_MOD,),
        in_specs=[
            pl.BlockSpec((rows, d), lambda k: (0, 0)),
            pl.BlockSpec((d, d), lambda k: (0, k)),
            pl.BlockSpec((1, d), lambda k: (0, k)),
        ],
        out_specs=pl.BlockSpec((rows, d), lambda k: (0, k)),
        compiler_params=_params(("arbitrary",)),
        name="mod_table",
    )(c_rows, w_ada, b_ada)


def _ffn_kernel(x_ref, mod_ref, g_ref, w1_ref, w3_ref, w2_ref, *rest, row0, f_chunk, final):
    if final:
        gf_ref, o_ref = rest
    else:
        (o_ref,) = rest
    x = x_ref[0]
    mod = mod_ref[0, 0]
    shift, scale1, gate = mod[row0:row0 + 1], mod[row0 + 1:row0 + 2], mod[row0 + 2:row0 + 3]
    h = _modnorm(x, g_ref[...], shift, scale1).astype(BF16)
    acc = jnp.zeros(x.shape, F32)
    f_total = w1_ref.shape[1]
    for f0 in range(0, f_total, f_chunk):
        a = _dot(h, w1_ref[:, f0:f0 + f_chunk])
        b = _dot(h, w3_ref[:, f0:f0 + f_chunk])
        u = (a * _sigmoid(a) * b).astype(BF16)
        acc = acc + _dot(u, w2_ref[f0:f0 + f_chunk, :])
    y = x + gate * acc
    if final:
        y = _rms(y) * gf_ref[...]
    o_ref[0] = y


def _ffn(xs, modt, gain, w1, w3, w2, *, row0, n_lat_tiles, n_tiles, final_gain=None):
    b_, _, d = xs.shape
    f = w1.shape[1]
    tm = TOKEN_TILE
    final = final_gain is not None
    kern = functools.partial(_ffn_kernel, row0=row0, f_chunk=f // 2, final=final)
    in_specs = [
        pl.BlockSpec((1, tm, d), lambda b, j: (b, j, 0)),
        pl.BlockSpec((1, 1, MOD_ROWS, d), lambda b, j: (b, (j >= n_lat_tiles).astype(jnp.int32), 0, 0)),
        _resident((1, d)),
        _resident((d, f)),
        _resident((d, f)),
        _resident((f, d)),
    ]
    args = [xs, modt, gain, w1, w3, w2]
    if final:
        in_specs.append(_resident((1, d)))
        args.append(final_gain)
    return pl.pallas_call(
        kern,
        out_shape=jax.ShapeDtypeStruct((b_, n_tiles * tm, d), F32),
        grid=(b_, n_tiles),
        in_specs=in_specs,
        out_specs=pl.BlockSpec((1, tm, d), lambda b, j: (b, j, 0)),
        compiler_params=_params(("parallel", "parallel")),
        name="ffn_final" if final else "ffn",
    )(*args)


def _proj_gla_kernel(x_ref, mod_ref, g_ref, wq_ref, wk_ref, wv_ref, wg_ref, wlr_ref, w2_ref, b2_ref,
                     q_ref, k_ref, v_ref, sg_ref, lf_ref, *, q_scale):
    x = x_ref[0]
    mod = mod_ref[0, 0]
    h = _modnorm(x, g_ref[...], mod[3:4], mod[4:5]).astype(BF16)
    q_ref[0] = _dot(h, wq_ref[...]) * q_scale
    k_ref[0] = _dot(h, wk_ref[...])
    v_ref[0] = _dot(h, wv_ref[...]).astype(BF16)
    gg = _dot(h, wg_ref[...])
    sg_ref[0] = gg * _sigmoid(gg)
    lr = _dot(h, wlr_ref[...]).astype(BF16)
    z = _dot(lr, w2_ref[...]) + b2_ref[...]
    lf_ref[0] = (jnp.minimum(z, 0.0) - jnp.log(1.0 + jnp.exp(-jnp.abs(z)))) * (1.0 / GLA_TAU)


def _proj_gla(xs, modt, gain, wq, wk, wv, wg, wlr, w2cat, b2cat, *, n_lat_tiles, n_tiles):
    b_, lp, d = xs.shape
    tm = TOKEN_TILE
    qk = wq.shape[1]
    vv = wv.shape[1]
    row = lambda width: pl.BlockSpec((1, tm, width), lambda b, j: (b, j, 0))
    kern = functools.partial(_proj_gla_kernel, q_scale=(qk // GLA_HEADS) ** -0.5)
    return pl.pallas_call(
        kern,
        out_shape=(
            jax.ShapeDtypeStruct((b_, lp, qk), F32),
            jax.ShapeDtypeStruct((b_, lp, qk), F32),
            jax.ShapeDtypeStruct((b_, lp, vv), BF16),
            jax.ShapeDtypeStruct((b_, lp, vv), F32),
            jax.ShapeDtypeStruct((b_, lp, 2 * qk), F32),
        ),
        grid=(b_, n_tiles),
        in_specs=[
            row(d),
            pl.BlockSpec((1, 1, MOD_ROWS, d), lambda b, j: (b, (j >= n_lat_tiles).astype(jnp.int32), 0, 0)),
            _resident((1, d)),
            _resident(wq.shape), _resident(wk.shape), _resident(wv.shape), _resident(wg.shape),
            _resident(wlr.shape), _resident(w2cat.shape), _resident(b2cat.shape),
        ],
        out_specs=(row(qk), row(qk), row(vv), row(vv), row(2 * qk)),
        compiler_params=_params(("parallel", "parallel")),
        name="proj_gla",
    )(xs, modt, gain, wq, wk, wv, wg, wlr, w2cat, b2cat)


def _rope(u, cos, sin_signed, first_half):
    out = []
    for g in range(u.shape[1] // LANES):
        ug = u[:, g * LANES:(g + 1) * LANES]
        partner = jnp.where(first_half, pltpu.roll(ug, LANES - 16, 1), pltpu.roll(ug, 16, 1))
        out.append(ug * cos + partner * sin_signed)
    return jnp.concatenate(out, axis=1)


def _proj_dif_kernel(x_ref, mod_ref, g_ref, cos_ref, sin_ref, tab_t_ref, wq_t_ref, wk_ref, wv_t_ref, wm_ref,
                     q_t_ref, k_ref, v_t_ref, sm_ref, *, q_scale):
    x = x_ref[0]
    mod = mod_ref[0, 0]
    h = _modnorm(x, g_ref[...], mod[3:4], mod[4:5]).astype(BF16)
    q_t = _dot_nt(wq_t_ref[...], h)
    quarter = DIFF_DH // 4
    tab = tab_t_ref[...]
    for base in range(0, q_t.shape[0], 2 * quarter):
        t0 = base % DIFF_DH
        cos = tab[t0:t0 + quarter]
        sin = tab[t0 + quarter:t0 + 2 * quarter]
        u1 = q_t[base:base + quarter]
        u2 = q_t[base + quarter:base + 2 * quarter]
        q_t_ref[0, base:base + quarter, :] = ((u1 * cos - u2 * sin) * q_scale).astype(BF16)
        q_t_ref[0, base + quarter:base + 2 * quarter, :] = ((u2 * cos + u1 * sin) * q_scale).astype(BF16)
    cos = cos_ref[...]
    sin_signed = sin_ref[...]
    lane = lax.broadcasted_iota(jnp.int32, cos.shape, 1)
    first_half = (lane % 32) < 16
    k_ref[0] = _rope(_dot(h, wk_ref[...]), cos, sin_signed, first_half).astype(BF16)
    v_t_ref[0] = _dot_nt(wv_t_ref[...], h).astype(BF16)
    sm_ref[0] = _sigmoid(_dot(h, wm_ref[...]))


def _proj_dif(xs, modt, gain, cos_t, sin_t, tab_t, wq_t, wk, wv_t, wm, *, n_lat_tiles, n_tiles):
    b_, lp, d = xs.shape
    tm = TOKEN_TILE
    row = lambda width: pl.BlockSpec((1, tm, width), lambda b, j: (b, j, 0))
    col = lambda height: pl.BlockSpec((1, height, tm), lambda b, j: (b, 0, j))
    tab = pl.BlockSpec((tm, LANES), lambda b, j: (j, 0))
    kern = functools.partial(_proj_dif_kernel, q_scale=DIFF_DH ** -0.5 * LOG2_E)
    return pl.pallas_call(
        kern,
        out_shape=(
            jax.ShapeDtypeStruct((b_, wq_t.shape[0], lp), BF16),
            jax.ShapeDtypeStruct((b_, lp, wk.shape[1]), BF16),
            jax.ShapeDtypeStruct((b_, wv_t.shape[0], lp), BF16),
            jax.ShapeDtypeStruct((b_, lp, wm.shape[1]), F32),
        ),
        grid=(b_, n_tiles),
        in_specs=[
            row(d),
            pl.BlockSpec((1, 1, MOD_ROWS, d), lambda b, j: (b, (j >= n_lat_tiles).astype(jnp.int32), 0, 0)),
            _resident((1, d)),
            tab, tab,
            pl.BlockSpec((DIFF_DH, tm), lambda b, j: (0, j)),
            _resident(wq_t.shape), _resident(wk.shape), _resident(wv_t.shape), _resident(wm.shape),
        ],
        out_specs=(col(wq_t.shape[0]), row(wk.shape[1]), col(wv_t.shape[0]), row(wm.shape[1])),
        compiler_params=_params(("parallel", "parallel")),
        name="proj_dif",
    )(xs, modt, gain, cos_t, sin_t, tab_t, wq_t, wk, wv_t, wm)


def _gla_direction(q, k, v, logf, tri, strict_keep, ref_row, end_row, s_ref, o_ref):
    parts = _split3(logf)
    cum = _dot(tri, parts[0]) + _dot(tri, parts[1]) + _dot(tri, parts[2])
    ref = cum[ref_row:ref_row + 1]
    total = cum[end_row:end_row + 1]
    q_state = (q * jnp.exp(cum)).astype(BF16)
    q_in = (q * jnp.exp(cum - ref)).astype(BF16)
    k_in = (k * jnp.exp(ref - cum)).astype(BF16)
    k_out = (k * jnp.exp(total - cum)).astype(BF16)
    decay = jnp.exp(total)
    dk = q.shape[1] // GLA_HEADS
    dv = v.shape[1] // GLA_HEADS
    for hd in range(GLA_HEADS):
        ks = slice(hd * dk, (hd + 1) * dk)
        vs = slice(hd * dv, (hd + 1) * dv)
        state_t = s_ref[hd]
        vh = v[:, vs]
        attn = jnp.where(strict_keep, _dot_nt(q_in[:, ks], k_in[:, ks]), 0.0).astype(BF16)
        o_ref[0, :, vs] = _dot_nt(q_state[:, ks], state_t.astype(BF16)) + _dot(attn, vh)
        s_ref[hd] = state_t * decay[:, ks] + _dot_tn(vh, k_out[:, ks])


def _gla_kernel(qf_ref, kf_ref, vf_ref, lf_ref, qb_ref, kb_ref, vb_ref, lb_ref,
                of_ref, ob_ref, sf_ref, sb_ref):
    @pl.when(pl.program_id(1) == 0)
    def _():
        sf_ref[...] = jnp.zeros(sf_ref.shape, F32)
        sb_ref[...] = jnp.zeros(sb_ref.shape, F32)

    c = qf_ref.shape[1]
    row = lax.broadcasted_iota(jnp.int32, (c, c), 0)
    col = lax.broadcasted_iota(jnp.int32, (c, c), 1)
    lower = col <= row
    upper = col >= row
    _gla_direction(qf_ref[0], kf_ref[0], vf_ref[0], lf_ref[0],
                   jnp.where(lower, 1.0, 0.0).astype(BF16), lower, c // 2 - 1, c - 1, sf_ref, of_ref)
    _gla_direction(qb_ref[0], kb_ref[0], vb_ref[0], lb_ref[0],
                   jnp.where(upper, 1.0, 0.0).astype(BF16), upper, c // 2, 0, sb_ref, ob_ref)


def _gla(gq, gk, gv, logf, *, n_lat_chunks, n_ctx_chunks):
    b_, lp, qk = gq.shape
    vv = gv.shape[2]
    c = GLA_CHUNK
    n = n_lat_chunks + n_ctx_chunks
    fwd = lambda b, s: (b, (s + n_lat_chunks) % n, 0)
    bwd = lambda b, s: (b, n - 1 - s, 0)
    lf_spec = pl.BlockSpec((1, c, qk), lambda b, s: (b, (s + n_lat_chunks) % n, 0))
    lb_spec = pl.BlockSpec((1, c, qk), lambda b, s: (b, n - 1 - s, 1))
    return pl.pallas_call(
        _gla_kernel,
        out_shape=(jax.ShapeDtypeStruct((b_, lp, vv), F32), jax.ShapeDtypeStruct((b_, lp, vv), F32)),
        grid=(b_, n),
        in_specs=[
            pl.BlockSpec((1, c, qk), fwd), pl.BlockSpec((1, c, qk), fwd), pl.BlockSpec((1, c, vv), fwd), lf_spec,
            pl.BlockSpec((1, c, qk), bwd), pl.BlockSpec((1, c, qk), bwd), pl.BlockSpec((1, c, vv), bwd), lb_spec,
        ],
        out_specs=(pl.BlockSpec((1, c, vv), fwd), pl.BlockSpec((1, c, vv), bwd)),
        scratch_shapes=[
            pltpu.VMEM((GLA_HEADS, vv // GLA_HEADS, qk // GLA_HEADS), F32),
            pltpu.VMEM((GLA_HEADS, vv // GLA_HEADS, qk // GLA_HEADS), F32),
        ],
        compiler_params=_params(("parallel", "arbitrary")),
        name="gla_scan",
    )(gq, gk, gv, logf, gq, gk, gv, logf)


def _attn_kernel(q_t_ref, k_ref, v_t_ref, lam_ref, o_ref, m_ref, acc_ref, s_ref, p_ref, a_ref,
                 *, tk, n_kblocks, lam_init):
    tq = q_t_ref.shape[2]
    q_t = q_t_ref[0]
    row = lax.broadcasted_iota(jnp.int32, q_t.shape, 0)
    zero = jnp.zeros_like(q_t)
    qz_t = jnp.concatenate([jnp.where(row < DIFF_DH, q_t, zero), jnp.where(row >= DIFF_DH, q_t, zero)], axis=1)
    m_ref[...] = jnp.full(m_ref.shape, -jnp.inf, F32)
    acc_ref[...] = jnp.zeros(acc_ref.shape, F32)

    groups = [slice(g0, g0 + ATTN_QGROUP) for g0 in range(0, 2 * tq, ATTN_QGROUP)]

    def scores(kb, slot, gs):
        s_ref[slot, :, gs] = _dot(kb, qz_t[:, gs])

    def softmax(slot, gs):
        s_t = s_ref[slot, :, gs]
        m_prev = m_ref[:, gs]
        m_new = jnp.maximum(m_prev, jnp.max(s_t, axis=0, keepdims=True))
        m_ref[:, gs] = m_new
        a_ref[slot, :, gs] = jnp.exp2(m_prev - m_new)
        p_ref[slot, :, gs] = jnp.exp2(s_t - m_new).astype(BF16)

    ones_rows = jnp.ones((BF16_ROWS, tk), BF16)

    def weighted_values(vb_t, slot, gs):
        acc_ref[:, gs] = a_ref[slot, :, gs] * acc_ref[:, gs] + _dot(vb_t, p_ref[slot, :, gs])

    def load_keys(j):
        return k_ref[0, pl.ds(pl.multiple_of(j * tk, tk), tk), :]

    def load_values(j):
        vb_t = v_t_ref[0, :, pl.ds(pl.multiple_of(j * tk, tk), tk)]
        return jnp.concatenate([vb_t, ones_rows], axis=0)

    def stage(j, slot, first=False, last=False):
        kb = None if last else load_keys(j + 1)
        vb_t = None if first else load_values(j - 1)
        for gs in groups:
            if not last:
                scores(kb, 1 - slot, gs)
            if not first:
                weighted_values(vb_t, 1 - slot, gs)
            softmax(slot, gs)

    kb0 = load_keys(0)
    for gs in groups:
        scores(kb0, 0, gs)
    stage(0, 0, first=True, last=n_kblocks == 1)
    n_pairs = max(n_kblocks - 2, 0) // 2

    def pair(i, carry):
        stage(2 * i + 1, 1)
        stage(2 * i + 2, 0)
        return carry

    lax.fori_loop(0, n_pairs, pair, 0)
    for j in range(2 * n_pairs + 1, n_kblocks):
        stage(j, j % 2, last=j == n_kblocks - 1)
    vb_last = load_values(n_kblocks - 1)
    for gs in groups:
        weighted_values(vb_last, (n_kblocks - 1) % 2, gs)

    lp = lam_ref[...]
    lam = (jnp.exp(jnp.sum(lp[0:1] * lp[1:2], axis=-1, keepdims=True))
           - jnp.exp(jnp.sum(lp[2:3] * lp[3:4], axis=-1, keepdims=True)) + lam_init)
    dv = v_t_ref.shape[1]
    o_t = acc_ref[:dv, :] / acc_ref[dv:dv + 1, :]
    o_ref[0] = (o_t[:, :tq] - lam * o_t[:, tq:]).T


def _attn(dq_t, dk, dv_t, lam_p, *, seq, n_keys, lam_init):
    b_, lp, width = dk.shape
    tq, tk = ATTN_TQ, ATTN_TK
    heads = width // LANES
    kern = functools.partial(_attn_kernel, tk=tk, n_kblocks=n_keys // tk, lam_init=lam_init)
    return pl.pallas_call(
        kern,
        out_shape=jax.ShapeDtypeStruct((b_, seq, width), F32),
        grid=(b_, heads, seq // tq),
        in_specs=[
            pl.BlockSpec((1, LANES, tq), lambda b, h, i: (b, h, i)),
            pl.BlockSpec((1, lp, LANES), lambda b, h, i: (b, 0, h)),
            pl.BlockSpec((1, LANES, lp), lambda b, h, i: (b, h, 0)),
            pl.BlockSpec(lam_p.shape, lambda b, h, i: (0, 0)),
        ],
        out_specs=pl.BlockSpec((1, tq, LANES), lambda b, h, i: (b, i, h)),
        scratch_shapes=[
            pltpu.VMEM((1, 2 * tq), F32),
            pltpu.VMEM((LANES + BF16_ROWS, 2 * tq), F32),
            pltpu.VMEM((2, tk, 2 * tq), F32),
            pltpu.VMEM((2, tk, 2 * tq), BF16),
            pltpu.VMEM((2, 1, 2 * tq), F32),
        ],
        compiler_params=_params(("parallel", "parallel", "parallel")),
        name="diff_attn",
    )(dq_t, dk, dv_t, lam_p)


def _head_rms(x, gain, width):
    parts = []
    for h0 in range(0, x.shape[1], width):
        parts.append(_rms(x[:, h0:h0 + width]))
    return jnp.concatenate(parts, axis=1) * gain


def _merge_kernel(x_ref, mod_ref, of_ref, ob_ref, sg_ref, od_ref, sm_ref, gg_ref, gd_ref,
                  wa_ref, wb_ref, wo_ref, o_ref, *, dv_gla, dv_diff, diff_scale):
    d = x_ref.shape[2]
    ya = _head_rms(of_ref[0] + ob_ref[0], gg_ref[...], dv_gla) * sg_ref[0]
    ya = _dot(ya.astype(BF16), wa_ref[...])
    yb = _head_rms(od_ref[0], gd_ref[...], dv_diff) * diff_scale
    yb = _dot(yb.astype(BF16), wb_ref[...])
    sm = sm_ref[0]
    merged = (sm[:, :d] * ya + sm[:, d:] * yb).astype(BF16)
    gate = mod_ref[0, 0][5:6]
    o_ref[0] = x_ref[0] + gate * _dot(merged, wo_ref[...])


def _merge(x1, modt, o_f, o_b, sgg, o_d, smg, g_gla, g_diff, wa, wb, wo, *, n_lat_tiles, diff_scale):
    b_, _, d = x1.shape
    tm = TOKEN_TILE
    row = lambda width: pl.BlockSpec((1, tm, width), lambda b, j: (b, j, 0))
    kern = functools.partial(_merge_kernel, dv_gla=o_f.shape[2] // GLA_HEADS,
                             dv_diff=o_d.shape[2] // DIFF_HEADS, diff_scale=diff_scale)
    return pl.pallas_call(
        kern,
        out_shape=jax.ShapeDtypeStruct((b_, n_lat_tiles * tm, d), F32),
        grid=(b_, n_lat_tiles),
        in_specs=[
            row(d),
            pl.BlockSpec((1, 1, MOD_ROWS, d), lambda b, j: (b, 0, 0, 0)),
            row(o_f.shape[2]), row(o_b.shape[2]), row(sgg.shape[2]), row(o_d.shape[2]), row(smg.shape[2]),
            _resident(g_gla.shape), _resident(g_diff.shape),
            _resident(wa.shape), _resident(wb.shape), _resident(wo.shape),
        ],
        out_specs=row(d),
        compiler_params=_params(("parallel", "parallel")),
        name="merge",
    )(x1, modt, o_f, o_b, sgg, o_d, smg, g_gla, g_diff, wa, wb, wo)


def _rope_tables(seq, lp):
    half = DIFF_DH // 2
    quarter = half // 2
    inv = ROPE_BASE ** (-np.arange(quarter, dtype=np.float32) / quarter)
    t = jnp.arange(seq, dtype=jnp.int32)
    ang_row = (t // GRID_W).astype(F32)[:, None] * inv[None, :]
    ang_col = (t % GRID_W).astype(F32)[:, None] * inv[None, :]
    ang = jnp.concatenate([ang_row, ang_row, ang_col, ang_col], axis=1)
    sign = np.tile(np.concatenate([-np.ones(quarter), np.ones(quarter)]), 2).astype(np.float32)
    cos = jnp.tile(jnp.cos(ang), (1, LANES // DIFF_DH))
    sin = jnp.tile(jnp.sin(ang) * sign[None, :], (1, LANES // DIFF_DH))
    pad = lp - seq
    cos = jnp.concatenate([cos, jnp.ones((pad, LANES), F32)], axis=0)
    sin = jnp.concatenate([sin, jnp.zeros((pad, LANES), F32)], axis=0)
    tab_t = jnp.concatenate([jnp.cos(ang_row), jnp.sin(ang_row), jnp.cos(ang_col), jnp.sin(ang_col)], axis=1).T
    ident = np.repeat(np.array([1.0, 0.0, 1.0, 0.0], np.float32), quarter)[:, None]
    tab_t = jnp.concatenate([tab_t, jnp.broadcast_to(ident, (DIFF_DH, pad))], axis=1)
    return cos, sin, tab_t


def kernel(x, c, ctx, c_ctx, w_ada, b_ada, ffn1_norm, ffn1_w1, ffn1_w3, ffn1_w2, mix_norm, w_in, gla_gate_w2, gla_gate_b, gla_out_norm, diff_lambda, diff_out_norm, w_branch_gla, w_branch_diff, w_out, ffn2_norm, ffn2_w1, ffn2_w3, ffn2_w2, final_norm):
    b_, seq, d = x.shape
    n_ctx = ctx.shape[1]
    assert w_ada.shape[0] == 1, "single-layer stack"
    tm = TOKEN_TILE
    assert seq % tm == 0 and seq % ATTN_TQ == 0 and n_ctx <= tm
    assert seq % GLA_CHUNK == 0 and n_ctx % GLA_CHUNK == 0 and (seq + n_ctx) % ATTN_TK == 0
    n_lat_tiles = seq // tm
    n_tiles = n_lat_tiles + 1
    lp = n_tiles * tm
    lam_init = 0.8 - 0.6 * math.exp(0.0)

    gla_qk = GLA_HEADS * (d // 8)
    gla_v = GLA_HEADS * (d // 4)
    dif = DIFF_HEADS * 2 * DIFF_DH
    splits = np.cumsum([gla_qk, gla_qk, gla_v, gla_v, 2 * GLA_RANK, dif, dif, dif, 2 * d])
    assert w_in.shape[2] == splits[-1]
    wi = w_in[0].astype(BF16)
    w_gq, w_gk, w_gv, w_gg, w_lr, w_dq, w_dk, w_dv, w_mg = jnp.split(wi, splits[:-1].tolist(), axis=1)
    w_lr = jnp.pad(w_lr, ((0, 0), (0, LANES - 2 * GLA_RANK)))
    w2cat = jnp.zeros((LANES, 2 * gla_qk), F32)
    w2cat = w2cat.at[:GLA_RANK, :gla_qk].set(gla_gate_w2[0, 0])
    w2cat = w2cat.at[GLA_RANK:2 * GLA_RANK, gla_qk:].set(gla_gate_w2[0, 1]).astype(BF16)
    b2cat = gla_gate_b[0].reshape(1, 2 * gla_qk)

    rows = -(-(b_ + 1) // 8) * 8
    c_rows = jnp.concatenate([c, c_ctx[None, :], jnp.zeros((rows - b_ - 1, d), F32)], axis=0)
    mods = _mod_table(c_rows, w_ada[0], b_ada).reshape(rows, N_MOD, d)
    modt = jnp.stack([mods[:b_], jnp.broadcast_to(mods[b_], (b_, N_MOD, d))], axis=1)
    modt = jnp.pad(modt, ((0, 0), (0, 0), (0, MOD_ROWS - N_MOD), (0, 0)))

    xs = jnp.concatenate([x, ctx, jnp.zeros((b_, lp - seq - n_ctx, d), F32)], axis=1)
    tiles = dict(n_lat_tiles=n_lat_tiles, n_tiles=n_tiles)

    x1 = _ffn(xs, modt, ffn1_norm, ffn1_w1[0].astype(BF16), ffn1_w3[0].astype(BF16),
              ffn1_w2[0].astype(BF16), row0=0, **tiles)

    gq, gk, gv, sgg, logf = _proj_gla(x1, modt, mix_norm, w_gq, w_gk, w_gv, w_gg, w_lr, w2cat, b2cat, **tiles)
    cos_t, sin_t, tab_t = _rope_tables(seq, lp)
    dq, dk, dv, smg = _proj_dif(x1, modt, mix_norm, cos_t, sin_t, tab_t, w_dq.T, w_dk, w_dv.T, w_mg, **tiles)

    o_f, o_b = _gla(gq, gk, gv, logf, n_lat_chunks=seq // GLA_CHUNK, n_ctx_chunks=n_ctx // GLA_CHUNK)

    lam_p = jnp.pad(diff_lambda[0], ((0, 4), (0, LANES - DIFF_DH)))
    o_d = _attn(dq, dk, dv, lam_p, seq=seq, n_keys=seq + n_ctx, lam_init=lam_init)

    x2 = _merge(x1, modt, o_f, o_b, sgg, o_d, smg, gla_out_norm, diff_out_norm,
                w_branch_gla[0].astype(BF16), w_branch_diff[0].astype(BF16), w_out[0].astype(BF16),
                n_lat_tiles=n_lat_tiles, diff_scale=1.0 - lam_init)

    return _ffn(x2, modt, ffn2_norm, ffn2_w1[0].astype(BF16), ffn2_w3[0].astype(BF16),
                ffn2_w2[0].astype(BF16), row0=6, n_lat_tiles=n_lat_tiles, n_tiles=n_lat_tiles,
                final_gain=final_norm.reshape(1, d))
```

```python
import functools
import math

import jax
import jax.numpy as jnp
import numpy as np
from jax import lax
from jax.experimental import pallas as pl
from jax.experimental.pallas import tpu as pltpu

F32 = jnp.float32
BF16 = jnp.bfloat16

EPS = 1e-6
GRID_W = 64
N_MOD = 9
MOD_ROWS = 16
GLA_HEADS = 4
GLA_RANK = 16
GLA_TAU = 16.0
GLA_CHUNK = 64
DIFF_HEADS = 8
DIFF_DH = 64
ROPE_BASE = 10000.0
LANES = 128
BF16_ROWS = 16
LOG2_E = 1.4426950408889634

TOKEN_TILE = 512
ATTN_TQ = 1024
ATTN_TK = 768
ATTN_QGROUP = 256
VMEM_LIMIT = 52 * 1024 * 1024


def _resident(shape):
    zeros = (0,) * len(shape)
    return pl.BlockSpec(shape, lambda *_: zeros, pipeline_mode=pl.Buffered(1))


def _params(semantics):
    return pltpu.CompilerParams(dimension_semantics=semantics, vmem_limit_bytes=VMEM_LIMIT)


def _dot(a, b):
    return jnp.dot(a, b, preferred_element_type=F32)


def _dot_nt(a, b):
    return lax.dot_general(a, b, (((1,), (1,)), ((), ())), preferred_element_type=F32)


def _dot_tn(a, b):
    return lax.dot_general(a, b, (((0,), (0,)), ((), ())), preferred_element_type=F32)


def _sigmoid(x):
    return 1.0 / (1.0 + jnp.exp(-x))


def _rms(x):
    return x * lax.rsqrt(jnp.mean(x * x, axis=-1, keepdims=True) + EPS)


def _modnorm(x, gain, shift, scale1):
    return (_rms(x) * gain) * scale1 + shift


def _mod_kernel(c_ref, w_ref, b_ref, o_ref):
    k = pl.program_id(0)
    c = c_ref[...]
    s = c * _sigmoid(c)
    acc = jnp.zeros(o_ref.shape, F32)
    w = w_ref[...]
    s_parts = _split3(s)
    w_parts = _split3(w)
    for i in range(3):
        for j in range(3 - i):
            acc = acc + _dot(s_parts[i], w_parts[j])
    val = acc + b_ref[...]
    add = jnp.where(k % 3 == 1, 1.0, 0.0).astype(F32)
    mul = jnp.where((k == 2) | (k == 8), 0.5, 1.0).astype(F32)
    o_ref[...] = (val + add) * mul


def _split3(x):
    hi = x.astype(BF16)
    r1 = x - hi.astype(F32)
    mid = r1.astype(BF16)
    lo = (r1 - mid.astype(F32)).astype(BF16)
    return hi, mid, lo


def _mod_table(c_rows, w_ada, b_ada):
    rows, d = c_rows.shape
    return pl.pallas_call(
        _mod_kernel,
        out_shape=jax.ShapeDtypeStruct((rows, N_MOD * d), F32),
        grid=(N_MOD,),
        in_specs=[
            pl.BlockSpec((rows, d), lambda k: (0, 0)),
            pl.BlockSpec((d, d), lambda k: (0, k)),
            pl.BlockSpec((1, d), lambda k: (0, k)),
        ],
        out_specs=pl.BlockSpec((rows, d), lambda k: (0, k)),
        compiler_params=_params(("arbitrary",)),
        name="mod_table",
    )(c_rows, w_ada, b_ada)


def _ffn_kernel(x_ref, mod_ref, g_ref, w1_ref, w3_ref, w2_ref, *rest, row0, f_chunk, final):
    if final:
        gf_ref, o_ref = rest
    else:
        (o_ref,) = rest
    x = x_ref[0]
    mod = mod_ref[0, 0]
    shift, scale1, gate = mod[row0:row0 + 1], mod[row0 + 1:row0 + 2], mod[row0 + 2:row0 + 3]
    h = _modnorm(x, g_ref[...], shift, scale1).astype(BF16)
    acc = jnp.zeros(x.shape, F32)
    f_total = w1_ref.shape[1]
    for f0 in range(0, f_total, f_chunk):
        a = _dot(h, w1_ref[:, f0:f0 + f_chunk])
        b = _dot(h, w3_ref[:, f0:f0 + f_chunk])
        u = (a * _sigmoid(a) * b).astype(BF16)
        acc = acc + _dot(u, w2_ref[f0:f0 + f_chunk, :])
    y = x + gate * acc
    if final:
        y = _rms(y) * gf_ref[...]
    o_ref[0] = y


def _ffn(xs, modt, gain, w1, w3, w2, *, row0, n_lat_tiles, n_tiles, final_gain=None):
    b_, _, d = xs.shape
    f = w1.shape[1]
    tm = TOKEN_TILE
    final = final_gain is not None
    kern = functools.partial(_ffn_kernel, row0=row0, f_chunk=f // 2, final=final)
    in_specs = [
        pl.BlockSpec((1, tm, d), lambda b, j: (b, j, 0)),
        pl.BlockSpec((1, 1, MOD_ROWS, d), lambda b, j: (b, (j >= n_lat_tiles).astype(jnp.int32), 0, 0)),
        _resident((1, d)),
        _resident((d, f)),
        _resident((d, f)),
        _resident((f, d)),
    ]
    args = [xs, modt, gain, w1, w3, w2]
    if final:
        in_specs.append(_resident((1, d)))
        args.append(final_gain)
    return pl.pallas_call(
        kern,
        out_shape=jax.ShapeDtypeStruct((b_, n_tiles * tm, d), F32),
        grid=(b_, n_tiles),
        in_specs=in_specs,
        out_specs=pl.BlockSpec((1, tm, d), lambda b, j: (b, j, 0)),
        compiler_params=_params(("parallel", "parallel")),
        name="ffn_final" if final else "ffn",
    )(*args)


def _proj_gla_kernel(x_ref, mod_ref, g_ref, wq_ref, wk_ref, wv_ref, wg_ref, wlr_ref, w2_ref, b2_ref,
                     q_ref, k_ref, v_ref, sg_ref, lf_ref, *, q_scale):
    x = x_ref[0]
    mod = mod_ref[0, 0]
    h = _modnorm(x, g_ref[...], mod[3:4], mod[4:5]).astype(BF16)
    q_ref[0] = _dot(h, wq_ref[...]) * q_scale
    k_ref[0] = _dot(h, wk_ref[...])
    v_ref[0] = _dot(h, wv_ref[...]).astype(BF16)
    gg = _dot(h, wg_ref[...])
    sg_ref[0] = gg * _sigmoid(gg)
    lr = _dot(h, wlr_ref[...]).astype(BF16)
    z = _dot(lr, w2_ref[...]) + b2_ref[...]
    lf_ref[0] = (jnp.minimum(z, 0.0) - jnp.log(1.0 + jnp.exp(-jnp.abs(z)))) * (1.0 / GLA_TAU)


def _proj_gla(xs, modt, gain, wq, wk, wv, wg, wlr, w2cat, b2cat, *, n_lat_tiles, n_tiles):
    b_, lp, d = xs.shape
    tm = TOKEN_TILE
    qk = wq.shape[1]
    vv = wv.shape[1]
    row = lambda width: pl.BlockSpec((1, tm, width), lambda b, j: (b, j, 0))
    kern = functools.partial(_proj_gla_kernel, q_scale=(qk // GLA_HEADS) ** -0.5)
    return pl.pallas_call(
        kern,
        out_shape=(
            jax.ShapeDtypeStruct((b_, lp, qk), F32),
            jax.ShapeDtypeStruct((b_, lp, qk), F32),
            jax.ShapeDtypeStruct((b_, lp, vv), BF16),
            jax.ShapeDtypeStruct((b_, lp, vv), F32),
            jax.ShapeDtypeStruct((b_, lp, 2 * qk), F32),
        ),
        grid=(b_, n_tiles),
        in_specs=[
            row(d),
            pl.BlockSpec((1, 1, MOD_ROWS, d), lambda b, j: (b, (j >= n_lat_tiles).astype(jnp.int32), 0, 0)),
            _resident((1, d)),
            _resident(wq.shape), _resident(wk.shape), _resident(wv.shape), _resident(wg.shape),
            _resident(wlr.shape), _resident(w2cat.shape), _resident(b2cat.shape),
        ],
        out_specs=(row(qk), row(qk), row(vv), row(vv), row(2 * qk)),
        compiler_params=_params(("parallel", "parallel")),
        name="proj_gla",
    )(xs, modt, gain, wq, wk, wv, wg, wlr, w2cat, b2cat)


def _rope(u, cos, sin_signed, first_half):
    out = []
    for g in range(u.shape[1] // LANES):
        ug = u[:, g * LANES:(g + 1) * LANES]
        partner = jnp.where(first_half, pltpu.roll(ug, LANES - 16, 1), pltpu.roll(ug, 16, 1))
        out.append(ug * cos + partner * sin_signed)
    return jnp.concatenate(out, axis=1)


def _proj_dif_kernel(x_ref, mod_ref, g_ref, cos_ref, sin_ref, tab_t_ref, wq_t_ref, wk_ref, wv_t_ref, wm_ref,
                     q_t_ref, k_ref, v_t_ref, sm_ref, *, q_scale):
    x = x_ref[0]
    mod = mod_ref[0, 0]
    h = _modnorm(x, g_ref[...], mod[3:4], mod[4:5]).astype(BF16)
    q_t = _dot_nt(wq_t_ref[...], h)
    quarter = DIFF_DH // 4
    tab = tab_t_ref[...]
    for base in range(0, q_t.shape[0], 2 * quarter):
        t0 = base % DIFF_DH
        cos = tab[t0:t0 + quarter]
        sin = tab[t0 + quarter:t0 + 2 * quarter]
        u1 = q_t[base:base + quarter]
        u2 = q_t[base + quarter:base + 2 * quarter]
        q_t_ref[0, base:base + quarter, :] = ((u1 * cos - u2 * sin) * q_scale).astype(BF16)
        q_t_ref[0, base + quarter:base + 2 * quarter, :] = ((u2 * cos + u1 * sin) * q_scale).astype(BF16)
    cos = cos_ref[...]
    sin_signed = sin_ref[...]
    lane = lax.broadcasted_iota(jnp.int32, cos.shape, 1)
    first_half = (lane % 32) < 16
    k_ref[0] = _rope(_dot(h, wk_ref[...]), cos, sin_signed, first_half).astype(BF16)
    v_t_ref[0] = _dot_nt(wv_t_ref[...], h).astype(BF16)
    sm_ref[0] = _sigmoid(_dot(h, wm_ref[...]))


def _proj_dif(xs, modt, gain, cos_t, sin_t, tab_t, wq_t, wk, wv_t, wm, *, n_lat_tiles, n_tiles):
    b_, lp, d = xs.shape
    tm = TOKEN_TILE
    row = lambda width: pl.BlockSpec((1, tm, width), lambda b, j: (b, j, 0))
    col = lambda height: pl.BlockSpec((1, height, tm), lambda b, j: (b, 0, j))
    tab = pl.BlockSpec((tm, LANES), lambda b, j: (j, 0))
    kern = functools.partial(_proj_dif_kernel, q_scale=DIFF_DH ** -0.5 * LOG2_E)
    return pl.pallas_call(
        kern,
        out_shape=(
            jax.ShapeDtypeStruct((b_, wq_t.shape[0], lp), BF16),
            jax.ShapeDtypeStruct((b_, lp, wk.shape[1]), BF16),
            jax.ShapeDtypeStruct((b_, wv_t.shape[0], lp), BF16),
            jax.ShapeDtypeStruct((b_, lp, wm.shape[1]), F32),
        ),
        grid=(b_, n_tiles),
        in_specs=[
            row(d),
            pl.BlockSpec((1, 1, MOD_ROWS, d), lambda b, j: (b, (j >= n_lat_tiles).astype(jnp.int32), 0, 0)),
            _resident((1, d)),
            tab, tab,
            pl.BlockSpec((DIFF_DH, tm), lambda b, j: (0, j)),
            _resident(wq_t.shape), _resident(wk.shape), _resident(wv_t.shape), _resident(wm.shape),
        ],
        out_specs=(col(wq_t.shape[0]), row(wk.shape[1]), col(wv_t.shape[0]), row(wm.shape[1])),
        compiler_params=_params(("parallel", "parallel")),
        name="proj_dif",
    )(xs, modt, gain, cos_t, sin_t, tab_t, wq_t, wk, wv_t, wm)


def _gla_direction(q, k, v, logf, tri, strict_keep, ref_row, end_row, s_ref, o_ref):
    parts = _split3(logf)
    cum = _dot(tri, parts[0]) + _dot(tri, parts[1]) + _dot(tri, parts[2])
    ref = cum[ref_row:ref_row + 1]
    total = cum[end_row:end_row + 1]
    q_state = (q * jnp.exp(cum)).astype(BF16)
    q_in = (q * jnp.exp(cum - ref)).astype(BF16)
    k_in = (k * jnp.exp(ref - cum)).astype(BF16)
    k_out = (k * jnp.exp(total - cum)).astype(BF16)
    decay = jnp.exp(total)
    dk = q.shape[1] // GLA_HEADS
    dv = v.shape[1] // GLA_HEADS
    for hd in range(GLA_HEADS):
        ks = slice(hd * dk, (hd + 1) * dk)
        vs = slice(hd * dv, (hd + 1) * dv)
        state_t = s_ref[hd]
        vh = v[:, vs]
        attn = jnp.where(strict_keep, _dot_nt(q_in[:, ks], k_in[:, ks]), 0.0).astype(BF16)
        o_ref[0, :, vs] = _dot_nt(q_state[:, ks], state_t.astype(BF16)) + _dot(attn, vh)
        s_ref[hd] = state_t * decay[:, ks] + _dot_tn(vh, k_out[:, ks])


def _gla_kernel(qf_ref, kf_ref, vf_ref, lf_ref, qb_ref, kb_ref, vb_ref, lb_ref,
                of_ref, ob_ref, sf_ref, sb_ref):
    @pl.when(pl.program_id(1) == 0)
    def _():
        sf_ref[...] = jnp.zeros(sf_ref.shape, F32)
        sb_ref[...] = jnp.zeros(sb_ref.shape, F32)

    c = qf_ref.shape[1]
    row = lax.broadcasted_iota(jnp.int32, (c, c), 0)
    col = lax.broadcasted_iota(jnp.int32, (c, c), 1)
    lower = col <= row
    upper = col >= row
    _gla_direction(qf_ref[0], kf_ref[0], vf_ref[0], lf_ref[0],
                   jnp.where(lower, 1.0, 0.0).astype(BF16), lower, c // 2 - 1, c - 1, sf_ref, of_ref)
    _gla_direction(qb_ref[0], kb_ref[0], vb_ref[0], lb_ref[0],
                   jnp.where(upper, 1.0, 0.0).astype(BF16), upper, c // 2, 0, sb_ref, ob_ref)


def _gla(gq, gk, gv, logf, *, n_lat_chunks, n_ctx_chunks):
    b_, lp, qk = gq.shape
    vv = gv.shape[2]
    c = GLA_CHUNK
    n = n_lat_chunks + n_ctx_chunks
    fwd = lambda b, s: (b, (s + n_lat_chunks) % n, 0)
    bwd = lambda b, s: (b, n - 1 - s, 0)
    lf_spec = pl.BlockSpec((1, c, qk), lambda b, s: (b, (s + n_lat_chunks) % n, 0))
    lb_spec = pl.BlockSpec((1, c, qk), lambda b, s: (b, n - 1 - s, 1))
    return pl.pallas_call(
        _gla_kernel,
        out_shape=(jax.ShapeDtypeStruct((b_, lp, vv), F32), jax.ShapeDtypeStruct((b_, lp, vv), F32)),
        grid=(b_, n),
        in_specs=[
            pl.BlockSpec((1, c, qk), fwd), pl.BlockSpec((1, c, qk), fwd), pl.BlockSpec((1, c, vv), fwd), lf_spec,
            pl.BlockSpec((1, c, qk), bwd), pl.BlockSpec((1, c, qk), bwd), pl.BlockSpec((1, c, vv), bwd), lb_spec,
        ],
        out_specs=(pl.BlockSpec((1, c, vv), fwd), pl.BlockSpec((1, c, vv), bwd)),
        scratch_shapes=[
            pltpu.VMEM((GLA_HEADS, vv // GLA_HEADS, qk // GLA_HEADS), F32),
            pltpu.VMEM((GLA_HEADS, vv // GLA_HEADS, qk // GLA_HEADS), F32),
        ],
        compiler_params=_params(("parallel", "arbitrary")),
        name="gla_scan",
    )(gq, gk, gv, logf, gq, gk, gv, logf)


def _attn_kernel(q_t_ref, k_ref, v_t_ref, lam_ref, o_ref, m_ref, acc_ref, s_ref, p_ref, a_ref,
                 *, tk, n_kblocks, lam_init):
    tq = q_t_ref.shape[2]
    q_t = q_t_ref[0]
    row = lax.broadcasted_iota(jnp.int32, q_t.shape, 0)
    zero = jnp.zeros_like(q_t)
    qz_t = jnp.concatenate([jnp.where(row < DIFF_DH, q_t, zero), jnp.where(row >= DIFF_DH, q_t, zero)], axis=1)
    m_ref[...] = jnp.full(m_ref.shape, -jnp.inf, F32)
    acc_ref[...] = jnp.zeros(acc_ref.shape, F32)

    groups = [slice(g0, g0 + ATTN_QGROUP) for g0 in range(0, 2 * tq, ATTN_QGROUP)]

    def scores(kb, slot, gs):
        s_ref[slot, :, gs] = _dot(kb, qz_t[:, gs])

    def softmax(slot, gs):
        s_t = s_ref[slot, :, gs]
        m_prev = m_ref[:, gs]
        m_new = jnp.maximum(m_prev, jnp.max(s_t, axis=0, keepdims=True))
        m_ref[:, gs] = m_new
        a_ref[slot, :, gs] = jnp.exp2(m_prev - m_new)
        p_ref[slot, :, gs] = jnp.exp2(s_t - m_new).astype(BF16)

    ones_rows = jnp.ones((BF16_ROWS, tk), BF16)

    def weighted_values(vb_t, slot, gs):
        acc_ref[:, gs] = a_ref[slot, :, gs] * acc_ref[:, gs] + _dot(vb_t, p_ref[slot, :, gs])

    def load_keys(j):
        return k_ref[0, pl.ds(pl.multiple_of(j * tk, tk), tk), :]

    def load_values(j):
        vb_t = v_t_ref[0, :, pl.ds(pl.multiple_of(j * tk, tk), tk)]
        return jnp.concatenate([vb_t, ones_rows], axis=0)

    def stage(j, slot, first=False, last=False):
        kb = None if last else load_keys(j + 1)
        vb_t = None if first else load_values(j - 1)
        for gs in groups:
            if not last:
                scores(kb, 1 - slot, gs)
            if not first:
                weighted_values(vb_t, 1 - slot, gs)
            softmax(slot, gs)

    kb0 = load_keys(0)
    for gs in groups:
        scores(kb0, 0, gs)
    stage(0, 0, first=True, last=n_kblocks == 1)
    n_pairs = max(n_kblocks - 2, 0) // 2

    def pair(i, carry):
        stage(2 * i + 1, 1)
        stage(2 * i + 2, 0)
        return carry

    lax.fori_loop(0, n_pairs, pair, 0)
    for j in range(2 * n_pairs + 1, n_kblocks):
        stage(j, j % 2, last=j == n_kblocks - 1)
    vb_last = load_values(n_kblocks - 1)
    for gs in groups:
        weighted_values(vb_last, (n_kblocks - 1) % 2, gs)

    lp = lam_ref[...]
    lam = (jnp.exp(jnp.sum(lp[0:1] * lp[1:2], axis=-1, keepdims=True))
           - jnp.exp(jnp.sum(lp[2:3] * lp[3:4], axis=-1, keepdims=True)) + lam_init)
    dv = v_t_ref.shape[1]
    o_t = acc_ref[:dv, :] / acc_ref[dv:dv + 1, :]
    o_ref[0] = (o_t[:, :tq] - lam * o_t[:, tq:]).T


def _attn(dq_t, dk, dv_t, lam_p, *, seq, n_keys, lam_init):
    b_, lp, width = dk.shape
    tq, tk = ATTN_TQ, ATTN_TK
    heads = width // LANES
    kern = functools.partial(_attn_kernel, tk=tk, n_kblocks=n_keys // tk, lam_init=lam_init)
    return pl.pallas_call(
        kern,
        out_shape=jax.ShapeDtypeStruct((b_, seq, width), F32),
        grid=(b_, heads, seq // tq),
        in_specs=[
            pl.BlockSpec((1, LANES, tq), lambda b, h, i: (b, h, i)),
            pl.BlockSpec((1, lp, LANES), lambda b, h, i: (b, 0, h)),
            pl.BlockSpec((1, LANES, lp), lambda b, h, i: (b, h, 0)),
            pl.BlockSpec(lam_p.shape, lambda b, h, i: (0, 0)),
        ],
        out_specs=pl.BlockSpec((1, tq, LANES), lambda b, h, i: (b, i, h)),
        scratch_shapes=[
            pltpu.VMEM((1, 2 * tq), F32),
            pltpu.VMEM((LANES + BF16_ROWS, 2 * tq), F32),
            pltpu.VMEM((2, tk, 2 * tq), F32),
            pltpu.VMEM((2, tk, 2 * tq), BF16),
            pltpu.VMEM((2, 1, 2 * tq), F32),
        ],
        compiler_params=_params(("parallel", "parallel", "parallel")),
        name="diff_attn",
    )(dq_t, dk, dv_t, lam_p)


def _head_rms(x, gain, width):
    parts = []
    for h0 in range(0, x.shape[1], width):
        parts.append(_rms(x[:, h0:h0 + width]))
    return jnp.concatenate(parts, axis=1) * gain


def _merge_kernel(x_ref, mod_ref, of_ref, ob_ref, sg_ref, od_ref, sm_ref, gg_ref, gd_ref,
                  wa_ref, wb_ref, wo_ref, o_ref, *, dv_gla, dv_diff, diff_scale):
    d = x_ref.shape[2]
    ya = _head_rms(of_ref[0] + ob_ref[0], gg_ref[...], dv_gla) * sg_ref[0]
    ya = _dot(ya.astype(BF16), wa_ref[...])
    yb = _head_rms(od_ref[0], gd_ref[...], dv_diff) * diff_scale
    yb = _dot(yb.astype(BF16), wb_ref[...])
    sm = sm_ref[0]
    merged = (sm[:, :d] * ya + sm[:, d:] * yb).astype(BF16)
    gate = mod_ref[0, 0][5:6]
    o_ref[0] = x_ref[0] + gate * _dot(merged, wo_ref[...])


def _merge(x1, modt, o_f, o_b, sgg, o_d, smg, g_gla, g_diff, wa, wb, wo, *, n_lat_tiles, diff_scale):
    b_, _, d = x1.shape
    tm = TOKEN_TILE
    row = lambda width: pl.BlockSpec((1, tm, width), lambda b, j: (b, j, 0))
    kern = functools.partial(_merge_kernel, dv_gla=o_f.shape[2] // GLA_HEADS,
                             dv_diff=o_d.shape[2] // DIFF_HEADS, diff_scale=diff_scale)
    return pl.pallas_call(
        kern,
        out_shape=jax.ShapeDtypeStruct((b_, n_lat_tiles * tm, d), F32),
        grid=(b_, n_lat_tiles),
        in_specs=[
            row(d),
            pl.BlockSpec((1, 1, MOD_ROWS, d), lambda b, j: (b, 0, 0, 0)),
            row(o_f.shape[2]), row(o_b.shape[2]), row(sgg.shape[2]), row(o_d.shape[2]), row(smg.shape[2]),
            _resident(g_gla.shape), _resident(g_diff.shape),
            _resident(wa.shape), _resident(wb.shape), _resident(wo.shape),
        ],
        out_specs=row(d),
        compiler_params=_params(("parallel", "parallel")),
        name="merge",
    )(x1, modt, o_f, o_b, sgg, o_d, smg, g_gla, g_diff, wa, wb, wo)


def _rope_tables(seq, lp):
    half = DIFF_DH // 2
    quarter = half // 2
    inv = ROPE_BASE ** (-np.arange(quarter, dtype=np.float32) / quarter)
    t = jnp.arange(seq, dtype=jnp.int32)
    ang_row = (t // GRID_W).astype(F32)[:, None] * inv[None, :]
    ang_col = (t % GRID_W).astype(F32)[:, None] * inv[None, :]
    ang = jnp.concatenate([ang_row, ang_row, ang_col, ang_col], axis=1)
    sign = np.tile(np.concatenate([-np.ones(quarter), np.ones(quarter)]), 2).astype(np.float32)
    cos = jnp.tile(jnp.cos(ang), (1, LANES // DIFF_DH))
    sin = jnp.tile(jnp.sin(ang) * sign[None, :], (1, LANES // DIFF_DH))
    pad = lp - seq
    cos = jnp.concatenate([cos, jnp.ones((pad, LANES), F32)], axis=0)
    sin = jnp.concatenate([sin, jnp.zeros((pad, LANES), F32)], axis=0)
    tab_t = jnp.concatenate([jnp.cos(ang_row), jnp.sin(ang_row), jnp.cos(ang_col), jnp.sin(ang_col)], axis=1).T
    ident = np.repeat(np.array([1.0, 0.0, 1.0, 0.0], np.float32), quarter)[:, None]
    tab_t = jnp.concatenate([tab_t, jnp.broadcast_to(ident, (DIFF_DH, pad))], axis=1)
    return cos, sin, tab_t


def kernel(x, c, ctx, c_ctx, w_ada, b_ada, ffn1_norm, ffn1_w1, ffn1_w3, ffn1_w2, mix_norm, w_in, gla_gate_w2, gla_gate_b, gla_out_norm, diff_lambda, diff_out_norm, w_branch_gla, w_branch_diff, w_out, ffn2_norm, ffn2_w1, ffn2_w3, ffn2_w2, final_norm):
    b_, seq, d = x.shape
    n_ctx = ctx.shape[1]
    assert w_ada.shape[0] == 1, "single-layer stack"
    tm = TOKEN_TILE
    assert seq % tm == 0 and seq % ATTN_TQ == 0 and n_ctx <= tm
    assert seq % GLA_CHUNK == 0 and n_ctx % GLA_CHUNK == 0 and (seq + n_ctx) % ATTN_TK == 0
    n_lat_tiles = seq // tm
    n_tiles = n_lat_tiles + 1
    lp = n_tiles * tm
    lam_init = 0.8 - 0.6 * math.exp(0.0)

    gla_qk = GLA_HEADS * (d // 8)
    gla_v = GLA_HEADS * (d // 4)
    dif = DIFF_HEADS * 2 * DIFF_DH
    splits = np.cumsum([gla_qk, gla_qk, gla_v, gla_v, 2 * GLA_RANK, dif, dif, dif, 2 * d])
    assert w_in.shape[2] == splits[-1]
    wi = w_in[0].astype(BF16)
    w_gq, w_gk, w_gv, w_gg, w_lr, w_dq, w_dk, w_dv, w_mg = jnp.split(wi, splits[:-1].tolist(), axis=1)
    w_lr = jnp.pad(w_lr, ((0, 0), (0, LANES - 2 * GLA_RANK)))
    w2cat = jnp.zeros((LANES, 2 * gla_qk), F32)
    w2cat = w2cat.at[:GLA_RANK, :gla_qk].set(gla_gate_w2[0, 0])
    w2cat = w2cat.at[GLA_RANK:2 * GLA_RANK, gla_qk:].set(gla_gate_w2[0, 1]).astype(BF16)
    b2cat = gla_gate_b[0].reshape(1, 2 * gla_qk)

    rows = -(-(b_ + 1) // 8) * 8
    c_rows = jnp.concatenate([c, c_ctx[None, :], jnp.zeros((rows - b_ - 1, d), F32)], axis=0)
    mods = _mod_table(c_rows, w_ada[0], b_ada).reshape(rows, N_MOD, d)
    modt = jnp.stack([mods[:b_], jnp.broadcast_to(mods[b_], (b_, N_MOD, d))], axis=1)
    modt = jnp.pad(modt, ((0, 0), (0, 0), (0, MOD_ROWS - N_MOD), (0, 0)))

    xs = jnp.concatenate([x, ctx, jnp.zeros((b_, lp - seq - n_ctx, d), F32)], axis=1)
    tiles = dict(n_lat_tiles=n_lat_tiles, n_tiles=n_tiles)

    x1 = _ffn(xs, modt, ffn1_norm, ffn1_w1[0].astype(BF16), ffn1_w3[0].astype(BF16),
              ffn1_w2[0].astype(BF16), row0=0, **tiles)

    gq, gk, gv, sgg, logf = _proj_gla(x1, modt, mix_norm, w_gq, w_gk, w_gv, w_gg, w_lr, w2cat, b2cat, **tiles)
    cos_t, sin_t, tab_t = _rope_tables(seq, lp)
    dq, dk, dv, smg = _proj_dif(x1, modt, mix_norm, cos_t, sin_t, tab_t, w_dq.T, w_dk, w_dv.T, w_mg, **tiles)

    o_f, o_b = _gla(gq, gk, gv, logf, n_lat_chunks=seq // GLA_CHUNK, n_ctx_chunks=n_ctx // GLA_CHUNK)

    lam_p = jnp.pad(diff_lambda[0], ((0, 4), (0, LANES - DIFF_DH)))
    o_d = _attn(dq, dk, dv, lam_p, seq=seq, n_keys=seq + n_ctx, lam_init=lam_init)

    x2 = _merge(x1, modt, o_f, o_b, sgg, o_d, smg, gla_out_norm, diff_out_norm,
                w_branch_gla[0].astype(BF16), w_branch_diff[0].astype(BF16), w_out[0].astype(BF16),
                n_lat_tiles=n_lat_tiles, diff_scale=1.0 - lam_init)

    return _ffn(x2, modt, ffn2_norm, ffn2_w1[0].astype(BF16), ffn2_w3[0].astype(BF16),
                ffn2_w2[0].astype(BF16), row0=6, n_lat_tiles=n_lat_tiles, n_tiles=n_lat_tiles,
                final_gain=final_norm.reshape(1, d))
```

```python
import functools
import math

import jax
import jax.numpy as jnp
import numpy as np
from jax import lax
from jax.experimental import pallas as pl
from jax.experimental.pallas import tpu as pltpu

F32 = jnp.float32
BF16 = jnp.bfloat16

EPS = 1e-6
GRID_W = 64
N_MOD = 9
MOD_ROWS = 16
GLA_HEADS = 4
GLA_RANK = 16
GLA_TAU = 16.0
GLA_CHUNK = 64
DIFF_HEADS = 8
DIFF_DH = 64
ROPE_BASE = 10000.0
LANES = 128
BF16_ROWS = 16
LOG2_E = 1.4426950408889634

TOKEN_TILE = 512
ATTN_TQ = 1024
ATTN_TK = 768
ATTN_QGROUP = 256
VMEM_LIMIT = 52 * 1024 * 1024


def _resident(shape):
    zeros = (0,) * len(shape)
    return pl.BlockSpec(shape, lambda *_: zeros, pipeline_mode=pl.Buffered(1))


def _params(semantics):
    return pltpu.CompilerParams(dimension_semantics=semantics, vmem_limit_bytes=VMEM_LIMIT)


def _dot(a, b):
    return jnp.dot(a, b, preferred_element_type=F32)


def _dot_nt(a, b):
    return lax.dot_general(a, b, (((1,), (1,)), ((), ())), preferred_element_type=F32)


def _dot_tn(a, b):
    return lax.dot_general(a, b, (((0,), (0,)), ((), ())), preferred_element_type=F32)


def _sigmoid(x):
    return 1.0 / (1.0 + jnp.exp(-x))


def _rms(x):
    return x * lax.rsqrt(jnp.mean(x * x, axis=-1, keepdims=True) + EPS)


def _modnorm(x, gain, shift, scale1):
    return (_rms(x) * gain) * scale1 + shift


def _mod_kernel(c_ref, w_ref, b_ref, o_ref):
    k = pl.program_id(0)
    c = c_ref[...]
    s = c * _sigmoid(c)
    acc = jnp.zeros(o_ref.shape, F32)
    w = w_ref[...]
    s_parts = _split3(s)
    w_parts = _split3(w)
    for i in range(3):
        for j in range(3 - i):
            acc = acc + _dot(s_parts[i], w_parts[j])
    val = acc + b_ref[...]
    add = jnp.where(k % 3 == 1, 1.0, 0.0).astype(F32)
    mul = jnp.where((k == 2) | (k == 8), 0.5, 1.0).astype(F32)
    o_ref[...] = (val + add) * mul


def _split3(x):
    hi = x.astype(BF16)
    r1 = x - hi.astype(F32)
    mid = r1.astype(BF16)
    lo = (r1 - mid.astype(F32)).astype(BF16)
    return hi, mid, lo


def _mod_table(c_rows, w_ada, b_ada):
    rows, d = c_rows.shape
    return pl.pallas_call(
        _mod_kernel,
        out_shape=jax.ShapeDtypeStruct((rows, N_MOD * d), F32),
        grid=(N_MOD,),
        in_specs=[
            pl.BlockSpec((rows, d), lambda k: (0, 0)),
            pl.BlockSpec((d, d), lambda k: (0, k)),
            pl.BlockSpec((1, d), lambda k: (0, k)),
        ],
        out_specs=pl.BlockSpec((rows, d), lambda k: (0, k)),
        compiler_params=_params(("arbitrary",)),
        name="mod_table",
    )(c_rows, w_ada, b_ada)


def _ffn_kernel(x_ref, mod_ref, g_ref, w1_ref, w3_ref, w2_ref, *rest, row0, f_chunk, final):
    if final:
        gf_ref, o_ref = rest
    else:
        (o_ref,) = rest
    x = x_ref[0]
    mod = mod_ref[0, 0]
    shift, scale1, gate = mod[row0:row0 + 1], mod[row0 + 1:row0 + 2], mod[row0 + 2:row0 + 3]
    h = _modnorm(x, g_ref[...], shift, scale1).astype(BF16)
    acc = jnp.zeros(x.shape, F32)
    f_total = w1_ref.shape[1]
    for f0 in range(0, f_total, f_chunk):
        a = _dot(h, w1_ref[:, f0:f0 + f_chunk])
        b = _dot(h, w3_ref[:, f0:f0 + f_chunk])
        u = (a * _sigmoid(a) * b).astype(BF16)
        acc = acc + _dot(u, w2_ref[f0:f0 + f_chunk, :])
    y = x + gate * acc
    if final:
        y = _rms(y) * gf_ref[...]
    o_ref[0] = y


def _ffn(xs, modt, gain, w1, w3, w2, *, row0, n_lat_tiles, n_tiles, final_gain=None):
    b_, _, d = xs.shape
    f = w1.shape[1]
    tm = TOKEN_TILE
    final = final_gain is not None
    kern = functools.partial(_ffn_kernel, row0=row0, f_chunk=f // 2, final=final)
    in_specs = [
        pl.BlockSpec((1, tm, d), lambda b, j: (b, j, 0)),
        pl.BlockSpec((1, 1, MOD_ROWS, d), lambda b, j: (b, (j >= n_lat_tiles).astype(jnp.int32), 0, 0)),
        _resident((1, d)),
        _resident((d, f)),
        _resident((d, f)),
        _resident((f, d)),
    ]
    args = [xs, modt, gain, w1, w3, w2]
    if final:
        in_specs.append(_resident((1, d)))
        args.append(final_gain)
    return pl.pallas_call(
        kern,
        out_shape=jax.ShapeDtypeStruct((b_, n_tiles * tm, d), F32),
        grid=(b_, n_tiles),
        in_specs=in_specs,
        out_specs=pl.BlockSpec((1, tm, d), lambda b, j: (b, j, 0)),
        compiler_params=_params(("parallel", "parallel")),
        name="ffn_final" if final else "ffn",
    )(*args)


def _proj_gla_kernel(x_ref, mod_ref, g_ref, wq_ref, wk_ref, wv_ref, wg_ref, wlr_ref, w2_ref, b2_ref,
                     q_ref, k_ref, v_ref, sg_ref, lf_ref, *, q_scale):
    x = x_ref[0]
    mod = mod_ref[0, 0]
    h = _modnorm(x, g_ref[...], mod[3:4], mod[4:5]).astype(BF16)
    q_ref[0] = _dot(h, wq_ref[...]) * q_scale
    k_ref[0] = _dot(h, wk_ref[...])
    v_ref[0] = _dot(h, wv_ref[...]).astype(BF16)
    gg = _dot(h, wg_ref[...])
    sg_ref[0] = gg * _sigmoid(gg)
    lr = _dot(h, wlr_ref[...]).astype(BF16)
    z = _dot(lr, w2_ref[...]) + b2_ref[...]
    lf_ref[0] = (jnp.minimum(z, 0.0) - jnp.log(1.0 + jnp.exp(-jnp.abs(z)))) * (1.0 / GLA_TAU)


def _proj_gla(xs, modt, gain, wq, wk, wv, wg, wlr, w2cat, b2cat, *, n_lat_tiles, n_tiles):
    b_, lp, d = xs.shape
    tm = TOKEN_TILE
    qk = wq.shape[1]
    vv = wv.shape[1]
    row = lambda width: pl.BlockSpec((1, tm, width), lambda b, j: (b, j, 0))
    kern = functools.partial(_proj_gla_kernel, q_scale=(qk // GLA_HEADS) ** -0.5)
    return pl.pallas_call(
        kern,
        out_shape=(
            jax.ShapeDtypeStruct((b_, lp, qk), F32),
            jax.ShapeDtypeStruct((b_, lp, qk), F32),
            jax.ShapeDtypeStruct((b_, lp, vv), BF16),
            jax.ShapeDtypeStruct((b_, lp, vv), F32),
            jax.ShapeDtypeStruct((b_, lp, 2 * qk), F32),
        ),
        grid=(b_, n_tiles),
        in_specs=[
            row(d),
            pl.BlockSpec((1, 1, MOD_ROWS, d), lambda b, j: (b, (j >= n_lat_tiles).astype(jnp.int32), 0, 0)),
            _resident((1, d)),
            _resident(wq.shape), _resident(wk.shape), _resident(wv.shape), _resident(wg.shape),
            _resident(wlr.shape), _resident(w2cat.shape), _resident(b2cat.shape),
        ],
        out_specs=(row(qk), row(qk), row(vv), row(vv), row(2 * qk)),
        compiler_params=_params(("parallel", "parallel")),
        name="proj_gla",
    )(xs, modt, gain, wq, wk, wv, wg, wlr, w2cat, b2cat)


def _rope(u, cos, sin_signed, first_half):
    out = []
    for g in range(u.shape[1] // LANES):
        ug = u[:, g * LANES:(g + 1) * LANES]
        partner = jnp.where(first_half, pltpu.roll(ug, LANES - 16, 1), pltpu.roll(ug, 16, 1))
        out.append(ug * cos + partner * sin_signed)
    return jnp.concatenate(out, axis=1)


def _proj_dif_kernel(x_ref, mod_ref, g_ref, cos_ref, sin_ref, tab_t_ref, wq_t_ref, wk_ref, wv_t_ref, wm_ref,
                     q_t_ref, k_ref, v_t_ref, sm_ref, *, q_scale):
    x = x_ref[0]
    mod = mod_ref[0, 0]
    h = _modnorm(x, g_ref[...], mod[3:4], mod[4:5]).astype(BF16)
    q_t = _dot_nt(wq_t_ref[...], h)
    quarter = DIFF_DH // 4
    tab = tab_t_ref[...]
    for base in range(0, q_t.shape[0], 2 * quarter):
        t0 = base % DIFF_DH
        cos = tab[t0:t0 + quarter]
        sin = tab[t0 + quarter:t0 + 2 * quarter]
        u1 = q_t[base:base + quarter]
        u2 = q_t[base + quarter:base + 2 * quarter]
        q_t_ref[0, base:base + quarter, :] = ((u1 * cos - u2 * sin) * q_scale).astype(BF16)
        q_t_ref[0, base + quarter:base + 2 * quarter, :] = ((u2 * cos + u1 * sin) * q_scale).astype(BF16)
    cos = cos_ref[...]
    sin_signed = sin_ref[...]
    lane = lax.broadcasted_iota(jnp.int32, cos.shape, 1)
    first_half = (lane % 32) < 16
    k_ref[0] = _rope(_dot(h, wk_ref[...]), cos, sin_signed, first_half).astype(BF16)
    v_t_ref[0] = _dot_nt(wv_t_ref[...], h).astype(BF16)
    sm_ref[0] = _sigmoid(_dot(h, wm_ref[...]))


def _proj_dif(xs, modt, gain, cos_t, sin_t, tab_t, wq_t, wk, wv_t, wm, *, n_lat_tiles, n_tiles):
    b_, lp, d = xs.shape
    tm = TOKEN_TILE
    row = lambda width: pl.BlockSpec((1, tm, width), lambda b, j: (b, j, 0))
    col = lambda height: pl.BlockSpec((1, height, tm), lambda b, j: (b, 0, j))
    tab = pl.BlockSpec((tm, LANES), lambda b, j: (j, 0))
    kern = functools.partial(_proj_dif_kernel, q_scale=DIFF_DH ** -0.5 * LOG2_E)
    return pl.pallas_call(
        kern,
        out_shape=(
            jax.ShapeDtypeStruct((b_, wq_t.shape[0], lp), BF16),
            jax.ShapeDtypeStruct((b_, lp, wk.shape[1]), BF16),
            jax.ShapeDtypeStruct((b_, wv_t.shape[0], lp), BF16),
            jax.ShapeDtypeStruct((b_, lp, wm.shape[1]), F32),
        ),
        grid=(b_, n_tiles),
        in_specs=[
            row(d),
            pl.BlockSpec((1, 1, MOD_ROWS, d), lambda b, j: (b, (j >= n_lat_tiles).astype(jnp.int32), 0, 0)),
            _resident((1, d)),
            tab, tab,
            pl.BlockSpec((DIFF_DH, tm), lambda b, j: (0, j)),
            _resident(wq_t.shape), _resident(wk.shape), _resident(wv_t.shape), _resident(wm.shape),
        ],
        out_specs=(col(wq_t.shape[0]), row(wk.shape[1]), col(wv_t.shape[0]), row(wm.shape[1])),
        compiler_params=_params(("parallel", "parallel")),
        name="proj_dif",
    )(xs, modt, gain, cos_t, sin_t, tab_t, wq_t, wk, wv_t, wm)


def _gla_direction(q, k, v, logf, tri, strict_keep, ref_row, end_row, s_ref, o_ref):
    parts = _split3(logf)
    cum = _dot(tri, parts[0]) + _dot(tri, parts[1]) + _dot(tri, parts[2])
    ref = cum[ref_row:ref_row + 1]
    total = cum[end_row:end_row + 1]
    q_state = (q * jnp.exp(cum)).astype(BF16)
    q_in = (q * jnp.exp(cum - ref)).astype(BF16)
    k_in = (k * jnp.exp(ref - cum)).astype(BF16)
    k_out = (k * jnp.exp(total - cum)).astype(BF16)
    decay = jnp.exp(total)
    dk = q.shape[1] // GLA_HEADS
    dv = v.shape[1] // GLA_HEADS
    for hd in range(GLA_HEADS):
        ks = slice(hd * dk, (hd + 1) * dk)
        vs = slice(hd * dv, (hd + 1) * dv)
        state_t = s_ref[hd]
        vh = v[:, vs]
        attn = jnp.where(strict_keep, _dot_nt(q_in[:, ks], k_in[:, ks]), 0.0).astype(BF16)
        o_ref[0, :, vs] = _dot_nt(q_state[:, ks], state_t.astype(BF16)) + _dot(attn, vh)
        s_ref[hd] = state_t * decay[:, ks] + _dot_tn(vh, k_out[:, ks])


def _gla_kernel(qf_ref, kf_ref, vf_ref, lf_ref, qb_ref, kb_ref, vb_ref, lb_ref,
                of_ref, ob_ref, sf_ref, sb_ref):
    @pl.when(pl.program_id(1) == 0)
    def _():
        sf_ref[...] = jnp.zeros(sf_ref.shape, F32)
        sb_ref[...] = jnp.zeros(sb_ref.shape, F32)

    c = qf_ref.shape[1]
    row = lax.broadcasted_iota(jnp.int32, (c, c), 0)
    col = lax.broadcasted_iota(jnp.int32, (c, c), 1)
    lower = col <= row
    upper = col >= row
    _gla_direction(qf_ref[0], kf_ref[0], vf_ref[0], lf_ref[0],
                   jnp.where(lower, 1.0, 0.0).astype(BF16), lower, c // 2 - 1, c - 1, sf_ref, of_ref)
    _gla_direction(qb_ref[0], kb_ref[0], vb_ref[0], lb_ref[0],
                   jnp.where(upper, 1.0, 0.0).astype(BF16), upper, c // 2, 0, sb_ref, ob_ref)


def _gla(gq, gk, gv, logf, *, n_lat_chunks, n_ctx_chunks):
    b_, lp, qk = gq.shape
    vv = gv.shape[2]
    c = GLA_CHUNK
    n = n_lat_chunks + n_ctx_chunks
    fwd = lambda b, s: (b, (s + n_lat_chunks) % n, 0)
    bwd = lambda b, s: (b, n - 1 - s, 0)
    lf_spec = pl.BlockSpec((1, c, qk), lambda b, s: (b, (s + n_lat_chunks) % n, 0))
    lb_spec = pl.BlockSpec((1, c, qk), lambda b, s: (b, n - 1 - s, 1))
    return pl.pallas_call(
        _gla_kernel,
        out_shape=(jax.ShapeDtypeStruct((b_, lp, vv), F32), jax.ShapeDtypeStruct((b_, lp, vv), F32)),
        grid=(b_, n),
        in_specs=[
            pl.BlockSpec((1, c, qk), fwd), pl.BlockSpec((1, c, qk), fwd), pl.BlockSpec((1, c, vv), fwd), lf_spec,
            pl.BlockSpec((1, c, qk), bwd), pl.BlockSpec((1, c, qk), bwd), pl.BlockSpec((1, c, vv), bwd), lb_spec,
        ],
        out_specs=(pl.BlockSpec((1, c, vv), fwd), pl.BlockSpec((1, c, vv), bwd)),
        scratch_shapes=[
            pltpu.VMEM((GLA_HEADS, vv // GLA_HEADS, qk // GLA_HEADS), F32),
            pltpu.VMEM((GLA_HEADS, vv // GLA_HEADS, qk // GLA_HEADS), F32),
        ],
        compiler_params=_params(("parallel", "arbitrary")),
        name="gla_scan",
    )(gq, gk, gv, logf, gq, gk, gv, logf)


def _attn_kernel(q_t_ref, k_ref, v_t_ref, lam_ref, o_ref, m_ref, acc_ref, s_ref, p_ref, a_ref,
                 *, tk, n_kblocks, lam_init):
    tq = q_t_ref.shape[2]
    q_t = q_t_ref[0]
    row = lax.broadcasted_iota(jnp.int32, q_t.shape, 0)
    zero = jnp.zeros_like(q_t)
    qz_t = jnp.concatenate([jnp.where(row < DIFF_DH, q_t, zero), jnp.where(row >= DIFF_DH, q_t, zero)], axis=1)
    m_ref[...] = jnp.full(m_ref.shape, -jnp.inf, F32)
    acc_ref[...] = jnp.zeros(acc_ref.shape, F32)

    groups = list(range(2 * tq // ATTN_QGROUP))
    lanes = lambda g: slice(g * ATTN_QGROUP, (g + 1) * ATTN_QGROUP)

    def scores(kb, slot, g):
        s_ref[slot, g] = _dot(kb, qz_t[:, lanes(g)])

    def softmax(slot, g):
        s_t = s_ref[slot, g]
        m_prev = m_ref[g]
        m_new = jnp.maximum(m_prev, jnp.max(s_t, axis=0, keepdims=True))
        m_ref[g] = m_new
        a_ref[slot, g] = jnp.exp2(m_prev - m_new)
        p_ref[slot, g] = jnp.exp2(s_t - m_new).astype(BF16)

    ones_rows = jnp.ones((BF16_ROWS, tk), BF16)

    def weighted_values(vb_t, slot, g):
        acc_ref[g] = a_ref[slot, g] * acc_ref[g] + _dot(vb_t, p_ref[slot, g])

    def load_keys(j):
        return k_ref[0, pl.ds(pl.multiple_of(j * tk, tk), tk), :]

    def load_values(j):
        vb_t = v_t_ref[0, :, pl.ds(pl.multiple_of(j * tk, tk), tk)]
        return jnp.concatenate([vb_t, ones_rows], axis=0)

    def stage(j, slot, first=False, last=False):
        kb = None if last else load_keys(j + 1)
        vb_t = None if first else load_values(j - 1)
        for g in groups:
            if not last:
                scores(kb, 1 - slot, g)
            if not first:
                weighted_values(vb_t, 1 - slot, g)
            softmax(slot, g)

    kb0 = load_keys(0)
    for g in groups:
        scores(kb0, 0, g)
    stage(0, 0, first=True, last=n_kblocks == 1)
    n_pairs = max(n_kblocks - 2, 0) // 2

    def pair(i, carry):
        stage(2 * i + 1, 1)
        stage(2 * i + 2, 0)
        return carry

    lax.fori_loop(0, n_pairs, pair, 0)
    for j in range(2 * n_pairs + 1, n_kblocks):
        stage(j, j % 2, last=j == n_kblocks - 1)
    vb_last = load_values(n_kblocks - 1)
    for g in groups:
        weighted_values(vb_last, (n_kblocks - 1) % 2, g)

    lp = lam_ref[...]
    lam = (jnp.exp(jnp.sum(lp[0:1] * lp[1:2], axis=-1, keepdims=True))
           - jnp.exp(jnp.sum(lp[2:3] * lp[3:4], axis=-1, keepdims=True)) + lam_init)
    dv = v_t_ref.shape[1]
    acc = jnp.concatenate([acc_ref[g] for g in groups], axis=1)
    o_t = acc[:dv] / acc[dv:dv + 1]
    o_ref[0] = (o_t[:, :tq] - lam * o_t[:, tq:]).T


def _attn(dq_t, dk, dv_t, lam_p, *, seq, n_keys, lam_init):
    b_, lp, width = dk.shape
    tq, tk = ATTN_TQ, ATTN_TK
    heads = width // LANES
    n_groups = 2 * tq // ATTN_QGROUP
    kern = functools.partial(_attn_kernel, tk=tk, n_kblocks=n_keys // tk, lam_init=lam_init)
    return pl.pallas_call(
        kern,
        out_shape=jax.ShapeDtypeStruct((b_, seq, width), F32),
        grid=(b_, heads, seq // tq),
        in_specs=[
            pl.BlockSpec((1, LANES, tq), lambda b, h, i: (b, h, i)),
            pl.BlockSpec((1, lp, LANES), lambda b, h, i: (b, 0, h)),
            pl.BlockSpec((1, LANES, lp), lambda b, h, i: (b, h, 0)),
            pl.BlockSpec(lam_p.shape, lambda b, h, i: (0, 0)),
        ],
        out_specs=pl.BlockSpec((1, tq, LANES), lambda b, h, i: (b, i, h)),
        scratch_shapes=[
            pltpu.VMEM((n_groups, 1, ATTN_QGROUP), F32),
            pltpu.VMEM((n_groups, LANES + BF16_ROWS, ATTN_QGROUP), F32),
            pltpu.VMEM((2, n_groups, tk, ATTN_QGROUP), F32),
            pltpu.VMEM((2, n_groups, tk, ATTN_QGROUP), BF16),
            pltpu.VMEM((2, n_groups, 1, ATTN_QGROUP), F32),
        ],
        compiler_params=_params(("parallel", "parallel", "parallel")),
        name="diff_attn",
    )(dq_t, dk, dv_t, lam_p)


def _head_rms(x, gain, width):
    parts = []
    for h0 in range(0, x.shape[1], width):
        parts.append(_rms(x[:, h0:h0 + width]))
    return jnp.concatenate(parts, axis=1) * gain


def _merge_kernel(x_ref, mod_ref, of_ref, ob_ref, sg_ref, od_ref, sm_ref, gg_ref, gd_ref,
                  wa_ref, wb_ref, wo_ref, o_ref, *, dv_gla, dv_diff, diff_scale):
    d = x_ref.shape[2]
    ya = _head_rms(of_ref[0] + ob_ref[0], gg_ref[...], dv_gla) * sg_ref[0]
    ya = _dot(ya.astype(BF16), wa_ref[...])
    yb = _head_rms(od_ref[0], gd_ref[...], dv_diff) * diff_scale
    yb = _dot(yb.astype(BF16), wb_ref[...])
    sm = sm_ref[0]
    merged = (sm[:, :d] * ya + sm[:, d:] * yb).astype(BF16)
    gate = mod_ref[0, 0][5:6]
    o_ref[0] = x_ref[0] + gate * _dot(merged, wo_ref[...])


def _merge(x1, modt, o_f, o_b, sgg, o_d, smg, g_gla, g_diff, wa, wb, wo, *, n_lat_tiles, diff_scale):
    b_, _, d = x1.shape
    tm = TOKEN_TILE
    row = lambda width: pl.BlockSpec((1, tm, width), lambda b, j: (b, j, 0))
    kern = functools.partial(_merge_kernel, dv_gla=o_f.shape[2] // GLA_HEADS,
                             dv_diff=o_d.shape[2] // DIFF_HEADS, diff_scale=diff_scale)
    return pl.pallas_call(
        kern,
        out_shape=jax.ShapeDtypeStruct((b_, n_lat_tiles * tm, d), F32),
        grid=(b_, n_lat_tiles),
        in_specs=[
            row(d),
            pl.BlockSpec((1, 1, MOD_ROWS, d), lambda b, j: (b, 0, 0, 0)),
            row(o_f.shape[2]), row(o_b.shape[2]), row(sgg.shape[2]), row(o_d.shape[2]), row(smg.shape[2]),
            _resident(g_gla.shape), _resident(g_diff.shape),
            _resident(wa.shape), _resident(wb.shape), _resident(wo.shape),
        ],
        out_specs=row(d),
        compiler_params=_params(("parallel", "parallel")),
        name="merge",
    )(x1, modt, o_f, o_b, sgg, o_d, smg, g_gla, g_diff, wa, wb, wo)


def _rope_tables(seq, lp):
    half = DIFF_DH // 2
    quarter = half // 2
    inv = ROPE_BASE ** (-np.arange(quarter, dtype=np.float32) / quarter)
    t = jnp.arange(seq, dtype=jnp.int32)
    ang_row = (t // GRID_W).astype(F32)[:, None] * inv[None, :]
    ang_col = (t % GRID_W).astype(F32)[:, None] * inv[None, :]
    ang = jnp.concatenate([ang_row, ang_row, ang_col, ang_col], axis=1)
    sign = np.tile(np.concatenate([-np.ones(quarter), np.ones(quarter)]), 2).astype(np.float32)
    cos = jnp.tile(jnp.cos(ang), (1, LANES // DIFF_DH))
    sin = jnp.tile(jnp.sin(ang) * sign[None, :], (1, LANES // DIFF_DH))
    pad = lp - seq
    cos = jnp.concatenate([cos, jnp.ones((pad, LANES), F32)], axis=0)
    sin = jnp.concatenate([sin, jnp.zeros((pad, LANES), F32)], axis=0)
    tab_t = jnp.concatenate([jnp.cos(ang_row), jnp.sin(ang_row), jnp.cos(ang_col), jnp.sin(ang_col)], axis=1).T
    ident = np.repeat(np.array([1.0, 0.0, 1.0, 0.0], np.float32), quarter)[:, None]
    tab_t = jnp.concatenate([tab_t, jnp.broadcast_to(ident, (DIFF_DH, pad))], axis=1)
    return cos, sin, tab_t


def kernel(x, c, ctx, c_ctx, w_ada, b_ada, ffn1_norm, ffn1_w1, ffn1_w3, ffn1_w2, mix_norm, w_in, gla_gate_w2, gla_gate_b, gla_out_norm, diff_lambda, diff_out_norm, w_branch_gla, w_branch_diff, w_out, ffn2_norm, ffn2_w1, ffn2_w3, ffn2_w2, final_norm):
    b_, seq, d = x.shape
    n_ctx = ctx.shape[1]
    assert w_ada.shape[0] == 1, "single-layer stack"
    tm = TOKEN_TILE
    assert seq % tm == 0 and seq % ATTN_TQ == 0 and n_ctx <= tm
    assert seq % GLA_CHUNK == 0 and n_ctx % GLA_CHUNK == 0 and (seq + n_ctx) % ATTN_TK == 0
    n_lat_tiles = seq // tm
    n_tiles = n_lat_tiles + 1
    lp = n_tiles * tm
    lam_init = 0.8 - 0.6 * math.exp(0.0)

    gla_qk = GLA_HEADS * (d // 8)
    gla_v = GLA_HEADS * (d // 4)
    dif = DIFF_HEADS * 2 * DIFF_DH
    splits = np.cumsum([gla_qk, gla_qk, gla_v, gla_v, 2 * GLA_RANK, dif, dif, dif, 2 * d])
    assert w_in.shape[2] == splits[-1]
    wi = w_in[0].astype(BF16)
    w_gq, w_gk, w_gv, w_gg, w_lr, w_dq, w_dk, w_dv, w_mg = jnp.split(wi, splits[:-1].tolist(), axis=1)
    w_lr = jnp.pad(w_lr, ((0, 0), (0, LANES - 2 * GLA_RANK)))
    w2cat = jnp.zeros((LANES, 2 * gla_qk), F32)
    w2cat = w2cat.at[:GLA_RANK, :gla_qk].set(gla_gate_w2[0, 0])
    w2cat = w2cat.at[GLA_RANK:2 * GLA_RANK, gla_qk:].set(gla_gate_w2[0, 1]).astype(BF16)
    b2cat = gla_gate_b[0].reshape(1, 2 * gla_qk)

    rows = -(-(b_ + 1) // 8) * 8
    c_rows = jnp.concatenate([c, c_ctx[None, :], jnp.zeros((rows - b_ - 1, d), F32)], axis=0)
    mods = _mod_table(c_rows, w_ada[0], b_ada).reshape(rows, N_MOD, d)
    modt = jnp.stack([mods[:b_], jnp.broadcast_to(mods[b_], (b_, N_MOD, d))], axis=1)
    modt = jnp.pad(modt, ((0, 0), (0, 0), (0, MOD_ROWS - N_MOD), (0, 0)))

    xs = jnp.concatenate([x, ctx, jnp.zeros((b_, lp - seq - n_ctx, d), F32)], axis=1)
    tiles = dict(n_lat_tiles=n_lat_tiles, n_tiles=n_tiles)

    x1 = _ffn(xs, modt, ffn1_norm, ffn1_w1[0].astype(BF16), ffn1_w3[0].astype(BF16),
              ffn1_w2[0].astype(BF16), row0=0, **tiles)

    gq, gk, gv, sgg, logf = _proj_gla(x1, modt, mix_norm, w_gq, w_gk, w_gv, w_gg, w_lr, w2cat, b2cat, **tiles)
    cos_t, sin_t, tab_t = _rope_tables(seq, lp)
    dq, dk, dv, smg = _proj_dif(x1, modt, mix_norm, cos_t, sin_t, tab_t, w_dq.T, w_dk, w_dv.T, w_mg, **tiles)

    o_f, o_b = _gla(gq, gk, gv, logf, n_lat_chunks=seq // GLA_CHUNK, n_ctx_chunks=n_ctx // GLA_CHUNK)

    lam_p = jnp.pad(diff_lambda[0], ((0, 4), (0, LANES - DIFF_DH)))
    o_d = _attn(dq, dk, dv, lam_p, seq=seq, n_keys=seq + n_ctx, lam_init=lam_init)

    x2 = _merge(x1, modt, o_f, o_b, sgg, o_d, smg, gla_out_norm, diff_out_norm,
                w_branch_gla[0].astype(BF16), w_branch_diff[0].astype(BF16), w_out[0].astype(BF16),
                n_lat_tiles=n_lat_tiles, diff_scale=1.0 - lam_init)

    return _ffn(x2, modt, ffn2_norm, ffn2_w1[0].astype(BF16), ffn2_w3[0].astype(BF16),
                ffn2_w2[0].astype(BF16), row0=6, n_lat_tiles=n_lat_tiles, n_tiles=n_lat_tiles,
                final_gain=final_norm.reshape(1, d))
```

```python
import functools
import math

import jax
import jax.numpy as jnp
import numpy as np
from jax import lax
from jax.experimental import pallas as pl
from jax.experimental.pallas import tpu as pltpu

F32 = jnp.float32
BF16 = jnp.bfloat16

EPS = 1e-6
GRID_W = 64
N_MOD = 9
MOD_ROWS = 16
GLA_HEADS = 4
GLA_RANK = 16
GLA_TAU = 16.0
GLA_CHUNK = 64
GLA_BLOCK = 256
DIFF_HEADS = 8
DIFF_DH = 64
ROPE_BASE = 10000.0
LANES = 128
BF16_ROWS = 16
LOG2_E = 1.4426950408889634

TOKEN_TILE = 512
ATTN_TQ = 1024
ATTN_TK = 768
ATTN_QGROUP = 256
VMEM_LIMIT = 52 * 1024 * 1024


def _resident(shape):
    zeros = (0,) * len(shape)
    return pl.BlockSpec(shape, lambda *_: zeros, pipeline_mode=pl.Buffered(1))


def _params(semantics):
    return pltpu.CompilerParams(dimension_semantics=semantics, vmem_limit_bytes=VMEM_LIMIT)


def _dot(a, b):
    return jnp.dot(a, b, preferred_element_type=F32)


def _dot_nt(a, b):
    return lax.dot_general(a, b, (((1,), (1,)), ((), ())), preferred_element_type=F32)


def _dot_tn(a, b):
    return lax.dot_general(a, b, (((0,), (0,)), ((), ())), preferred_element_type=F32)


def _sigmoid(x):
    return 1.0 / (1.0 + jnp.exp(-x))


def _rms(x):
    return x * lax.rsqrt(jnp.mean(x * x, axis=-1, keepdims=True) + EPS)


def _modnorm(x, gain, shift, scale1):
    return (_rms(x) * gain) * scale1 + shift


def _mod_kernel(c_ref, w_ref, b_ref, o_ref):
    k = pl.program_id(0)
    c = c_ref[...]
    s = c * _sigmoid(c)
    acc = jnp.zeros(o_ref.shape, F32)
    w = w_ref[...]
    s_parts = _split3(s)
    w_parts = _split3(w)
    for i in range(3):
        for j in range(3 - i):
            acc = acc + _dot(s_parts[i], w_parts[j])
    val = acc + b_ref[...]
    add = jnp.where(k % 3 == 1, 1.0, 0.0).astype(F32)
    mul = jnp.where((k == 2) | (k == 8), 0.5, 1.0).astype(F32)
    o_ref[...] = (val + add) * mul


def _split3(x):
    hi = x.astype(BF16)
    r1 = x - hi.astype(F32)
    mid = r1.astype(BF16)
    lo = (r1 - mid.astype(F32)).astype(BF16)
    return hi, mid, lo


def _mod_table(c_rows, w_ada, b_ada):
    rows, d = c_rows.shape
    return pl.pallas_call(
        _mod_kernel,
        out_shape=jax.ShapeDtypeStruct((rows, N_MOD * d), F32),
        grid=(N_MOD,),
        in_specs=[
            pl.BlockSpec((rows, d), lambda k: (0, 0)),
            pl.BlockSpec((d, d), lambda k: (0, k)),
            pl.BlockSpec((1, d), lambda k: (0, k)),
        ],
        out_specs=pl.BlockSpec((rows, d), lambda k: (0, k)),
        compiler_params=_params(("arbitrary",)),
        name="mod_table",
    )(c_rows, w_ada, b_ada)


def _ffn_kernel(x_ref, mod_ref, g_ref, w1_ref, w3_ref, w2_ref, *rest, row0, f_chunk, final):
    if final:
        gf_ref, o_ref = rest
    else:
        (o_ref,) = rest
    x = x_ref[0]
    mod = mod_ref[0, 0]
    shift, scale1, gate = mod[row0:row0 + 1], mod[row0 + 1:row0 + 2], mod[row0 + 2:row0 + 3]
    h = _modnorm(x, g_ref[...], shift, scale1).astype(BF16)
    acc = jnp.zeros(x.shape, F32)
    f_total = w1_ref.shape[1]
    for f0 in range(0, f_total, f_chunk):
        a = _dot(h, w1_ref[:, f0:f0 + f_chunk])
        b = _dot(h, w3_ref[:, f0:f0 + f_chunk])
        u = (a * _sigmoid(a) * b).astype(BF16)
        acc = acc + _dot(u, w2_ref[f0:f0 + f_chunk, :])
    y = x + gate * acc
    if final:
        y = _rms(y) * gf_ref[...]
    o_ref[0] = y


def _ffn(xs, modt, gain, w1, w3, w2, *, row0, n_lat_tiles, n_tiles, final_gain=None):
    b_, _, d = xs.shape
    f = w1.shape[1]
    tm = TOKEN_TILE
    final = final_gain is not None
    kern = functools.partial(_ffn_kernel, row0=row0, f_chunk=f // 2, final=final)
    in_specs = [
        pl.BlockSpec((1, tm, d), lambda b, j: (b, j, 0)),
        pl.BlockSpec((1, 1, MOD_ROWS, d), lambda b, j: (b, (j >= n_lat_tiles).astype(jnp.int32), 0, 0)),
        _resident((1, d)),
        _resident((d, f)),
        _resident((d, f)),
        _resident((f, d)),
    ]
    args = [xs, modt, gain, w1, w3, w2]
    if final:
        in_specs.append(_resident((1, d)))
        args.append(final_gain)
    return pl.pallas_call(
        kern,
        out_shape=jax.ShapeDtypeStruct((b_, n_tiles * tm, d), F32),
        grid=(b_, n_tiles),
        in_specs=in_specs,
        out_specs=pl.BlockSpec((1, tm, d), lambda b, j: (b, j, 0)),
        compiler_params=_params(("parallel", "parallel")),
        name="ffn_final" if final else "ffn",
    )(*args)


def _proj_gla_kernel(x_ref, mod_ref, g_ref, wq_ref, wk_ref, wv_ref, wg_ref, wlr_ref, w2_ref, b2_ref,
                     q_ref, k_ref, v_ref, sg_ref, lf_ref, *, q_scale):
    x = x_ref[0]
    mod = mod_ref[0, 0]
    h = _modnorm(x, g_ref[...], mod[3:4], mod[4:5]).astype(BF16)
    q_ref[0] = _dot(h, wq_ref[...]) * q_scale
    k_ref[0] = _dot(h, wk_ref[...])
    v_ref[0] = _dot(h, wv_ref[...]).astype(BF16)
    gg = _dot(h, wg_ref[...])
    sg_ref[0] = gg * _sigmoid(gg)
    lr = _dot(h, wlr_ref[...]).astype(BF16)
    z = _dot(lr, w2_ref[...]) + b2_ref[...]
    lf_ref[0] = (jnp.minimum(z, 0.0) - jnp.log(1.0 + jnp.exp(-jnp.abs(z)))) * (1.0 / GLA_TAU)


def _proj_gla(xs, modt, gain, wq, wk, wv, wg, wlr, w2cat, b2cat, *, n_lat_tiles, n_tiles):
    b_, lp, d = xs.shape
    tm = TOKEN_TILE
    qk = wq.shape[1]
    vv = wv.shape[1]
    row = lambda width: pl.BlockSpec((1, tm, width), lambda b, j: (b, j, 0))
    kern = functools.partial(_proj_gla_kernel, q_scale=(qk // GLA_HEADS) ** -0.5)
    return pl.pallas_call(
        kern,
        out_shape=(
            jax.ShapeDtypeStruct((b_, lp, qk), F32),
            jax.ShapeDtypeStruct((b_, lp, qk), F32),
            jax.ShapeDtypeStruct((b_, lp, vv), BF16),
            jax.ShapeDtypeStruct((b_, lp, vv), F32),
            jax.ShapeDtypeStruct((b_, lp, 2 * qk), F32),
        ),
        grid=(b_, n_tiles),
        in_specs=[
            row(d),
            pl.BlockSpec((1, 1, MOD_ROWS, d), lambda b, j: (b, (j >= n_lat_tiles).astype(jnp.int32), 0, 0)),
            _resident((1, d)),
            _resident(wq.shape), _resident(wk.shape), _resident(wv.shape), _resident(wg.shape),
            _resident(wlr.shape), _resident(w2cat.shape), _resident(b2cat.shape),
        ],
        out_specs=(row(qk), row(qk), row(vv), row(vv), row(2 * qk)),
        compiler_params=_params(("parallel", "parallel")),
        name="proj_gla",
    )(xs, modt, gain, wq, wk, wv, wg, wlr, w2cat, b2cat)


def _rope(u, cos, sin_signed, first_half):
    out = []
    for g in range(u.shape[1] // LANES):
        ug = u[:, g * LANES:(g + 1) * LANES]
        partner = jnp.where(first_half, pltpu.roll(ug, LANES - 16, 1), pltpu.roll(ug, 16, 1))
        out.append(ug * cos + partner * sin_signed)
    return jnp.concatenate(out, axis=1)


def _proj_dif_kernel(x_ref, mod_ref, g_ref, cos_ref, sin_ref, tab_t_ref, wq_t_ref, wk_ref, wv_t_ref, wm_ref,
                     q_t_ref, k_ref, v_t_ref, sm_ref, *, q_scale):
    x = x_ref[0]
    mod = mod_ref[0, 0]
    h = _modnorm(x, g_ref[...], mod[3:4], mod[4:5]).astype(BF16)
    q_t = _dot_nt(wq_t_ref[...], h)
    quarter = DIFF_DH // 4
    tab = tab_t_ref[...]
    for base in range(0, q_t.shape[0], 2 * quarter):
        t0 = base % DIFF_DH
        cos = tab[t0:t0 + quarter]
        sin = tab[t0 + quarter:t0 + 2 * quarter]
        u1 = q_t[base:base + quarter]
        u2 = q_t[base + quarter:base + 2 * quarter]
        q_t_ref[0, base:base + quarter, :] = ((u1 * cos - u2 * sin) * q_scale).astype(BF16)
        q_t_ref[0, base + quarter:base + 2 * quarter, :] = ((u2 * cos + u1 * sin) * q_scale).astype(BF16)
    cos = cos_ref[...]
    sin_signed = sin_ref[...]
    lane = lax.broadcasted_iota(jnp.int32, cos.shape, 1)
    first_half = (lane % 32) < 16
    k_ref[0] = _rope(_dot(h, wk_ref[...]), cos, sin_signed, first_half).astype(BF16)
    v_t_ref[0] = _dot_nt(wv_t_ref[...], h).astype(BF16)
    sm_ref[0] = _sigmoid(_dot(h, wm_ref[...]))


def _proj_dif(xs, modt, gain, cos_t, sin_t, tab_t, wq_t, wk, wv_t, wm, *, n_lat_tiles, n_tiles):
    b_, lp, d = xs.shape
    tm = TOKEN_TILE
    row = lambda width: pl.BlockSpec((1, tm, width), lambda b, j: (b, j, 0))
    col = lambda height: pl.BlockSpec((1, height, tm), lambda b, j: (b, 0, j))
    tab = pl.BlockSpec((tm, LANES), lambda b, j: (j, 0))
    kern = functools.partial(_proj_dif_kernel, q_scale=DIFF_DH ** -0.5 * LOG2_E)
    return pl.pallas_call(
        kern,
        out_shape=(
            jax.ShapeDtypeStruct((b_, wq_t.shape[0], lp), BF16),
            jax.ShapeDtypeStruct((b_, lp, wk.shape[1]), BF16),
            jax.ShapeDtypeStruct((b_, wv_t.shape[0], lp), BF16),
            jax.ShapeDtypeStruct((b_, lp, wm.shape[1]), F32),
        ),
        grid=(b_, n_tiles),
        in_specs=[
            row(d),
            pl.BlockSpec((1, 1, MOD_ROWS, d), lambda b, j: (b, (j >= n_lat_tiles).astype(jnp.int32), 0, 0)),
            _resident((1, d)),
            tab, tab,
            pl.BlockSpec((DIFF_DH, tm), lambda b, j: (0, j)),
            _resident(wq_t.shape), _resident(wk.shape), _resident(wv_t.shape), _resident(wm.shape),
        ],
        out_specs=(col(wq_t.shape[0]), row(wk.shape[1]), col(wv_t.shape[0]), row(wm.shape[1])),
        compiler_params=_params(("parallel", "parallel")),
        name="proj_dif",
    )(xs, modt, gain, cos_t, sin_t, tab_t, wq_t, wk, wv_t, wm)


def _gla_decays(q, k, logf, tri, ref_row, end_row):
    c = GLA_CHUNK
    parts = _split3(logf)
    cum = _dot(tri, parts[0]) + _dot(tri, parts[1]) + _dot(tri, parts[2])
    n_chunks = q.shape[0] // c
    totals = [cum[a * c + end_row:a * c + end_row + 1] for a in range(n_chunks)]
    per_chunk = lambda rows: jnp.concatenate([jnp.broadcast_to(r, (c, r.shape[1])) for r in rows], axis=0)
    ref = per_chunk([cum[a * c + ref_row:a * c + ref_row + 1] for a in range(n_chunks)])
    total = per_chunk(totals)
    q_state = (q * jnp.exp(cum)).astype(BF16)
    q_in = (q * jnp.exp(cum - ref)).astype(BF16)
    k_in = (k * jnp.exp(ref - cum)).astype(BF16)
    k_out = (k * jnp.exp(total - cum)).astype(BF16)
    return q_state, q_in, k_in, k_out, [jnp.exp(t) for t in totals]


def _gla_kernel(qf_ref, kf_ref, vf_ref, lf_ref, qb_ref, kb_ref, vb_ref, lb_ref,
                of_ref, ob_ref, sf_ref, sb_ref):
    @pl.when(pl.program_id(1) == 0)
    def _():
        sf_ref[...] = jnp.zeros(sf_ref.shape, F32)
        sb_ref[...] = jnp.zeros(sb_ref.shape, F32)

    c = GLA_CHUNK
    rows = qf_ref.shape[1]
    n_chunks = rows // c
    row = lax.broadcasted_iota(jnp.int32, (rows, rows), 0)
    col = lax.broadcasted_iota(jnp.int32, (rows, rows), 1)
    same_chunk = (row // c) == (col // c)
    lower = same_chunk & (col <= row)
    upper = same_chunk & (col >= row)
    dirs = [
        (_gla_decays(qf_ref[0], kf_ref[0], lf_ref[0], jnp.where(lower, 1.0, 0.0).astype(BF16),
                     c // 2 - 1, c - 1), vf_ref[0], lower, sf_ref, of_ref, list(range(n_chunks))),
        (_gla_decays(qb_ref[0], kb_ref[0], lb_ref[0], jnp.where(upper, 1.0, 0.0).astype(BF16),
                     c // 2, 0), vb_ref[0], upper, sb_ref, ob_ref, list(range(n_chunks))[::-1]),
    ]
    dk = qf_ref.shape[2] // GLA_HEADS
    dv = vf_ref.shape[2] // GLA_HEADS
    chains = []
    for (q_state, q_in, k_in, k_out, decays), v, keep, s_ref, o_ref, order in dirs:
        for hd in range(GLA_HEADS):
            ks = slice(hd * dk, (hd + 1) * dk)
            vs = slice(hd * dv, (hd + 1) * dv)
            chains.append(dict(q_state=q_state[:, ks], q_in=q_in[:, ks], k_in=k_in[:, ks], k_out=k_out[:, ks],
                               decays=[d[:, ks] for d in decays], v=v[:, vs], keep=keep, s_ref=s_ref,
                               o_ref=o_ref, hd=hd, vs=vs, order=order))
    chunk = lambda x, a: x[a * c:(a + 1) * c]
    for ch in chains:
        ch["attn"] = _dot_nt(ch["q_in"], ch["k_in"])
    for ch in chains:
        ch["update"] = [_dot_tn(chunk(ch["v"], a), chunk(ch["k_out"], a)) for a in range(n_chunks)]
    for ch in chains:
        state_t = ch["s_ref"][ch["hd"]]
        ch["before"] = {}
        for a in ch["order"]:
            ch["before"][a] = state_t
            state_t = state_t * ch["decays"][a] + ch["update"][a]
        ch["s_ref"][ch["hd"]] = state_t
    for ch in chains:
        ch["inter"] = jnp.concatenate(
            [_dot_nt(chunk(ch["q_state"], a), ch["before"][a].astype(BF16)) for a in range(n_chunks)], axis=0)
    for ch in chains:
        attn = jnp.where(ch["keep"], ch["attn"], 0.0).astype(BF16)
        ch["o_ref"][0, :, ch["vs"]] = ch["inter"] + _dot(attn, ch["v"])


def _gla(gq, gk, gv, logf, *, n_lat_blocks, n_ctx_blocks):
    b_, lp, qk = gq.shape
    vv = gv.shape[2]
    c = GLA_BLOCK
    n = n_lat_blocks + n_ctx_blocks
    fwd = lambda b, s: (b, (s + n_lat_blocks) % n, 0)
    bwd = lambda b, s: (b, n - 1 - s, 0)
    lf_spec = pl.BlockSpec((1, c, qk), lambda b, s: (b, (s + n_lat_blocks) % n, 0))
    lb_spec = pl.BlockSpec((1, c, qk), lambda b, s: (b, n - 1 - s, 1))
    return pl.pallas_call(
        _gla_kernel,
        out_shape=(jax.ShapeDtypeStruct((b_, lp, vv), F32), jax.ShapeDtypeStruct((b_, lp, vv), F32)),
        grid=(b_, n),
        in_specs=[
            pl.BlockSpec((1, c, qk), fwd), pl.BlockSpec((1, c, qk), fwd), pl.BlockSpec((1, c, vv), fwd), lf_spec,
            pl.BlockSpec((1, c, qk), bwd), pl.BlockSpec((1, c, qk), bwd), pl.BlockSpec((1, c, vv), bwd), lb_spec,
        ],
        out_specs=(pl.BlockSpec((1, c, vv), fwd), pl.BlockSpec((1, c, vv), bwd)),
        scratch_shapes=[
            pltpu.VMEM((GLA_HEADS, vv // GLA_HEADS, qk // GLA_HEADS), F32),
            pltpu.VMEM((GLA_HEADS, vv // GLA_HEADS, qk // GLA_HEADS), F32),
        ],
        compiler_params=_params(("parallel", "arbitrary")),
        name="gla_scan",
    )(gq, gk, gv, logf, gq, gk, gv, logf)


def _attn_kernel(q_t_ref, k_ref, v_t_ref, lam_ref, o_ref, m_ref, acc_ref, s_ref, p_ref, a_ref,
                 *, tk, n_kblocks, lam_init):
    tq = q_t_ref.shape[2]
    q_t = q_t_ref[0]
    row = lax.broadcasted_iota(jnp.int32, q_t.shape, 0)
    zero = jnp.zeros_like(q_t)
    qz_t = jnp.concatenate([jnp.where(row < DIFF_DH, q_t, zero), jnp.where(row >= DIFF_DH, q_t, zero)], axis=1)
    m_ref[...] = jnp.full(m_ref.shape, -jnp.inf, F32)
    acc_ref[...] = jnp.zeros(acc_ref.shape, F32)

    groups = list(range(2 * tq // ATTN_QGROUP))
    lanes = lambda g: slice(g * ATTN_QGROUP, (g + 1) * ATTN_QGROUP)

    def scores(kb, slot, g):
        s_ref[slot, g] = _dot(kb, qz_t[:, lanes(g)])

    def softmax(slot, g):
        s_t = s_ref[slot, g]
        m_prev = m_ref[g]
        m_new = jnp.maximum(m_prev, jnp.max(s_t, axis=0, keepdims=True))
        m_ref[g] = m_new
        a_ref[slot, g] = jnp.exp2(m_prev - m_new)
        p_ref[slot, g] = jnp.exp2(s_t - m_new).astype(BF16)

    ones_rows = jnp.ones((BF16_ROWS, tk), BF16)

    def weighted_values(vb_t, slot, g):
        acc_ref[g] = a_ref[slot, g] * acc_ref[g] + _dot(vb_t, p_ref[slot, g])

    def load_keys(j):
        return k_ref[0, pl.ds(pl.multiple_of(j * tk, tk), tk), :]

    def load_values(j):
        vb_t = v_t_ref[0, :, pl.ds(pl.multiple_of(j * tk, tk), tk)]
        return jnp.concatenate([vb_t, ones_rows], axis=0)

    def stage(j, slot, first=False, last=False):
        kb = None if last else load_keys(j + 1)
        vb_t = None if first else load_values(j - 1)
        for g in groups:
            if not last:
                scores(kb, 1 - slot, g)
            if not first:
                weighted_values(vb_t, 1 - slot, g)
            softmax(slot, g)

    kb0 = load_keys(0)
    for g in groups:
        scores(kb0, 0, g)
    stage(0, 0, first=True, last=n_kblocks == 1)
    n_pairs = max(n_kblocks - 2, 0) // 2

    def pair(i, carry):
        stage(2 * i + 1, 1)
        stage(2 * i + 2, 0)
        return carry

    lax.fori_loop(0, n_pairs, pair, 0)
    for j in range(2 * n_pairs + 1, n_kblocks):
        stage(j, j % 2, last=j == n_kblocks - 1)
    vb_last = load_values(n_kblocks - 1)
    for g in groups:
        weighted_values(vb_last, (n_kblocks - 1) % 2, g)

    lp = lam_ref[...]
    lam = (jnp.exp(jnp.sum(lp[0:1] * lp[1:2], axis=-1, keepdims=True))
           - jnp.exp(jnp.sum(lp[2:3] * lp[3:4], axis=-1, keepdims=True)) + lam_init)
    dv = v_t_ref.shape[1]
    acc = jnp.concatenate([acc_ref[g] for g in groups], axis=1)
    o_t = acc[:dv] / acc[dv:dv + 1]
    o_ref[0] = (o_t[:, :tq] - lam * o_t[:, tq:]).T


def _attn(dq_t, dk, dv_t, lam_p, *, seq, n_keys, lam_init):
    b_, lp, width = dk.shape
    tq, tk = ATTN_TQ, ATTN_TK
    heads = width // LANES
    n_groups = 2 * tq // ATTN_QGROUP
    kern = functools.partial(_attn_kernel, tk=tk, n_kblocks=n_keys // tk, lam_init=lam_init)
    return pl.pallas_call(
        kern,
        out_shape=jax.ShapeDtypeStruct((b_, seq, width), F32),
        grid=(b_, heads, seq // tq),
        in_specs=[
            pl.BlockSpec((1, LANES, tq), lambda b, h, i: (b, h, i)),
            pl.BlockSpec((1, lp, LANES), lambda b, h, i: (b, 0, h)),
            pl.BlockSpec((1, LANES, lp), lambda b, h, i: (b, h, 0)),
            pl.BlockSpec(lam_p.shape, lambda b, h, i: (0, 0)),
        ],
        out_specs=pl.BlockSpec((1, tq, LANES), lambda b, h, i: (b, i, h)),
        scratch_shapes=[
            pltpu.VMEM((n_groups, 1, ATTN_QGROUP), F32),
            pltpu.VMEM((n_groups, LANES + BF16_ROWS, ATTN_QGROUP), F32),
            pltpu.VMEM((2, n_groups, tk, ATTN_QGROUP), F32),
            pltpu.VMEM((2, n_groups, tk, ATTN_QGROUP), BF16),
            pltpu.VMEM((2, n_groups, 1, ATTN_QGROUP), F32),
        ],
        compiler_params=_params(("parallel", "parallel", "parallel")),
        name="diff_attn",
    )(dq_t, dk, dv_t, lam_p)


def _head_rms(x, gain, width):
    parts = []
    for h0 in range(0, x.shape[1], width):
        parts.append(_rms(x[:, h0:h0 + width]))
    return jnp.concatenate(parts, axis=1) * gain


def _merge_kernel(x_ref, mod_ref, of_ref, ob_ref, sg_ref, od_ref, sm_ref, gg_ref, gd_ref,
                  wa_ref, wb_ref, wo_ref, o_ref, *, dv_gla, dv_diff, diff_scale):
    d = x_ref.shape[2]
    ya = _head_rms(of_ref[0] + ob_ref[0], gg_ref[...], dv_gla) * sg_ref[0]
    ya = _dot(ya.astype(BF16), wa_ref[...])
    yb = _head_rms(od_ref[0], gd_ref[...], dv_diff) * diff_scale
    yb = _dot(yb.astype(BF16), wb_ref[...])
    sm = sm_ref[0]
    merged = (sm[:, :d] * ya + sm[:, d:] * yb).astype(BF16)
    gate = mod_ref[0, 0][5:6]
    o_ref[0] = x_ref[0] + gate * _dot(merged, wo_ref[...])


def _merge(x1, modt, o_f, o_b, sgg, o_d, smg, g_gla, g_diff, wa, wb, wo, *, n_lat_tiles, diff_scale):
    b_, _, d = x1.shape
    tm = TOKEN_TILE
    row = lambda width: pl.BlockSpec((1, tm, width), lambda b, j: (b, j, 0))
    kern = functools.partial(_merge_kernel, dv_gla=o_f.shape[2] // GLA_HEADS,
                             dv_diff=o_d.shape[2] // DIFF_HEADS, diff_scale=diff_scale)
    return pl.pallas_call(
        kern,
        out_shape=jax.ShapeDtypeStruct((b_, n_lat_tiles * tm, d), F32),
        grid=(b_, n_lat_tiles),
        in_specs=[
            row(d),
            pl.BlockSpec((1, 1, MOD_ROWS, d), lambda b, j: (b, 0, 0, 0)),
            row(o_f.shape[2]), row(o_b.shape[2]), row(sgg.shape[2]), row(o_d.shape[2]), row(smg.shape[2]),
            _resident(g_gla.shape), _resident(g_diff.shape),
            _resident(wa.shape), _resident(wb.shape), _resident(wo.shape),
        ],
        out_specs=row(d),
        compiler_params=_params(("parallel", "parallel")),
        name="merge",
    )(x1, modt, o_f, o_b, sgg, o_d, smg, g_gla, g_diff, wa, wb, wo)


def _rope_tables(seq, lp):
    half = DIFF_DH // 2
    quarter = half // 2
    inv = ROPE_BASE ** (-np.arange(quarter, dtype=np.float32) / quarter)
    t = jnp.arange(seq, dtype=jnp.int32)
    ang_row = (t // GRID_W).astype(F32)[:, None] * inv[None, :]
    ang_col = (t % GRID_W).astype(F32)[:, None] * inv[None, :]
    ang = jnp.concatenate([ang_row, ang_row, ang_col, ang_col], axis=1)
    sign = np.tile(np.concatenate([-np.ones(quarter), np.ones(quarter)]), 2).astype(np.float32)
    cos = jnp.tile(jnp.cos(ang), (1, LANES // DIFF_DH))
    sin = jnp.tile(jnp.sin(ang) * sign[None, :], (1, LANES // DIFF_DH))
    pad = lp - seq
    cos = jnp.concatenate([cos, jnp.ones((pad, LANES), F32)], axis=0)
    sin = jnp.concatenate([sin, jnp.zeros((pad, LANES), F32)], axis=0)
    tab_t = jnp.concatenate([jnp.cos(ang_row), jnp.sin(ang_row), jnp.cos(ang_col), jnp.sin(ang_col)], axis=1).T
    ident = np.repeat(np.array([1.0, 0.0, 1.0, 0.0], np.float32), quarter)[:, None]
    tab_t = jnp.concatenate([tab_t, jnp.broadcast_to(ident, (DIFF_DH, pad))], axis=1)
    return cos, sin, tab_t


def kernel(x, c, ctx, c_ctx, w_ada, b_ada, ffn1_norm, ffn1_w1, ffn1_w3, ffn1_w2, mix_norm, w_in, gla_gate_w2, gla_gate_b, gla_out_norm, diff_lambda, diff_out_norm, w_branch_gla, w_branch_diff, w_out, ffn2_norm, ffn2_w1, ffn2_w3, ffn2_w2, final_norm):
    b_, seq, d = x.shape
    n_ctx = ctx.shape[1]
    assert w_ada.shape[0] == 1, "single-layer stack"
    tm = TOKEN_TILE
    assert seq % tm == 0 and seq % ATTN_TQ == 0 and n_ctx <= tm
    assert seq % GLA_BLOCK == 0 and n_ctx % GLA_BLOCK == 0 and (seq + n_ctx) % ATTN_TK == 0
    n_lat_tiles = seq // tm
    n_tiles = n_lat_tiles + 1
    lp = n_tiles * tm
    lam_init = 0.8 - 0.6 * math.exp(0.0)

    gla_qk = GLA_HEADS * (d // 8)
    gla_v = GLA_HEADS * (d // 4)
    dif = DIFF_HEADS * 2 * DIFF_DH
    splits = np.cumsum([gla_qk, gla_qk, gla_v, gla_v, 2 * GLA_RANK, dif, dif, dif, 2 * d])
    assert w_in.shape[2] == splits[-1]
    wi = w_in[0].astype(BF16)
    w_gq, w_gk, w_gv, w_gg, w_lr, w_dq, w_dk, w_dv, w_mg = jnp.split(wi, splits[:-1].tolist(), axis=1)
    w_lr = jnp.pad(w_lr, ((0, 0), (0, LANES - 2 * GLA_RANK)))
    w2cat = jnp.zeros((LANES, 2 * gla_qk), F32)
    w2cat = w2cat.at[:GLA_RANK, :gla_qk].set(gla_gate_w2[0, 0])
    w2cat = w2cat.at[GLA_RANK:2 * GLA_RANK, gla_qk:].set(gla_gate_w2[0, 1]).astype(BF16)
    b2cat = gla_gate_b[0].reshape(1, 2 * gla_qk)

    rows = -(-(b_ + 1) // 8) * 8
    c_rows = jnp.concatenate([c, c_ctx[None, :], jnp.zeros((rows - b_ - 1, d), F32)], axis=0)
    mods = _mod_table(c_rows, w_ada[0], b_ada).reshape(rows, N_MOD, d)
    modt = jnp.stack([mods[:b_], jnp.broadcast_to(mods[b_], (b_, N_MOD, d))], axis=1)
    modt = jnp.pad(modt, ((0, 0), (0, 0), (0, MOD_ROWS - N_MOD), (0, 0)))

    xs = jnp.concatenate([x, ctx, jnp.zeros((b_, lp - seq - n_ctx, d), F32)], axis=1)
    tiles = dict(n_lat_tiles=n_lat_tiles, n_tiles=n_tiles)

    x1 = _ffn(xs, modt, ffn1_norm, ffn1_w1[0].astype(BF16), ffn1_w3[0].astype(BF16),
              ffn1_w2[0].astype(BF16), row0=0, **tiles)

    gq, gk, gv, sgg, logf = _proj_gla(x1, modt, mix_norm, w_gq, w_gk, w_gv, w_gg, w_lr, w2cat, b2cat, **tiles)
    cos_t, sin_t, tab_t = _rope_tables(seq, lp)
    dq, dk, dv, smg = _proj_dif(x1, modt, mix_norm, cos_t, sin_t, tab_t, w_dq.T, w_dk, w_dv.T, w_mg, **tiles)

    o_f, o_b = _gla(gq, gk, gv, logf, n_lat_blocks=seq // GLA_BLOCK, n_ctx_blocks=n_ctx // GLA_BLOCK)

    lam_p = jnp.pad(diff_lambda[0], ((0, 4), (0, LANES - DIFF_DH)))
    o_d = _attn(dq, dk, dv, lam_p, seq=seq, n_keys=seq + n_ctx, lam_init=lam_init)

    x2 = _merge(x1, modt, o_f, o_b, sgg, o_d, smg, gla_out_norm, diff_out_norm,
                w_branch_gla[0].astype(BF16), w_branch_diff[0].astype(BF16), w_out[0].astype(BF16),
                n_lat_tiles=n_lat_tiles, diff_scale=1.0 - lam_init)

    return _ffn(x2, modt, ffn2_norm, ffn2_w1[0].astype(BF16), ffn2_w3[0].astype(BF16),
                ffn2_w2[0].astype(BF16), row0=6, n_lat_tiles=n_lat_tiles, n_tiles=n_lat_tiles,
                final_gain=final_norm.reshape(1, d))
```

```python
import functools
import math

import jax
import jax.numpy as jnp
import numpy as np
from jax import lax
from jax.experimental import pallas as pl
from jax.experimental.pallas import tpu as pltpu

F32 = jnp.float32
BF16 = jnp.bfloat16

EPS = 1e-6
GRID_W = 64
N_MOD = 9
MOD_ROWS = 16
GLA_HEADS = 4
GLA_RANK = 16
GLA_TAU = 16.0
GLA_CHUNK = 64
GLA_BLOCK = 256
DIFF_HEADS = 8
DIFF_DH = 64
ROPE_BASE = 10000.0
LANES = 128
MXU_TILE = 256
BF16_ROWS = 16
LOG2_E = 1.4426950408889634

TOKEN_TILE = 512
ATTN_TQ = 1024
ATTN_TK = 768
ATTN_QGROUP = 256
VMEM_LIMIT = 52 * 1024 * 1024


def _resident(shape):
    zeros = (0,) * len(shape)
    return pl.BlockSpec(shape, lambda *_: zeros, pipeline_mode=pl.Buffered(1))


def _params(semantics):
    return pltpu.CompilerParams(dimension_semantics=semantics, vmem_limit_bytes=VMEM_LIMIT)


def _dot(a, b):
    return jnp.dot(a, b, preferred_element_type=F32)


def _dot_nt(a, b):
    return lax.dot_general(a, b, (((1,), (1,)), ((), ())), preferred_element_type=F32)


def _dot_tn(a, b):
    return lax.dot_general(a, b, (((0,), (0,)), ((), ())), preferred_element_type=F32)


def _sigmoid(x):
    return 1.0 / (1.0 + jnp.exp(-x))


def _rms(x):
    return x * lax.rsqrt(jnp.mean(x * x, axis=-1, keepdims=True) + EPS)


def _modnorm(x, gain, shift, scale1):
    return (_rms(x) * gain) * scale1 + shift


def _mod_kernel(c_ref, w_ref, b_ref, o_ref):
    k = pl.program_id(0)
    c = c_ref[...]
    s = c * _sigmoid(c)
    acc = jnp.zeros(o_ref.shape, F32)
    w = w_ref[...]
    s_parts = _split3(s)
    w_parts = _split3(w)
    for i in range(3):
        for j in range(3 - i):
            acc = acc + _dot(s_parts[i], w_parts[j])
    val = acc + b_ref[...]
    add = jnp.where(k % 3 == 1, 1.0, 0.0).astype(F32)
    mul = jnp.where((k == 2) | (k == 8), 0.5, 1.0).astype(F32)
    o_ref[...] = (val + add) * mul


def _split3(x):
    hi = x.astype(BF16)
    r1 = x - hi.astype(F32)
    mid = r1.astype(BF16)
    lo = (r1 - mid.astype(F32)).astype(BF16)
    return hi, mid, lo


def _mod_table(c_rows, w_ada, b_ada):
    rows, d = c_rows.shape
    return pl.pallas_call(
        _mod_kernel,
        out_shape=jax.ShapeDtypeStruct((rows, N_MOD * d), F32),
        grid=(N_MOD,),
        in_specs=[
            pl.BlockSpec((rows, d), lambda k: (0, 0)),
            pl.BlockSpec((d, d), lambda k: (0, k)),
            pl.BlockSpec((1, d), lambda k: (0, k)),
        ],
        out_specs=pl.BlockSpec((rows, d), lambda k: (0, k)),
        compiler_params=_params(("arbitrary",)),
        name="mod_table",
    )(c_rows, w_ada, b_ada)


def _ffn_kernel(x_ref, mod_ref, g_ref, w1_ref, w3_ref, w2_ref, *rest, row0, f_chunk, n_lat_tiles, final):
    if final:
        gf_ref, o_ref = rest
        x = x_ref[0]
    else:
        c_ref, o_ref = rest
        x = jnp.where(pl.program_id(1) >= n_lat_tiles, c_ref[0], x_ref[0])
    mod = mod_ref[0, 0]
    shift, scale1, gate = mod[row0:row0 + 1], mod[row0 + 1:row0 + 2], mod[row0 + 2:row0 + 3]
    h = _modnorm(x, g_ref[...], shift, scale1).astype(BF16)
    acc = jnp.zeros(x.shape, F32)
    f_total = w1_ref.shape[1]
    for f0 in range(0, f_total, f_chunk):
        f1 = min(f0 + f_chunk, f_total)
        a = _dot(h, w1_ref[:, f0:f1])
        b = _dot(h, w3_ref[:, f0:f1])
        u = (a * _sigmoid(a) * b).astype(BF16)
        acc = acc + _dot(u, w2_ref[f0:f1, :])
    y = x + gate * acc
    if final:
        y = _rms(y) * gf_ref[...]
    o_ref[0] = y


def _ffn(x, modt, gain, w1, w3, w2, *, row0, n_lat_tiles, n_tiles, ctx_tile=None, final_gain=None):
    b_, _, d = x.shape
    f = w1.shape[1]
    tm = TOKEN_TILE
    final = final_gain is not None
    assert final != (ctx_tile is not None)
    f_chunk = -(-f // (2 * MXU_TILE)) * MXU_TILE
    kern = functools.partial(_ffn_kernel, row0=row0, f_chunk=f_chunk, n_lat_tiles=n_lat_tiles, final=final)
    in_specs = [
        pl.BlockSpec((1, tm, d), lambda b, j: (b, jnp.minimum(j, n_lat_tiles - 1), 0)),
        pl.BlockSpec((1, 1, MOD_ROWS, d), lambda b, j: (b, (j >= n_lat_tiles).astype(jnp.int32), 0, 0)),
        _resident((1, d)),
        _resident((d, f)),
        _resident((d, f)),
        _resident((f, d)),
    ]
    args = [x, modt, gain, w1, w3, w2]
    if final:
        in_specs.append(_resident((1, d)))
        args.append(final_gain)
    else:
        in_specs.append(pl.BlockSpec((1, tm, d), lambda b, j: (b, 0, 0)))
        args.append(ctx_tile)
    return pl.pallas_call(
        kern,
        out_shape=jax.ShapeDtypeStruct((b_, n_tiles * tm, d), F32),
        grid=(b_, n_tiles),
        in_specs=in_specs,
        out_specs=pl.BlockSpec((1, tm, d), lambda b, j: (b, j, 0)),
        compiler_params=_params(("parallel", "parallel")),
        name="ffn_final" if final else "ffn",
    )(*args)


def _proj_gla_kernel(x_ref, mod_ref, g_ref, wq_ref, wk_ref, wv_ref, wg_ref, wlr_ref, w2_ref, b2_ref,
                     q_ref, k_ref, v_ref, sg_ref, lf_ref, *, q_scale):
    x = x_ref[0]
    mod = mod_ref[0, 0]
    h = _modnorm(x, g_ref[...], mod[3:4], mod[4:5]).astype(BF16)
    lr = _dot(h, wlr_ref[...]).astype(BF16)
    z = _dot(lr, w2_ref[...]) + b2_ref[...]
    lf_ref[0] = (jnp.minimum(z, 0.0) - jnp.log(1.0 + jnp.exp(-jnp.abs(z)))) * (1.0 / GLA_TAU)
    gg = _dot(h, wg_ref[...])
    sg_ref[0] = gg * _sigmoid(gg)
    v_ref[0] = _dot(h, wv_ref[...]).astype(BF16)
    q_ref[0] = _dot(h, wq_ref[...]) * q_scale
    k_ref[0] = _dot(h, wk_ref[...])


def _proj_gla(xs, modt, gain, wq, wk, wv, wg, wlr, w2cat, b2cat, *, n_lat_tiles, n_tiles):
    b_, lp, d = xs.shape
    tm = TOKEN_TILE
    qk = wq.shape[1]
    vv = wv.shape[1]
    row = lambda width: pl.BlockSpec((1, tm, width), lambda b, j: (b, j, 0))
    kern = functools.partial(_proj_gla_kernel, q_scale=(qk // GLA_HEADS) ** -0.5)
    return pl.pallas_call(
        kern,
        out_shape=(
            jax.ShapeDtypeStruct((b_, lp, qk), F32),
            jax.ShapeDtypeStruct((b_, lp, qk), F32),
            jax.ShapeDtypeStruct((b_, lp, vv), BF16),
            jax.ShapeDtypeStruct((b_, lp, vv), F32),
            jax.ShapeDtypeStruct((b_, lp, 2 * qk), F32),
        ),
        grid=(b_, n_tiles),
        in_specs=[
            row(d),
            pl.BlockSpec((1, 1, MOD_ROWS, d), lambda b, j: (b, (j >= n_lat_tiles).astype(jnp.int32), 0, 0)),
            _resident((1, d)),
            _resident(wq.shape), _resident(wk.shape), _resident(wv.shape), _resident(wg.shape),
            _resident(wlr.shape), _resident(w2cat.shape), _resident(b2cat.shape),
        ],
        out_specs=(row(qk), row(qk), row(vv), row(vv), row(2 * qk)),
        compiler_params=_params(("parallel", "parallel")),
        name="proj_gla",
    )(xs, modt, gain, wq, wk, wv, wg, wlr, w2cat, b2cat)


def _rope(u, cos, sin_signed, first_half):
    out = []
    for g in range(u.shape[1] // LANES):
        ug = u[:, g * LANES:(g + 1) * LANES]
        partner = jnp.where(first_half, pltpu.roll(ug, LANES - 16, 1), pltpu.roll(ug, 16, 1))
        out.append(ug * cos + partner * sin_signed)
    return jnp.concatenate(out, axis=1)


def _proj_dif_kernel(x_ref, mod_ref, g_ref, cos_ref, sin_ref, tab_t_ref, wq_t_ref, wk_ref, wv_t_ref, wm_ref,
                     q_t_ref, k_ref, v_t_ref, sm_ref, *, q_scale):
    x = x_ref[0]
    mod = mod_ref[0, 0]
    h = _modnorm(x, g_ref[...], mod[3:4], mod[4:5]).astype(BF16)
    q_t = _dot_nt(wq_t_ref[...], h)
    quarter = DIFF_DH // 4
    tab = tab_t_ref[...]
    for base in range(0, q_t.shape[0], 2 * quarter):
        t0 = base % DIFF_DH
        cos = tab[t0:t0 + quarter]
        sin = tab[t0 + quarter:t0 + 2 * quarter]
        u1 = q_t[base:base + quarter]
        u2 = q_t[base + quarter:base + 2 * quarter]
        q_t_ref[0, base:base + quarter, :] = ((u1 * cos - u2 * sin) * q_scale).astype(BF16)
        q_t_ref[0, base + quarter:base + 2 * quarter, :] = ((u2 * cos + u1 * sin) * q_scale).astype(BF16)
    cos = cos_ref[...]
    sin_signed = sin_ref[...]
    lane = lax.broadcasted_iota(jnp.int32, cos.shape, 1)
    first_half = (lane % 32) < 16
    k_ref[0] = _rope(_dot(h, wk_ref[...]), cos, sin_signed, first_half).astype(BF16)
    v_t_ref[0] = _dot_nt(wv_t_ref[...], h).astype(BF16)
    sm_ref[0] = _sigmoid(_dot(h, wm_ref[...]))


def _proj_dif(xs, modt, gain, cos_t, sin_t, tab_t, wq_t, wk, wv_t, wm, *, n_lat_tiles, n_tiles):
    b_, lp, d = xs.shape
    tm = TOKEN_TILE
    row = lambda width: pl.BlockSpec((1, tm, width), lambda b, j: (b, j, 0))
    col = lambda height: pl.BlockSpec((1, height, tm), lambda b, j: (b, 0, j))
    tab = pl.BlockSpec((tm, LANES), lambda b, j: (j, 0))
    kern = functools.partial(_proj_dif_kernel, q_scale=DIFF_DH ** -0.5 * LOG2_E)
    return pl.pallas_call(
        kern,
        out_shape=(
            jax.ShapeDtypeStruct((b_, wq_t.shape[0], lp), BF16),
            jax.ShapeDtypeStruct((b_, lp, wk.shape[1]), BF16),
            jax.ShapeDtypeStruct((b_, wv_t.shape[0], lp), BF16),
            jax.ShapeDtypeStruct((b_, lp, wm.shape[1]), F32),
        ),
        grid=(b_, n_tiles),
        in_specs=[
            row(d),
            pl.BlockSpec((1, 1, MOD_ROWS, d), lambda b, j: (b, (j >= n_lat_tiles).astype(jnp.int32), 0, 0)),
            _resident((1, d)),
            tab, tab,
            pl.BlockSpec((DIFF_DH, tm), lambda b, j: (0, j)),
            _resident(wq_t.shape), _resident(wk.shape), _resident(wv_t.shape), _resident(wm.shape),
        ],
        out_specs=(col(wq_t.shape[0]), row(wk.shape[1]), col(wv_t.shape[0]), row(wm.shape[1])),
        compiler_params=_params(("parallel", "parallel")),
        name="proj_dif",
    )(xs, modt, gain, cos_t, sin_t, tab_t, wq_t, wk, wv_t, wm)


def _gla_decays(q, k, logf, tri, ref_row, end_row):
    c = GLA_CHUNK
    parts = _split3(logf)
    cum = _dot(tri, parts[0]) + _dot(tri, parts[1]) + _dot(tri, parts[2])
    n_chunks = q.shape[0] // c
    totals = [cum[a * c + end_row:a * c + end_row + 1] for a in range(n_chunks)]
    per_chunk = lambda rows: jnp.concatenate([jnp.broadcast_to(r, (c, r.shape[1])) for r in rows], axis=0)
    ref = per_chunk([cum[a * c + ref_row:a * c + ref_row + 1] for a in range(n_chunks)])
    total = per_chunk(totals)
    q_state = (q * jnp.exp(cum)).astype(BF16)
    q_in = (q * jnp.exp(cum - ref)).astype(BF16)
    k_in = (k * jnp.exp(ref - cum)).astype(BF16)
    k_out = (k * jnp.exp(total - cum)).astype(BF16)
    return q_state, q_in, k_in, k_out, [jnp.exp(t) for t in totals]


def _gla_kernel(qf_ref, kf_ref, vf_ref, lf_ref, qb_ref, kb_ref, vb_ref, lb_ref,
                of_ref, ob_ref, sf_ref, sb_ref):
    @pl.when(pl.program_id(1) == 0)
    def _():
        sf_ref[...] = jnp.zeros(sf_ref.shape, F32)
        sb_ref[...] = jnp.zeros(sb_ref.shape, F32)

    c = GLA_CHUNK
    rows = qf_ref.shape[1]
    n_chunks = rows // c
    row = lax.broadcasted_iota(jnp.int32, (rows, rows), 0)
    col = lax.broadcasted_iota(jnp.int32, (rows, rows), 1)
    same_chunk = (row // c) == (col // c)
    lower = same_chunk & (col <= row)
    upper = same_chunk & (col >= row)
    dirs = [
        (_gla_decays(qf_ref[0], kf_ref[0], lf_ref[0], jnp.where(lower, 1.0, 0.0).astype(BF16),
                     c // 2 - 1, c - 1), vf_ref[0], lower, sf_ref, of_ref, list(range(n_chunks))),
        (_gla_decays(qb_ref[0], kb_ref[0], lb_ref[0], jnp.where(upper, 1.0, 0.0).astype(BF16),
                     c // 2, 0), vb_ref[0], upper, sb_ref, ob_ref, list(range(n_chunks))[::-1]),
    ]
    dk = qf_ref.shape[2] // GLA_HEADS
    dv = vf_ref.shape[2] // GLA_HEADS
    chains = []
    for (q_state, q_in, k_in, k_out, decays), v, keep, s_ref, o_ref, order in dirs:
        for hd in range(GLA_HEADS):
            ks = slice(hd * dk, (hd + 1) * dk)
            vs = slice(hd * dv, (hd + 1) * dv)
            chains.append(dict(q_state=q_state[:, ks], q_in=q_in[:, ks], k_in=k_in[:, ks], k_out=k_out[:, ks],
                               decays=[d[:, ks] for d in decays], v=v[:, vs], keep=keep, s_ref=s_ref,
                               o_ref=o_ref, hd=hd, vs=vs, order=order))
    chunk = lambda x, a: x[a * c:(a + 1) * c]
    for ch in chains:
        ch["attn"] = _dot_nt(ch["q_in"], ch["k_in"])
    for ch in chains:
        ch["update"] = [_dot_tn(chunk(ch["v"], a), chunk(ch["k_out"], a)) for a in range(n_chunks)]
    for ch in chains:
        state_t = ch["s_ref"][ch["hd"]]
        ch["before"] = {}
        for a in ch["order"]:
            ch["before"][a] = state_t
            state_t = state_t * ch["decays"][a] + ch["update"][a]
        ch["s_ref"][ch["hd"]] = state_t
    for ch in chains:
        ch["inter"] = jnp.concatenate(
            [_dot_nt(chunk(ch["q_state"], a), ch["before"][a].astype(BF16)) for a in range(n_chunks)], axis=0)
    for ch in chains:
        attn = jnp.where(ch["keep"], ch["attn"], 0.0).astype(BF16)
        ch["o_ref"][0, :, ch["vs"]] = ch["inter"] + _dot(attn, ch["v"])


def _gla(gq, gk, gv, logf, *, n_lat_blocks, n_ctx_blocks):
    b_, lp, qk = gq.shape
    vv = gv.shape[2]
    c = GLA_BLOCK
    n = n_lat_blocks + n_ctx_blocks
    fwd = lambda b, s: (b, (s + n_lat_blocks) % n, 0)
    bwd = lambda b, s: (b, n - 1 - s, 0)
    lf_spec = pl.BlockSpec((1, c, qk), lambda b, s: (b, (s + n_lat_blocks) % n, 0))
    lb_spec = pl.BlockSpec((1, c, qk), lambda b, s: (b, n - 1 - s, 1))
    return pl.pallas_call(
        _gla_kernel,
        out_shape=(jax.ShapeDtypeStruct((b_, lp, vv), F32), jax.ShapeDtypeStruct((b_, lp, vv), F32)),
        grid=(b_, n),
        in_specs=[
            pl.BlockSpec((1, c, qk), fwd), pl.BlockSpec((1, c, qk), fwd), pl.BlockSpec((1, c, vv), fwd), lf_spec,
            pl.BlockSpec((1, c, qk), bwd), pl.BlockSpec((1, c, qk), bwd), pl.BlockSpec((1, c, vv), bwd), lb_spec,
        ],
        out_specs=(pl.BlockSpec((1, c, vv), fwd), pl.BlockSpec((1, c, vv), bwd)),
        scratch_shapes=[
            pltpu.VMEM((GLA_HEADS, vv // GLA_HEADS, qk // GLA_HEADS), F32),
            pltpu.VMEM((GLA_HEADS, vv // GLA_HEADS, qk // GLA_HEADS), F32),
        ],
        compiler_params=_params(("parallel", "arbitrary")),
        name="gla_scan",
    )(gq, gk, gv, logf, gq, gk, gv, logf)


def _attn_kernel(q_t_ref, k_ref, v_t_ref, lam_ref, o_ref, m_ref, acc_ref, s_ref, p_ref, a_ref,
                 *, tk, n_kblocks, lam_init):
    tq = q_t_ref.shape[2]
    q_t = q_t_ref[0]
    row = lax.broadcasted_iota(jnp.int32, q_t.shape, 0)
    zero = jnp.zeros_like(q_t)
    qz_t = jnp.concatenate([jnp.where(row < DIFF_DH, q_t, zero), jnp.where(row >= DIFF_DH, q_t, zero)], axis=1)
    m_ref[...] = jnp.full(m_ref.shape, -jnp.inf, F32)
    acc_ref[...] = jnp.zeros(acc_ref.shape, F32)

    groups = list(range(2 * tq // ATTN_QGROUP))
    lanes = lambda g: slice(g * ATTN_QGROUP, (g + 1) * ATTN_QGROUP)

    def scores(kb, slot, g):
        s_ref[slot, g] = _dot(kb, qz_t[:, lanes(g)])

    def softmax(slot, g):
        s_t = s_ref[slot, g]
        m_prev = m_ref[g]
        m_new = jnp.maximum(m_prev, jnp.max(s_t, axis=0, keepdims=True))
        m_ref[g] = m_new
        a_ref[slot, g] = jnp.exp2(m_prev - m_new)
        p_ref[slot, g] = jnp.exp2(s_t - m_new).astype(BF16)

    ones_rows = jnp.ones((BF16_ROWS, tk), BF16)

    def weighted_values(vb_t, slot, g):
        acc_ref[g] = a_ref[slot, g] * acc_ref[g] + _dot(vb_t, p_ref[slot, g])

    def load_keys(j):
        return k_ref[0, pl.ds(pl.multiple_of(j * tk, tk), tk), :]

    def load_values(j):
        vb_t = v_t_ref[0, :, pl.ds(pl.multiple_of(j * tk, tk), tk)]
        return jnp.concatenate([vb_t, ones_rows], axis=0)

    def stage(j, slot, first=False, last=False):
        kb = None if last else load_keys(j + 1)
        vb_t = None if first else load_values(j - 1)
        for g in groups:
            if not last:
                scores(kb, 1 - slot, g)
            if not first:
                weighted_values(vb_t, 1 - slot, g)
            softmax(slot, g)

    kb0 = load_keys(0)
    for g in groups:
        scores(kb0, 0, g)
    stage(0, 0, first=True, last=n_kblocks == 1)
    n_pairs = max(n_kblocks - 2, 0) // 2

    def pair(i, carry):
        stage(2 * i + 1, 1)
        stage(2 * i + 2, 0)
        return carry

    lax.fori_loop(0, n_pairs, pair, 0)
    for j in range(2 * n_pairs + 1, n_kblocks):
        stage(j, j % 2, last=j == n_kblocks - 1)
    vb_last = load_values(n_kblocks - 1)
    for g in groups:
        weighted_values(vb_last, (n_kblocks - 1) % 2, g)

    lp = lam_ref[...]
    lam = (jnp.exp(jnp.sum(lp[0:1] * lp[1:2], axis=-1, keepdims=True))
           - jnp.exp(jnp.sum(lp[2:3] * lp[3:4], axis=-1, keepdims=True)) + lam_init)
    dv = v_t_ref.shape[1]
    acc = jnp.concatenate([acc_ref[g] for g in groups], axis=1)
    o_t = acc[:dv] / acc[dv:dv + 1]
    o_ref[0] = (o_t[:, :tq] - lam * o_t[:, tq:]).T


def _attn(dq_t, dk, dv_t, lam_p, *, seq, n_keys, lam_init):
    b_, lp, width = dk.shape
    tq, tk = ATTN_TQ, ATTN_TK
    heads = width // LANES
    n_groups = 2 * tq // ATTN_QGROUP
    kern = functools.partial(_attn_kernel, tk=tk, n_kblocks=n_keys // tk, lam_init=lam_init)
    return pl.pallas_call(
        kern,
        out_shape=jax.ShapeDtypeStruct((b_, seq, width), F32),
        grid=(b_, heads, seq // tq),
        in_specs=[
            pl.BlockSpec((1, LANES, tq), lambda b, h, i: (b, h, i)),
            pl.BlockSpec((1, lp, LANES), lambda b, h, i: (b, 0, h)),
            pl.BlockSpec((1, LANES, lp), lambda b, h, i: (b, h, 0)),
            pl.BlockSpec(lam_p.shape, lambda b, h, i: (0, 0)),
        ],
        out_specs=pl.BlockSpec((1, tq, LANES), lambda b, h, i: (b, i, h)),
        scratch_shapes=[
            pltpu.VMEM((n_groups, 1, ATTN_QGROUP), F32),
            pltpu.VMEM((n_groups, LANES + BF16_ROWS, ATTN_QGROUP), F32),
            pltpu.VMEM((2, n_groups, tk, ATTN_QGROUP), F32),
            pltpu.VMEM((2, n_groups, tk, ATTN_QGROUP), BF16),
            pltpu.VMEM((2, n_groups, 1, ATTN_QGROUP), F32),
        ],
        compiler_params=_params(("parallel", "parallel", "parallel")),
        name="diff_attn",
    )(dq_t, dk, dv_t, lam_p)


def _head_rms(x, gain, width):
    parts = []
    for h0 in range(0, x.shape[1], width):
        parts.append(_rms(x[:, h0:h0 + width]))
    return jnp.concatenate(parts, axis=1) * gain


def _merge_kernel(x_ref, mod_ref, of_ref, ob_ref, sg_ref, od_ref, sm_ref, gg_ref, gd_ref,
                  wa_ref, wb_ref, wo_ref, o_ref, *, dv_gla, dv_diff, diff_scale):
    d = x_ref.shape[2]
    ya = _head_rms(of_ref[0] + ob_ref[0], gg_ref[...], dv_gla) * sg_ref[0]
    ya = _dot(ya.astype(BF16), wa_ref[...])
    yb = _head_rms(od_ref[0], gd_ref[...], dv_diff) * diff_scale
    yb = _dot(yb.astype(BF16), wb_ref[...])
    sm = sm_ref[0]
    merged = (sm[:, :d] * ya + sm[:, d:] * yb).astype(BF16)
    gate = mod_ref[0, 0][5:6]
    o_ref[0] = x_ref[0] + gate * _dot(merged, wo_ref[...])


def _merge(x1, modt, o_f, o_b, sgg, o_d, smg, g_gla, g_diff, wa, wb, wo, *, n_lat_tiles, diff_scale):
    b_, _, d = x1.shape
    tm = TOKEN_TILE
    row = lambda width: pl.BlockSpec((1, tm, width), lambda b, j: (b, j, 0))
    kern = functools.partial(_merge_kernel, dv_gla=o_f.shape[2] // GLA_HEADS,
                             dv_diff=o_d.shape[2] // DIFF_HEADS, diff_scale=diff_scale)
    return pl.pallas_call(
        kern,
        out_shape=jax.ShapeDtypeStruct((b_, n_lat_tiles * tm, d), F32),
        grid=(b_, n_lat_tiles),
        in_specs=[
            row(d),
            pl.BlockSpec((1, 1, MOD_ROWS, d), lambda b, j: (b, 0, 0, 0)),
            row(o_f.shape[2]), row(o_b.shape[2]), row(sgg.shape[2]), row(o_d.shape[2]), row(smg.shape[2]),
            _resident(g_gla.shape), _resident(g_diff.shape),
            _resident(wa.shape), _resident(wb.shape), _resident(wo.shape),
        ],
        out_specs=row(d),
        compiler_params=_params(("parallel", "parallel")),
        name="merge",
    )(x1, modt, o_f, o_b, sgg, o_d, smg, g_gla, g_diff, wa, wb, wo)


def _rope_tables(seq, lp):
    half = DIFF_DH // 2
    quarter = half // 2
    inv = ROPE_BASE ** (-np.arange(quarter, dtype=np.float32) / quarter)
    t = jnp.arange(seq, dtype=jnp.int32)
    ang_row = (t // GRID_W).astype(F32)[:, None] * inv[None, :]
    ang_col = (t % GRID_W).astype(F32)[:, None] * inv[None, :]
    ang = jnp.concatenate([ang_row, ang_row, ang_col, ang_col], axis=1)
    sign = np.tile(np.concatenate([-np.ones(quarter), np.ones(quarter)]), 2).astype(np.float32)
    cos = jnp.tile(jnp.cos(ang), (1, LANES // DIFF_DH))
    sin = jnp.tile(jnp.sin(ang) * sign[None, :], (1, LANES // DIFF_DH))
    pad = lp - seq
    cos = jnp.concatenate([cos, jnp.ones((pad, LANES), F32)], axis=0)
    sin = jnp.concatenate([sin, jnp.zeros((pad, LANES), F32)], axis=0)
    tab_t = jnp.concatenate([jnp.cos(ang_row), jnp.sin(ang_row), jnp.cos(ang_col), jnp.sin(ang_col)], axis=1).T
    ident = np.repeat(np.array([1.0, 0.0, 1.0, 0.0], np.float32), quarter)[:, None]
    tab_t = jnp.concatenate([tab_t, jnp.broadcast_to(ident, (DIFF_DH, pad))], axis=1)
    return cos, sin, tab_t


def kernel(x, c, ctx, c_ctx, w_ada, b_ada, ffn1_norm, ffn1_w1, ffn1_w3, ffn1_w2, mix_norm, w_in, gla_gate_w2, gla_gate_b, gla_out_norm, diff_lambda, diff_out_norm, w_branch_gla, w_branch_diff, w_out, ffn2_norm, ffn2_w1, ffn2_w3, ffn2_w2, final_norm):
    b_, seq, d = x.shape
    n_ctx = ctx.shape[1]
    assert w_ada.shape[0] == 1, "single-layer stack"
    tm = TOKEN_TILE
    assert seq % tm == 0 and seq % ATTN_TQ == 0 and n_ctx <= tm
    assert seq % GLA_BLOCK == 0 and n_ctx % GLA_BLOCK == 0 and (seq + n_ctx) % ATTN_TK == 0
    n_lat_tiles = seq // tm
    n_tiles = n_lat_tiles + 1
    lp = n_tiles * tm
    lam_init = 0.8 - 0.6 * math.exp(0.0)

    gla_qk = GLA_HEADS * (d // 8)
    gla_v = GLA_HEADS * (d // 4)
    dif = DIFF_HEADS * 2 * DIFF_DH
    splits = np.cumsum([gla_qk, gla_qk, gla_v, gla_v, 2 * GLA_RANK, dif, dif, dif, 2 * d])
    assert w_in.shape[2] == splits[-1]
    wi = w_in[0].astype(BF16)
    w_gq, w_gk, w_gv, w_gg, w_lr, w_dq, w_dk, w_dv, w_mg = jnp.split(wi, splits[:-1].tolist(), axis=1)
    w_lr = jnp.pad(w_lr, ((0, 0), (0, LANES - 2 * GLA_RANK)))
    w2cat = jnp.zeros((LANES, 2 * gla_qk), F32)
    w2cat = w2cat.at[:GLA_RANK, :gla_qk].set(gla_gate_w2[0, 0])
    w2cat = w2cat.at[GLA_RANK:2 * GLA_RANK, gla_qk:].set(gla_gate_w2[0, 1]).astype(BF16)
    b2cat = gla_gate_b[0].reshape(1, 2 * gla_qk)

    rows = -(-(b_ + 1) // 8) * 8
    c_rows = jnp.concatenate([c, c_ctx[None, :], jnp.zeros((rows - b_ - 1, d), F32)], axis=0)
    mods = _mod_table(c_rows, w_ada[0], b_ada).reshape(rows, N_MOD, d)
    modt = jnp.stack([mods[:b_], jnp.broadcast_to(mods[b_], (b_, N_MOD, d))], axis=1)
    modt = jnp.pad(modt, ((0, 0), (0, 0), (0, MOD_ROWS - N_MOD), (0, 0)))

    ctx_tile = jnp.pad(ctx, ((0, 0), (0, tm - n_ctx), (0, 0)))
    tiles = dict(n_lat_tiles=n_lat_tiles, n_tiles=n_tiles)

    x1 = _ffn(x, modt, ffn1_norm, ffn1_w1[0].astype(BF16), ffn1_w3[0].astype(BF16),
              ffn1_w2[0].astype(BF16), row0=0, ctx_tile=ctx_tile, **tiles)

    gq, gk, gv, sgg, logf = _proj_gla(x1, modt, mix_norm, w_gq, w_gk, w_gv, w_gg, w_lr, w2cat, b2cat, **tiles)
    cos_t, sin_t, tab_t = _rope_tables(seq, lp)
    dq, dk, dv, smg = _proj_dif(x1, modt, mix_norm, cos_t, sin_t, tab_t, w_dq.T, w_dk, w_dv.T, w_mg, **tiles)

    o_f, o_b = _gla(gq, gk, gv, logf, n_lat_blocks=seq // GLA_BLOCK, n_ctx_blocks=n_ctx // GLA_BLOCK)

    lam_p = jnp.pad(diff_lambda[0], ((0, 4), (0, LANES - DIFF_DH)))
    o_d = _attn(dq, dk, dv, lam_p, seq=seq, n_keys=seq + n_ctx, lam_init=lam_init)

    x2 = _merge(x1, modt, o_f, o_b, sgg, o_d, smg, gla_out_norm, diff_out_norm,
                w_branch_gla[0].astype(BF16), w_branch_diff[0].astype(BF16), w_out[0].astype(BF16),
                n_lat_tiles=n_lat_tiles, diff_scale=1.0 - lam_init)

    return _ffn(x2, modt, ffn2_norm, ffn2_w1[0].astype(BF16), ffn2_w3[0].astype(BF16),
                ffn2_w2[0].astype(BF16), row0=6, n_lat_tiles=n_lat_tiles, n_tiles=n_lat_tiles,
                final_gain=final_norm.reshape(1, d))
```

```python
import functools
import math

import jax
import jax.numpy as jnp
import numpy as np
from jax import lax
from jax.experimental import pallas as pl
from jax.experimental.pallas import tpu as pltpu

F32 = jnp.float32
BF16 = jnp.bfloat16

EPS = 1e-6
GRID_W = 64
N_MOD = 9
MOD_ROWS = 16
GLA_HEADS = 4
GLA_RANK = 16
GLA_TAU = 16.0
GLA_CHUNK = 64
GLA_BLOCK = 256
DIFF_HEADS = 8
DIFF_DH = 64
ROPE_BASE = 10000.0
LANES = 128
MXU_TILE = 256
BF16_ROWS = 16
LOG2_E = 1.4426950408889634

TOKEN_TILE = 512
ATTN_TQ = 1024
ATTN_TK = 768
ATTN_QGROUP = 256
VMEM_LIMIT = 52 * 1024 * 1024


def _resident(shape):
    zeros = (0,) * len(shape)
    return pl.BlockSpec(shape, lambda *_: zeros, pipeline_mode=pl.Buffered(1))


def _params(semantics):
    return pltpu.CompilerParams(dimension_semantics=semantics, vmem_limit_bytes=VMEM_LIMIT)


def _dot(a, b):
    return jnp.dot(a, b, preferred_element_type=F32)


def _dot_nt(a, b):
    return lax.dot_general(a, b, (((1,), (1,)), ((), ())), preferred_element_type=F32)


def _dot_tn(a, b):
    return lax.dot_general(a, b, (((0,), (0,)), ((), ())), preferred_element_type=F32)


def _sigmoid(x):
    return 1.0 / (1.0 + jnp.exp(-x))


def _rms(x):
    return x * lax.rsqrt(jnp.mean(x * x, axis=-1, keepdims=True) + EPS)


def _modnorm(x, gain, shift, scale1):
    return (_rms(x) * gain) * scale1 + shift


def _mod_kernel(c_ref, w_ref, b_ref, o_ref):
    k = pl.program_id(0)
    c = c_ref[...]
    s = c * _sigmoid(c)
    acc = jnp.zeros(o_ref.shape, F32)
    w = w_ref[...]
    s_parts = _split3(s)
    w_parts = _split3(w)
    for i in range(3):
        for j in range(3 - i):
            acc = acc + _dot(s_parts[i], w_parts[j])
    val = acc + b_ref[...]
    add = jnp.where(k % 3 == 1, 1.0, 0.0).astype(F32)
    mul = jnp.where((k == 2) | (k == 8), 0.5, 1.0).astype(F32)
    o_ref[...] = (val + add) * mul


def _split3(x):
    hi = x.astype(BF16)
    r1 = x - hi.astype(F32)
    mid = r1.astype(BF16)
    lo = (r1 - mid.astype(F32)).astype(BF16)
    return hi, mid, lo


def _mod_table(c_rows, w_ada, b_ada):
    rows, d = c_rows.shape
    return pl.pallas_call(
        _mod_kernel,
        out_shape=jax.ShapeDtypeStruct((rows, N_MOD * d), F32),
        grid=(N_MOD,),
        in_specs=[
            pl.BlockSpec((rows, d), lambda k: (0, 0)),
            pl.BlockSpec((d, d), lambda k: (0, k)),
            pl.BlockSpec((1, d), lambda k: (0, k)),
        ],
        out_specs=pl.BlockSpec((rows, d), lambda k: (0, k)),
        compiler_params=_params(("arbitrary",)),
        name="mod_table",
    )(c_rows, w_ada, b_ada)


def _ffn_kernel(x_ref, mod_ref, g_ref, w1_ref, w3_ref, w2_ref, *rest, row0, f_chunk, n_lat_tiles, final):
    if final:
        gf_ref, o_ref = rest
        x = x_ref[0]
    else:
        c_ref, o_ref = rest
        x = jnp.where(pl.program_id(1) >= n_lat_tiles, c_ref[0], x_ref[0])
    mod = mod_ref[0, 0]
    shift, scale1, gate = mod[row0:row0 + 1], mod[row0 + 1:row0 + 2], mod[row0 + 2:row0 + 3]
    h = _modnorm(x, g_ref[...], shift, scale1).astype(BF16)
    acc = jnp.zeros(x.shape, F32)
    f_total = w1_ref.shape[1]
    for f0 in range(0, f_total, f_chunk):
        f1 = min(f0 + f_chunk, f_total)
        a = _dot(h, w1_ref[:, f0:f1])
        b = _dot(h, w3_ref[:, f0:f1])
        u = (a * _sigmoid(a) * b).astype(BF16)
        acc = acc + _dot(u, w2_ref[f0:f1, :])
    y = x + gate * acc
    if final:
        y = _rms(y) * gf_ref[...]
    o_ref[0] = y


def _ffn(x, modt, gain, w1, w3, w2, *, row0, n_lat_tiles, n_tiles, ctx_tile=None, final_gain=None):
    b_, _, d = x.shape
    f = w1.shape[1]
    tm = TOKEN_TILE
    final = final_gain is not None
    assert final != (ctx_tile is not None)
    f_chunk = -(-f // (2 * MXU_TILE)) * MXU_TILE
    kern = functools.partial(_ffn_kernel, row0=row0, f_chunk=f_chunk, n_lat_tiles=n_lat_tiles, final=final)
    in_specs = [
        pl.BlockSpec((1, tm, d), lambda b, j: (b, jnp.minimum(j, n_lat_tiles - 1), 0)),
        pl.BlockSpec((1, 1, MOD_ROWS, d), lambda b, j: (b, (j >= n_lat_tiles).astype(jnp.int32), 0, 0)),
        _resident((1, d)),
        _resident((d, f)),
        _resident((d, f)),
        _resident((f, d)),
    ]
    args = [x, modt, gain, w1, w3, w2]
    if final:
        in_specs.append(_resident((1, d)))
        args.append(final_gain)
    else:
        in_specs.append(pl.BlockSpec((1, tm, d), lambda b, j: (b, 0, 0)))
        args.append(ctx_tile)
    return pl.pallas_call(
        kern,
        out_shape=jax.ShapeDtypeStruct((b_, n_tiles * tm, d), F32),
        grid=(b_, n_tiles),
        in_specs=in_specs,
        out_specs=pl.BlockSpec((1, tm, d), lambda b, j: (b, j, 0)),
        compiler_params=_params(("parallel", "parallel")),
        name="ffn_final" if final else "ffn",
    )(*args)


def _proj_gla_kernel(x_ref, mod_ref, g_ref, wq_ref, wk_ref, wv_ref, wg_ref, wlr_ref, w2_ref, b2_ref,
                     q_ref, k_ref, v_ref, sg_ref, lf_ref, *, q_scale):
    x = x_ref[0]
    mod = mod_ref[0, 0]
    h = _modnorm(x, g_ref[...], mod[3:4], mod[4:5]).astype(BF16)
    lr = _dot(h, wlr_ref[...]).astype(BF16)
    z = _dot(lr, w2_ref[...]) + b2_ref[...]
    lf_ref[0] = (jnp.minimum(z, 0.0) - jnp.log(1.0 + jnp.exp(-jnp.abs(z)))) * (1.0 / GLA_TAU)
    gg = _dot(h, wg_ref[...])
    sg_ref[0] = gg * _sigmoid(gg)
    v_ref[0] = _dot(h, wv_ref[...]).astype(BF16)
    q_ref[0] = _dot(h, wq_ref[...]) * q_scale
    k_ref[0] = _dot(h, wk_ref[...])


def _proj_gla(xs, modt, gain, wq, wk, wv, wg, wlr, w2cat, b2cat, *, n_lat_tiles, n_tiles):
    b_, lp, d = xs.shape
    tm = TOKEN_TILE
    qk = wq.shape[1]
    vv = wv.shape[1]
    row = lambda width: pl.BlockSpec((1, tm, width), lambda b, j: (b, j, 0))
    kern = functools.partial(_proj_gla_kernel, q_scale=(qk // GLA_HEADS) ** -0.5)
    return pl.pallas_call(
        kern,
        out_shape=(
            jax.ShapeDtypeStruct((b_, lp, qk), F32),
            jax.ShapeDtypeStruct((b_, lp, qk), F32),
            jax.ShapeDtypeStruct((b_, lp, vv), BF16),
            jax.ShapeDtypeStruct((b_, lp, vv), F32),
            jax.ShapeDtypeStruct((b_, lp, 2 * qk), F32),
        ),
        grid=(b_, n_tiles),
        in_specs=[
            row(d),
            pl.BlockSpec((1, 1, MOD_ROWS, d), lambda b, j: (b, (j >= n_lat_tiles).astype(jnp.int32), 0, 0)),
            _resident((1, d)),
            _resident(wq.shape), _resident(wk.shape), _resident(wv.shape), _resident(wg.shape),
            _resident(wlr.shape), _resident(w2cat.shape), _resident(b2cat.shape),
        ],
        out_specs=(row(qk), row(qk), row(vv), row(vv), row(2 * qk)),
        compiler_params=_params(("parallel", "parallel")),
        name="proj_gla",
    )(xs, modt, gain, wq, wk, wv, wg, wlr, w2cat, b2cat)


def _rope(u, cos, sin_signed, first_half):
    out = []
    for g in range(u.shape[1] // LANES):
        ug = u[:, g * LANES:(g + 1) * LANES]
        partner = jnp.where(first_half, pltpu.roll(ug, LANES - 16, 1), pltpu.roll(ug, 16, 1))
        out.append(ug * cos + partner * sin_signed)
    return jnp.concatenate(out, axis=1)


def _proj_dif_kernel(x_ref, mod_ref, g_ref, cos_ref, sin_ref, tab_t_ref, wq_t_ref, wk_ref, wv_t_ref, wm_ref,
                     q_t_ref, k_ref, v_t_ref, sm_ref, *, q_scale):
    x = x_ref[0]
    mod = mod_ref[0, 0]
    h = _modnorm(x, g_ref[...], mod[3:4], mod[4:5]).astype(BF16)
    q_t = _dot_nt(wq_t_ref[...], h)
    quarter = DIFF_DH // 4
    tab = tab_t_ref[...]
    for base in range(0, q_t.shape[0], 2 * quarter):
        t0 = base % DIFF_DH
        cos = tab[t0:t0 + quarter]
        sin = tab[t0 + quarter:t0 + 2 * quarter]
        u1 = q_t[base:base + quarter]
        u2 = q_t[base + quarter:base + 2 * quarter]
        q_t_ref[0, base:base + quarter, :] = ((u1 * cos - u2 * sin) * q_scale).astype(BF16)
        q_t_ref[0, base + quarter:base + 2 * quarter, :] = ((u2 * cos + u1 * sin) * q_scale).astype(BF16)
    cos = cos_ref[...]
    sin_signed = sin_ref[...]
    lane = lax.broadcasted_iota(jnp.int32, cos.shape, 1)
    first_half = (lane % 32) < 16
    k_ref[0] = _rope(_dot(h, wk_ref[...]), cos, sin_signed, first_half).astype(BF16)
    v_t_ref[0] = _dot_nt(wv_t_ref[...], h).astype(BF16)
    sm_ref[0] = _sigmoid(_dot(h, wm_ref[...]))


def _proj_dif(xs, modt, gain, cos_t, sin_t, tab_t, wq_t, wk, wv_t, wm, *, n_lat_tiles, n_tiles):
    b_, lp, d = xs.shape
    tm = TOKEN_TILE
    row = lambda width: pl.BlockSpec((1, tm, width), lambda b, j: (b, j, 0))
    col = lambda height: pl.BlockSpec((1, height, tm), lambda b, j: (b, 0, j))
    tab = pl.BlockSpec((tm, LANES), lambda b, j: (j, 0))
    kern = functools.partial(_proj_dif_kernel, q_scale=DIFF_DH ** -0.5 * LOG2_E)
    return pl.pallas_call(
        kern,
        out_shape=(
            jax.ShapeDtypeStruct((b_, wq_t.shape[0], lp), BF16),
            jax.ShapeDtypeStruct((b_, lp, wk.shape[1]), BF16),
            jax.ShapeDtypeStruct((b_, wv_t.shape[0], lp), BF16),
            jax.ShapeDtypeStruct((b_, lp, wm.shape[1]), F32),
        ),
        grid=(b_, n_tiles),
        in_specs=[
            row(d),
            pl.BlockSpec((1, 1, MOD_ROWS, d), lambda b, j: (b, (j >= n_lat_tiles).astype(jnp.int32), 0, 0)),
            _resident((1, d)),
            tab, tab,
            pl.BlockSpec((DIFF_DH, tm), lambda b, j: (0, j)),
            _resident(wq_t.shape), _resident(wk.shape), _resident(wv_t.shape), _resident(wm.shape),
        ],
        out_specs=(col(wq_t.shape[0]), row(wk.shape[1]), col(wv_t.shape[0]), row(wm.shape[1])),
        compiler_params=_params(("parallel", "parallel")),
        name="proj_dif",
    )(xs, modt, gain, cos_t, sin_t, tab_t, wq_t, wk, wv_t, wm)


def _gla_decays(q, k, logf, tri, ref_row, end_row):
    c = GLA_CHUNK
    parts = _split3(logf)
    cum = _dot(tri, parts[0]) + _dot(tri, parts[1]) + _dot(tri, parts[2])
    n_chunks = q.shape[0] // c
    totals = [cum[a * c + end_row:a * c + end_row + 1] for a in range(n_chunks)]
    per_chunk = lambda rows: jnp.concatenate([jnp.broadcast_to(r, (c, r.shape[1])) for r in rows], axis=0)
    ref = per_chunk([cum[a * c + ref_row:a * c + ref_row + 1] for a in range(n_chunks)])
    total = per_chunk(totals)
    q_state = (q * jnp.exp(cum)).astype(BF16)
    q_in = (q * jnp.exp(cum - ref)).astype(BF16)
    k_in = (k * jnp.exp(ref - cum)).astype(BF16)
    k_out = (k * jnp.exp(total - cum)).astype(BF16)
    return q_state, q_in, k_in, k_out, [jnp.exp(t) for t in totals]


def _gla_kernel(qf_ref, kf_ref, vf_ref, lf_ref, qb_ref, kb_ref, vb_ref, lb_ref,
                of_ref, ob_ref, sf_ref, sb_ref):
    @pl.when(pl.program_id(1) == 0)
    def _():
        sf_ref[...] = jnp.zeros(sf_ref.shape, F32)
        sb_ref[...] = jnp.zeros(sb_ref.shape, F32)

    c = GLA_CHUNK
    rows = qf_ref.shape[1]
    n_chunks = rows // c
    row = lax.broadcasted_iota(jnp.int32, (rows, rows), 0)
    col = lax.broadcasted_iota(jnp.int32, (rows, rows), 1)
    same_chunk = (row // c) == (col // c)
    lower = same_chunk & (col <= row)
    upper = same_chunk & (col >= row)
    dirs = [
        (_gla_decays(qf_ref[0], kf_ref[0], lf_ref[0], jnp.where(lower, 1.0, 0.0).astype(BF16),
                     c // 2 - 1, c - 1), vf_ref[0], lower, sf_ref, of_ref, list(range(n_chunks))),
        (_gla_decays(qb_ref[0], kb_ref[0], lb_ref[0], jnp.where(upper, 1.0, 0.0).astype(BF16),
                     c // 2, 0), vb_ref[0], upper, sb_ref, ob_ref, list(range(n_chunks))[::-1]),
    ]
    dk = qf_ref.shape[2] // GLA_HEADS
    dv = vf_ref.shape[2] // GLA_HEADS
    chains = []
    for (q_state, q_in, k_in, k_out, decays), v, keep, s_ref, o_ref, order in dirs:
        for hd in range(GLA_HEADS):
            ks = slice(hd * dk, (hd + 1) * dk)
            vs = slice(hd * dv, (hd + 1) * dv)
            chains.append(dict(q_state=q_state[:, ks], q_in=q_in[:, ks], k_in=k_in[:, ks], k_out=k_out[:, ks],
                               decays=[d[:, ks] for d in decays], v=v[:, vs], keep=keep, s_ref=s_ref,
                               o_ref=o_ref, hd=hd, vs=vs, order=order))
    chunk = lambda x, a: x[a * c:(a + 1) * c]
    for ch in chains:
        ch["attn"] = _dot_nt(ch["q_in"], ch["k_in"])
    for ch in chains:
        ch["update"] = [_dot_tn(chunk(ch["v"], a), chunk(ch["k_out"], a)) for a in range(n_chunks)]
    for ch in chains:
        state_t = ch["s_ref"][ch["hd"]]
        ch["before"] = {}
        for a in ch["order"]:
            ch["before"][a] = state_t
            state_t = state_t * ch["decays"][a] + ch["update"][a]
        ch["s_ref"][ch["hd"]] = state_t
    for ch in chains:
        ch["inter"] = jnp.concatenate(
            [_dot_nt(chunk(ch["q_state"], a), ch["before"][a].astype(BF16)) for a in range(n_chunks)], axis=0)
    for ch in chains:
        attn = jnp.where(ch["keep"], ch["attn"], 0.0).astype(BF16)
        ch["o_ref"][0, :, ch["vs"]] = ch["inter"] + _dot(attn, ch["v"])


def _gla(gq, gk, gv, logf, *, n_lat_blocks, n_ctx_blocks):
    b_, lp, qk = gq.shape
    vv = gv.shape[2]
    c = GLA_BLOCK
    n = n_lat_blocks + n_ctx_blocks
    fwd = lambda b, s: (b, (s + n_lat_blocks) % n, 0)
    bwd = lambda b, s: (b, n - 1 - s, 0)
    lf_spec = pl.BlockSpec((1, c, qk), lambda b, s: (b, (s + n_lat_blocks) % n, 0))
    lb_spec = pl.BlockSpec((1, c, qk), lambda b, s: (b, n - 1 - s, 1))
    return pl.pallas_call(
        _gla_kernel,
        out_shape=(jax.ShapeDtypeStruct((b_, lp, vv), F32), jax.ShapeDtypeStruct((b_, lp, vv), F32)),
        grid=(b_, n),
        in_specs=[
            pl.BlockSpec((1, c, qk), fwd), pl.BlockSpec((1, c, qk), fwd), pl.BlockSpec((1, c, vv), fwd), lf_spec,
            pl.BlockSpec((1, c, qk), bwd), pl.BlockSpec((1, c, qk), bwd), pl.BlockSpec((1, c, vv), bwd), lb_spec,
        ],
        out_specs=(pl.BlockSpec((1, c, vv), fwd), pl.BlockSpec((1, c, vv), bwd)),
        scratch_shapes=[
            pltpu.VMEM((GLA_HEADS, vv // GLA_HEADS, qk // GLA_HEADS), F32),
            pltpu.VMEM((GLA_HEADS, vv // GLA_HEADS, qk // GLA_HEADS), F32),
        ],
        compiler_params=_params(("parallel", "arbitrary")),
        name="gla_scan",
    )(gq, gk, gv, logf, gq, gk, gv, logf)


def _attn_kernel(q_t_ref, k_ref, v_t_ref, lam_ref, o_ref, m_ref, acc_ref, s_ref, p_ref, a_ref, bm_ref,
                 *, tk, n_kblocks, lam_init):
    tq = q_t_ref.shape[2]
    q_t = q_t_ref[0]
    row = lax.broadcasted_iota(jnp.int32, q_t.shape, 0)
    zero = jnp.zeros_like(q_t)
    qz_t = jnp.concatenate([jnp.where(row < DIFF_DH, q_t, zero), jnp.where(row >= DIFF_DH, q_t, zero)], axis=1)
    m_ref[...] = jnp.full(m_ref.shape, -jnp.inf, F32)
    acc_ref[...] = jnp.zeros(acc_ref.shape, F32)

    groups = list(range(2 * tq // ATTN_QGROUP))
    lanes = lambda g: slice(g * ATTN_QGROUP, (g + 1) * ATTN_QGROUP)

    def scores(kb, slot, g):
        s_t = _dot(kb, qz_t[:, lanes(g)])
        s_ref[slot, g] = s_t
        bm_ref[slot, g] = jnp.max(s_t, axis=0, keepdims=True)

    def softmax(slot, g):
        s_t = s_ref[slot, g]
        m_prev = m_ref[g]
        m_new = jnp.maximum(m_prev, bm_ref[slot, g])
        m_ref[g] = m_new
        a_ref[slot, g] = jnp.exp2(m_prev - m_new)
        p_ref[slot, g] = jnp.exp2(s_t - m_new).astype(BF16)

    ones_rows = jnp.ones((BF16_ROWS, tk), BF16)

    def weighted_values(vb_t, slot, g):
        acc_ref[g] = a_ref[slot, g] * acc_ref[g] + _dot(vb_t, p_ref[slot, g])

    def load_keys(j):
        return k_ref[0, pl.ds(pl.multiple_of(j * tk, tk), tk), :]

    def load_values(j):
        vb_t = v_t_ref[0, :, pl.ds(pl.multiple_of(j * tk, tk), tk)]
        return jnp.concatenate([vb_t, ones_rows], axis=0)

    def stage(j, slot, first=False, last=False):
        kb = None if last else load_keys(j + 1)
        vb_t = None if first else load_values(j - 1)
        for g in groups:
            if not last:
                scores(kb, 1 - slot, g)
            if not first:
                weighted_values(vb_t, 1 - slot, g)
            softmax(slot, g)

    kb0 = load_keys(0)
    for g in groups:
        scores(kb0, 0, g)
    stage(0, 0, first=True, last=n_kblocks == 1)
    n_pairs = max(n_kblocks - 2, 0) // 2

    def pair(i, carry):
        stage(2 * i + 1, 1)
        stage(2 * i + 2, 0)
        return carry

    lax.fori_loop(0, n_pairs, pair, 0)
    for j in range(2 * n_pairs + 1, n_kblocks):
        stage(j, j % 2, last=j == n_kblocks - 1)
    vb_last = load_values(n_kblocks - 1)
    for g in groups:
        weighted_values(vb_last, (n_kblocks - 1) % 2, g)

    lp = lam_ref[...]
    lam = (jnp.exp(jnp.sum(lp[0:1] * lp[1:2], axis=-1, keepdims=True))
           - jnp.exp(jnp.sum(lp[2:3] * lp[3:4], axis=-1, keepdims=True)) + lam_init)
    dv = v_t_ref.shape[1]
    acc = jnp.concatenate([acc_ref[g] for g in groups], axis=1)
    o_t = acc[:dv] / acc[dv:dv + 1]
    o_ref[0] = (o_t[:, :tq] - lam * o_t[:, tq:]).T


def _attn(dq_t, dk, dv_t, lam_p, *, seq, n_keys, lam_init):
    b_, lp, width = dk.shape
    tq, tk = ATTN_TQ, ATTN_TK
    heads = width // LANES
    n_groups = 2 * tq // ATTN_QGROUP
    kern = functools.partial(_attn_kernel, tk=tk, n_kblocks=n_keys // tk, lam_init=lam_init)
    return pl.pallas_call(
        kern,
        out_shape=jax.ShapeDtypeStruct((b_, seq, width), F32),
        grid=(b_, heads, seq // tq),
        in_specs=[
            pl.BlockSpec((1, LANES, tq), lambda b, h, i: (b, h, i)),
            pl.BlockSpec((1, lp, LANES), lambda b, h, i: (b, 0, h)),
            pl.BlockSpec((1, LANES, lp), lambda b, h, i: (b, h, 0)),
            pl.BlockSpec(lam_p.shape, lambda b, h, i: (0, 0)),
        ],
        out_specs=pl.BlockSpec((1, tq, LANES), lambda b, h, i: (b, i, h)),
        scratch_shapes=[
            pltpu.VMEM((n_groups, 1, ATTN_QGROUP), F32),
            pltpu.VMEM((n_groups, LANES + BF16_ROWS, ATTN_QGROUP), F32),
            pltpu.VMEM((2, n_groups, tk, ATTN_QGROUP), F32),
            pltpu.VMEM((2, n_groups, tk, ATTN_QGROUP), BF16),
            pltpu.VMEM((2, n_groups, 1, ATTN_QGROUP), F32),
            pltpu.VMEM((2, n_groups, 1, ATTN_QGROUP), F32),
        ],
        compiler_params=_params(("parallel", "parallel", "parallel")),
        name="diff_attn",
    )(dq_t, dk, dv_t, lam_p)


def _head_rms(x, gain, width):
    parts = []
    for h0 in range(0, x.shape[1], width):
        parts.append(_rms(x[:, h0:h0 + width]))
    return jnp.concatenate(parts, axis=1) * gain


def _merge_kernel(x_ref, mod_ref, of_ref, ob_ref, sg_ref, od_ref, sm_ref, gg_ref, gd_ref,
                  wa_ref, wb_ref, wo_ref, o_ref, *, dv_gla, dv_diff, diff_scale):
    d = x_ref.shape[2]
    ya = _head_rms(of_ref[0] + ob_ref[0], gg_ref[...], dv_gla) * sg_ref[0]
    ya = _dot(ya.astype(BF16), wa_ref[...])
    yb = _head_rms(od_ref[0], gd_ref[...], dv_diff) * diff_scale
    yb = _dot(yb.astype(BF16), wb_ref[...])
    sm = sm_ref[0]
    merged = (sm[:, :d] * ya + sm[:, d:] * yb).astype(BF16)
    gate = mod_ref[0, 0][5:6]
    o_ref[0] = x_ref[0] + gate * _dot(merged, wo_ref[...])


def _merge(x1, modt, o_f, o_b, sgg, o_d, smg, g_gla, g_diff, wa, wb, wo, *, n_lat_tiles, diff_scale):
    b_, _, d = x1.shape
    tm = TOKEN_TILE
    row = lambda width: pl.BlockSpec((1, tm, width), lambda b, j: (b, j, 0))
    kern = functools.partial(_merge_kernel, dv_gla=o_f.shape[2] // GLA_HEADS,
                             dv_diff=o_d.shape[2] // DIFF_HEADS, diff_scale=diff_scale)
    return pl.pallas_call(
        kern,
        out_shape=jax.ShapeDtypeStruct((b_, n_lat_tiles * tm, d), F32),
        grid=(b_, n_lat_tiles),
        in_specs=[
            row(d),
            pl.BlockSpec((1, 1, MOD_ROWS, d), lambda b, j: (b, 0, 0, 0)),
            row(o_f.shape[2]), row(o_b.shape[2]), row(sgg.shape[2]), row(o_d.shape[2]), row(smg.shape[2]),
            _resident(g_gla.shape), _resident(g_diff.shape),
            _resident(wa.shape), _resident(wb.shape), _resident(wo.shape),
        ],
        out_specs=row(d),
        compiler_params=_params(("parallel", "parallel")),
        name="merge",
    )(x1, modt, o_f, o_b, sgg, o_d, smg, g_gla, g_diff, wa, wb, wo)


def _rope_tables(seq, lp):
    half = DIFF_DH // 2
    quarter = half // 2
    inv = ROPE_BASE ** (-np.arange(quarter, dtype=np.float32) / quarter)
    t = jnp.arange(seq, dtype=jnp.int32)
    ang_row = (t // GRID_W).astype(F32)[:, None] * inv[None, :]
    ang_col = (t % GRID_W).astype(F32)[:, None] * inv[None, :]
    ang = jnp.concatenate([ang_row, ang_row, ang_col, ang_col], axis=1)
    sign = np.tile(np.concatenate([-np.ones(quarter), np.ones(quarter)]), 2).astype(np.float32)
    cos = jnp.tile(jnp.cos(ang), (1, LANES // DIFF_DH))
    sin = jnp.tile(jnp.sin(ang) * sign[None, :], (1, LANES // DIFF_DH))
    pad = lp - seq
    cos = jnp.concatenate([cos, jnp.ones((pad, LANES), F32)], axis=0)
    sin = jnp.concatenate([sin, jnp.zeros((pad, LANES), F32)], axis=0)
    tab_t = jnp.concatenate([jnp.cos(ang_row), jnp.sin(ang_row), jnp.cos(ang_col), jnp.sin(ang_col)], axis=1).T
    ident = np.repeat(np.array([1.0, 0.0, 1.0, 0.0], np.float32), quarter)[:, None]
    tab_t = jnp.concatenate([tab_t, jnp.broadcast_to(ident, (DIFF_DH, pad))], axis=1)
    return cos, sin, tab_t


def kernel(x, c, ctx, c_ctx, w_ada, b_ada, ffn1_norm, ffn1_w1, ffn1_w3, ffn1_w2, mix_norm, w_in, gla_gate_w2, gla_gate_b, gla_out_norm, diff_lambda, diff_out_norm, w_branch_gla, w_branch_diff, w_out, ffn2_norm, ffn2_w1, ffn2_w3, ffn2_w2, final_norm):
    b_, seq, d = x.shape
    n_ctx = ctx.shape[1]
    assert w_ada.shape[0] == 1, "single-layer stack"
    tm = TOKEN_TILE
    assert seq % tm == 0 and seq % ATTN_TQ == 0 and n_ctx <= tm
    assert seq % GLA_BLOCK == 0 and n_ctx % GLA_BLOCK == 0 and (seq + n_ctx) % ATTN_TK == 0
    n_lat_tiles = seq // tm
    n_tiles = n_lat_tiles + 1
    lp = n_tiles * tm
    lam_init = 0.8 - 0.6 * math.exp(0.0)

    gla_qk = GLA_HEADS * (d // 8)
    gla_v = GLA_HEADS * (d // 4)
    dif = DIFF_HEADS * 2 * DIFF_DH
    splits = np.cumsum([gla_qk, gla_qk, gla_v, gla_v, 2 * GLA_RANK, dif, dif, dif, 2 * d])
    assert w_in.shape[2] == splits[-1]
    wi = w_in[0].astype(BF16)
    w_gq, w_gk, w_gv, w_gg, w_lr, w_dq, w_dk, w_dv, w_mg = jnp.split(wi, splits[:-1].tolist(), axis=1)
    w_lr = jnp.pad(w_lr, ((0, 0), (0, LANES - 2 * GLA_RANK)))
    w2cat = jnp.zeros((LANES, 2 * gla_qk), F32)
    w2cat = w2cat.at[:GLA_RANK, :gla_qk].set(gla_gate_w2[0, 0])
    w2cat = w2cat.at[GLA_RANK:2 * GLA_RANK, gla_qk:].set(gla_gate_w2[0, 1]).astype(BF16)
    b2cat = gla_gate_b[0].reshape(1, 2 * gla_qk)

    rows = -(-(b_ + 1) // 8) * 8
    c_rows = jnp.concatenate([c, c_ctx[None, :], jnp.zeros((rows - b_ - 1, d), F32)], axis=0)
    mods = _mod_table(c_rows, w_ada[0], b_ada).reshape(rows, N_MOD, d)
    modt = jnp.stack([mods[:b_], jnp.broadcast_to(mods[b_], (b_, N_MOD, d))], axis=1)
    modt = jnp.pad(modt, ((0, 0), (0, 0), (0, MOD_ROWS - N_MOD), (0, 0)))

    ctx_tile = jnp.pad(ctx, ((0, 0), (0, tm - n_ctx), (0, 0)))
    tiles = dict(n_lat_tiles=n_lat_tiles, n_tiles=n_tiles)

    x1 = _ffn(x, modt, ffn1_norm, ffn1_w1[0].astype(BF16), ffn1_w3[0].astype(BF16),
              ffn1_w2[0].astype(BF16), row0=0, ctx_tile=ctx_tile, **tiles)

    gq, gk, gv, sgg, logf = _proj_gla(x1, modt, mix_norm, w_gq, w_gk, w_gv, w_gg, w_lr, w2cat, b2cat, **tiles)
    cos_t, sin_t, tab_t = _rope_tables(seq, lp)
    dq, dk, dv, smg = _proj_dif(x1, modt, mix_norm, cos_t, sin_t, tab_t, w_dq.T, w_dk, w_dv.T, w_mg, **tiles)

    o_f, o_b = _gla(gq, gk, gv, logf, n_lat_blocks=seq // GLA_BLOCK, n_ctx_blocks=n_ctx // GLA_BLOCK)

    lam_p = jnp.pad(diff_lambda[0], ((0, 4), (0, LANES - DIFF_DH)))
    o_d = _attn(dq, dk, dv, lam_p, seq=seq, n_keys=seq + n_ctx, lam_init=lam_init)

    x2 = _merge(x1, modt, o_f, o_b, sgg, o_d, smg, gla_out_norm, diff_out_norm,
                w_branch_gla[0].astype(BF16), w_branch_diff[0].astype(BF16), w_out[0].astype(BF16),
                n_lat_tiles=n_lat_tiles, diff_scale=1.0 - lam_init)

    return _ffn(x2, modt, ffn2_norm, ffn2_w1[0].astype(BF16), ffn2_w3[0].astype(BF16),
                ffn2_w2[0].astype(BF16), row0=6, n_lat_tiles=n_lat_tiles, n_tiles=n_lat_tiles,
                final_gain=final_norm.reshape(1, d))
```

```python
import functools
import math

import jax
import jax.numpy as jnp
import numpy as np
from jax import lax
from jax.experimental import pallas as pl
from jax.experimental.pallas import tpu as pltpu

F32 = jnp.float32
BF16 = jnp.bfloat16

EPS = 1e-6
GRID_W = 64
N_MOD = 9
MOD_ROWS = 16
GLA_HEADS = 4
GLA_RANK = 16
GLA_TAU = 16.0
GLA_CHUNK = 64
GLA_BLOCK = 256
DIFF_HEADS = 8
DIFF_DH = 64
ROPE_BASE = 10000.0
LANES = 128
MXU_TILE = 256
BF16_ROWS = 16
LOG2_E = 1.4426950408889634

TOKEN_TILE = 512
ATTN_TQ = 1024
ATTN_TK = 768
ATTN_QGROUP = 256
VMEM_LIMIT = 52 * 1024 * 1024


def _resident(shape):
    zeros = (0,) * len(shape)
    return pl.BlockSpec(shape, lambda *_: zeros, pipeline_mode=pl.Buffered(1))


def _params(semantics):
    return pltpu.CompilerParams(dimension_semantics=semantics, vmem_limit_bytes=VMEM_LIMIT)


def _dot(a, b):
    return jnp.dot(a, b, preferred_element_type=F32)


def _dot_nt(a, b):
    return lax.dot_general(a, b, (((1,), (1,)), ((), ())), preferred_element_type=F32)


def _dot_tn(a, b):
    return lax.dot_general(a, b, (((0,), (0,)), ((), ())), preferred_element_type=F32)


def _sigmoid(x):
    return 1.0 / (1.0 + jnp.exp(-x))


def _rms(x):
    return x * lax.rsqrt(jnp.mean(x * x, axis=-1, keepdims=True) + EPS)


def _modnorm(x, gain, shift, scale1):
    return (_rms(x) * gain) * scale1 + shift


def _mod_kernel(c_ref, w_ref, b_ref, o_ref):
    k = pl.program_id(0)
    c = c_ref[...]
    s = c * _sigmoid(c)
    acc = jnp.zeros(o_ref.shape, F32)
    w = w_ref[...]
    s_parts = _split3(s)
    w_parts = _split3(w)
    for i in range(3):
        for j in range(3 - i):
            acc = acc + _dot(s_parts[i], w_parts[j])
    val = acc + b_ref[...]
    add = jnp.where(k % 3 == 1, 1.0, 0.0).astype(F32)
    mul = jnp.where((k == 2) | (k == 8), 0.5, 1.0).astype(F32)
    o_ref[...] = (val + add) * mul


def _split3(x):
    hi = x.astype(BF16)
    r1 = x - hi.astype(F32)
    mid = r1.astype(BF16)
    lo = (r1 - mid.astype(F32)).astype(BF16)
    return hi, mid, lo


def _mod_table(c_rows, w_ada, b_ada):
    rows, d = c_rows.shape
    return pl.pallas_call(
        _mod_kernel,
        out_shape=jax.ShapeDtypeStruct((rows, N_MOD * d), F32),
        grid=(N_MOD,),
        in_specs=[
            pl.BlockSpec((rows, d), lambda k: (0, 0)),
            pl.BlockSpec((d, d), lambda k: (0, k)),
            pl.BlockSpec((1, d), lambda k: (0, k)),
        ],
        out_specs=pl.BlockSpec((rows, d), lambda k: (0, k)),
        compiler_params=_params(("arbitrary",)),
        name="mod_table",
    )(c_rows, w_ada, b_ada)


def _ffn_kernel(x_ref, mod_ref, g_ref, w1_ref, w3_ref, w2_ref, *rest, row0, f_chunk, n_lat_tiles, final):
    if final:
        gf_ref, o_ref = rest
        x = x_ref[0]
    else:
        c_ref, o_ref = rest
        x = jnp.where(pl.program_id(1) >= n_lat_tiles, c_ref[0], x_ref[0])
    mod = mod_ref[0, 0]
    shift, scale1, gate = mod[row0:row0 + 1], mod[row0 + 1:row0 + 2], mod[row0 + 2:row0 + 3]
    h = _modnorm(x, g_ref[...], shift, scale1).astype(BF16)
    acc = jnp.zeros(x.shape, F32)
    f_total = w1_ref.shape[1]
    for f0 in range(0, f_total, f_chunk):
        f1 = min(f0 + f_chunk, f_total)
        a = _dot(h, w1_ref[:, f0:f1])
        b = _dot(h, w3_ref[:, f0:f1])
        u = (a * _sigmoid(a) * b).astype(BF16)
        acc = acc + _dot(u, w2_ref[f0:f1, :])
    y = x + gate * acc
    if final:
        y = _rms(y) * gf_ref[...]
    o_ref[0] = y


def _ffn(x, modt, gain, w1, w3, w2, *, row0, n_lat_tiles, n_tiles, ctx_tile=None, final_gain=None):
    b_, _, d = x.shape
    f = w1.shape[1]
    tm = TOKEN_TILE
    final = final_gain is not None
    assert final != (ctx_tile is not None)
    f_chunk = -(-f // (2 * MXU_TILE)) * MXU_TILE
    kern = functools.partial(_ffn_kernel, row0=row0, f_chunk=f_chunk, n_lat_tiles=n_lat_tiles, final=final)
    in_specs = [
        pl.BlockSpec((1, tm, d), lambda b, j: (b, jnp.minimum(j, n_lat_tiles - 1), 0)),
        pl.BlockSpec((1, 1, MOD_ROWS, d), lambda b, j: (b, (j >= n_lat_tiles).astype(jnp.int32), 0, 0)),
        _resident((1, d)),
        _resident((d, f)),
        _resident((d, f)),
        _resident((f, d)),
    ]
    args = [x, modt, gain, w1, w3, w2]
    if final:
        in_specs.append(_resident((1, d)))
        args.append(final_gain)
    else:
        in_specs.append(pl.BlockSpec((1, tm, d), lambda b, j: (b, 0, 0)))
        args.append(ctx_tile)
    return pl.pallas_call(
        kern,
        out_shape=jax.ShapeDtypeStruct((b_, n_tiles * tm, d), F32),
        grid=(b_, n_tiles),
        in_specs=in_specs,
        out_specs=pl.BlockSpec((1, tm, d), lambda b, j: (b, j, 0)),
        compiler_params=_params(("parallel", "parallel")),
        name="ffn_final" if final else "ffn",
    )(*args)


def _proj_gla_kernel(x_ref, mod_ref, g_ref, wq_ref, wk_ref, wv_ref, wg_ref, wlr_ref, w2_ref, b2_ref,
                     q_ref, k_ref, v_ref, sg_ref, lf_ref, *, q_scale):
    x = x_ref[0]
    mod = mod_ref[0, 0]
    h = _modnorm(x, g_ref[...], mod[3:4], mod[4:5]).astype(BF16)
    lr = _dot(h, wlr_ref[...]).astype(BF16)
    z = _dot(lr, w2_ref[...]) + b2_ref[...]
    lf_ref[0] = (jnp.minimum(z, 0.0) - jnp.log(1.0 + jnp.exp(-jnp.abs(z)))) * (1.0 / GLA_TAU)
    gg = _dot(h, wg_ref[...])
    sg_ref[0] = (gg * _sigmoid(gg)).astype(BF16)
    v_ref[0] = _dot(h, wv_ref[...]).astype(BF16)
    q_ref[0] = _dot(h, wq_ref[...]) * q_scale
    k_ref[0] = _dot(h, wk_ref[...])


def _proj_gla(xs, modt, gain, wq, wk, wv, wg, wlr, w2cat, b2cat, *, n_lat_tiles, n_tiles):
    b_, lp, d = xs.shape
    tm = TOKEN_TILE
    qk = wq.shape[1]
    vv = wv.shape[1]
    row = lambda width: pl.BlockSpec((1, tm, width), lambda b, j: (b, j, 0))
    kern = functools.partial(_proj_gla_kernel, q_scale=(qk // GLA_HEADS) ** -0.5)
    return pl.pallas_call(
        kern,
        out_shape=(
            jax.ShapeDtypeStruct((b_, lp, qk), F32),
            jax.ShapeDtypeStruct((b_, lp, qk), F32),
            jax.ShapeDtypeStruct((b_, lp, vv), BF16),
            jax.ShapeDtypeStruct((b_, lp, vv), BF16),
            jax.ShapeDtypeStruct((b_, lp, 2 * qk), F32),
        ),
        grid=(b_, n_tiles),
        in_specs=[
            row(d),
            pl.BlockSpec((1, 1, MOD_ROWS, d), lambda b, j: (b, (j >= n_lat_tiles).astype(jnp.int32), 0, 0)),
            _resident((1, d)),
            _resident(wq.shape), _resident(wk.shape), _resident(wv.shape), _resident(wg.shape),
            _resident(wlr.shape), _resident(w2cat.shape), _resident(b2cat.shape),
        ],
        out_specs=(row(qk), row(qk), row(vv), row(vv), row(2 * qk)),
        compiler_params=_params(("parallel", "parallel")),
        name="proj_gla",
    )(xs, modt, gain, wq, wk, wv, wg, wlr, w2cat, b2cat)


def _rope(u, cos, sin_signed, first_half):
    out = []
    for g in range(u.shape[1] // LANES):
        ug = u[:, g * LANES:(g + 1) * LANES]
        partner = jnp.where(first_half, pltpu.roll(ug, LANES - 16, 1), pltpu.roll(ug, 16, 1))
        out.append(ug * cos + partner * sin_signed)
    return jnp.concatenate(out, axis=1)


def _proj_dif_kernel(x_ref, mod_ref, g_ref, cos_ref, sin_ref, tab_t_ref, wq_t_ref, wk_ref, wv_t_ref, wm_ref,
                     q_t_ref, k_ref, v_t_ref, sm_ref, *, q_scale):
    x = x_ref[0]
    mod = mod_ref[0, 0]
    h = _modnorm(x, g_ref[...], mod[3:4], mod[4:5]).astype(BF16)
    q_t = _dot_nt(wq_t_ref[...], h)
    quarter = DIFF_DH // 4
    tab = tab_t_ref[...]
    for base in range(0, q_t.shape[0], 2 * quarter):
        t0 = base % DIFF_DH
        cos = tab[t0:t0 + quarter]
        sin = tab[t0 + quarter:t0 + 2 * quarter]
        u1 = q_t[base:base + quarter]
        u2 = q_t[base + quarter:base + 2 * quarter]
        q_t_ref[0, base:base + quarter, :] = ((u1 * cos - u2 * sin) * q_scale).astype(BF16)
        q_t_ref[0, base + quarter:base + 2 * quarter, :] = ((u2 * cos + u1 * sin) * q_scale).astype(BF16)
    cos = cos_ref[...]
    sin_signed = sin_ref[...]
    lane = lax.broadcasted_iota(jnp.int32, cos.shape, 1)
    first_half = (lane % 32) < 16
    k_ref[0] = _rope(_dot(h, wk_ref[...]), cos, sin_signed, first_half).astype(BF16)
    v_t_ref[0] = _dot_nt(wv_t_ref[...], h).astype(BF16)
    sm_ref[0] = _sigmoid(_dot(h, wm_ref[...])).astype(BF16)


def _proj_dif(xs, modt, gain, cos_t, sin_t, tab_t, wq_t, wk, wv_t, wm, *, n_lat_tiles, n_tiles):
    b_, lp, d = xs.shape
    tm = TOKEN_TILE
    row = lambda width: pl.BlockSpec((1, tm, width), lambda b, j: (b, j, 0))
    col = lambda height: pl.BlockSpec((1, height, tm), lambda b, j: (b, 0, j))
    tab = pl.BlockSpec((tm, LANES), lambda b, j: (j, 0))
    kern = functools.partial(_proj_dif_kernel, q_scale=DIFF_DH ** -0.5 * LOG2_E)
    return pl.pallas_call(
        kern,
        out_shape=(
            jax.ShapeDtypeStruct((b_, wq_t.shape[0], lp), BF16),
            jax.ShapeDtypeStruct((b_, lp, wk.shape[1]), BF16),
            jax.ShapeDtypeStruct((b_, wv_t.shape[0], lp), BF16),
            jax.ShapeDtypeStruct((b_, lp, wm.shape[1]), BF16),
        ),
        grid=(b_, n_tiles),
        in_specs=[
            row(d),
            pl.BlockSpec((1, 1, MOD_ROWS, d), lambda b, j: (b, (j >= n_lat_tiles).astype(jnp.int32), 0, 0)),
            _resident((1, d)),
            tab, tab,
            pl.BlockSpec((DIFF_DH, tm), lambda b, j: (0, j)),
            _resident(wq_t.shape), _resident(wk.shape), _resident(wv_t.shape), _resident(wm.shape),
        ],
        out_specs=(col(wq_t.shape[0]), row(wk.shape[1]), col(wv_t.shape[0]), row(wm.shape[1])),
        compiler_params=_params(("parallel", "parallel")),
        name="proj_dif",
    )(xs, modt, gain, cos_t, sin_t, tab_t, wq_t, wk, wv_t, wm)


def _gla_decays(q, k, logf, tri, ref_row, end_row):
    c = GLA_CHUNK
    parts = _split3(logf)
    cum = _dot(tri, parts[0]) + _dot(tri, parts[1]) + _dot(tri, parts[2])
    n_chunks = q.shape[0] // c
    totals = [cum[a * c + end_row:a * c + end_row + 1] for a in range(n_chunks)]
    per_chunk = lambda rows: jnp.concatenate([jnp.broadcast_to(r, (c, r.shape[1])) for r in rows], axis=0)
    ref = per_chunk([cum[a * c + ref_row:a * c + ref_row + 1] for a in range(n_chunks)])
    total = per_chunk(totals)
    q_state = (q * jnp.exp(cum)).astype(BF16)
    q_in = (q * jnp.exp(cum - ref)).astype(BF16)
    k_in = (k * jnp.exp(ref - cum)).astype(BF16)
    k_out = (k * jnp.exp(total - cum)).astype(BF16)
    return q_state, q_in, k_in, k_out, [jnp.exp(t) for t in totals]


def _gla_kernel(qf_ref, kf_ref, vf_ref, lf_ref, qb_ref, kb_ref, vb_ref, lb_ref,
                of_ref, ob_ref, sf_ref, sb_ref):
    @pl.when(pl.program_id(1) == 0)
    def _():
        sf_ref[...] = jnp.zeros(sf_ref.shape, F32)
        sb_ref[...] = jnp.zeros(sb_ref.shape, F32)

    c = GLA_CHUNK
    rows = qf_ref.shape[1]
    n_chunks = rows // c
    row = lax.broadcasted_iota(jnp.int32, (rows, rows), 0)
    col = lax.broadcasted_iota(jnp.int32, (rows, rows), 1)
    same_chunk = (row // c) == (col // c)
    lower = same_chunk & (col <= row)
    upper = same_chunk & (col >= row)
    dirs = [
        (_gla_decays(qf_ref[0], kf_ref[0], lf_ref[0], jnp.where(lower, 1.0, 0.0).astype(BF16),
                     c // 2 - 1, c - 1), vf_ref[0], lower, sf_ref, of_ref, list(range(n_chunks))),
        (_gla_decays(qb_ref[0], kb_ref[0], lb_ref[0], jnp.where(upper, 1.0, 0.0).astype(BF16),
                     c // 2, 0), vb_ref[0], upper, sb_ref, ob_ref, list(range(n_chunks))[::-1]),
    ]
    dk = qf_ref.shape[2] // GLA_HEADS
    dv = vf_ref.shape[2] // GLA_HEADS
    chains = []
    for (q_state, q_in, k_in, k_out, decays), v, keep, s_ref, o_ref, order in dirs:
        for hd in range(GLA_HEADS):
            ks = slice(hd * dk, (hd + 1) * dk)
            vs = slice(hd * dv, (hd + 1) * dv)
            chains.append(dict(q_state=q_state[:, ks], q_in=q_in[:, ks], k_in=k_in[:, ks], k_out=k_out[:, ks],
                               decays=[d[:, ks] for d in decays], v=v[:, vs], keep=keep, s_ref=s_ref,
                               o_ref=o_ref, hd=hd, vs=vs, order=order))
    chunk = lambda x, a: x[a * c:(a + 1) * c]
    for ch in chains:
        ch["attn"] = _dot_nt(ch["q_in"], ch["k_in"])
    for ch in chains:
        ch["update"] = [_dot_tn(chunk(ch["v"], a), chunk(ch["k_out"], a)) for a in range(n_chunks)]
    for ch in chains:
        state_t = ch["s_ref"][ch["hd"]]
        ch["before"] = {}
        for a in ch["order"]:
            ch["before"][a] = state_t
            state_t = state_t * ch["decays"][a] + ch["update"][a]
        ch["s_ref"][ch["hd"]] = state_t
    for ch in chains:
        ch["inter"] = jnp.concatenate(
            [_dot_nt(chunk(ch["q_state"], a), ch["before"][a].astype(BF16)) for a in range(n_chunks)], axis=0)
    for ch in chains:
        attn = jnp.where(ch["keep"], ch["attn"], 0.0).astype(BF16)
        ch["o_ref"][0, :, ch["vs"]] = (ch["inter"] + _dot(attn, ch["v"])).astype(BF16)


def _gla(gq, gk, gv, logf, *, n_lat_blocks, n_ctx_blocks):
    b_, lp, qk = gq.shape
    vv = gv.shape[2]
    c = GLA_BLOCK
    n = n_lat_blocks + n_ctx_blocks
    fwd = lambda b, s: (b, (s + n_lat_blocks) % n, 0)
    bwd = lambda b, s: (b, n - 1 - s, 0)
    lf_spec = pl.BlockSpec((1, c, qk), lambda b, s: (b, (s + n_lat_blocks) % n, 0))
    lb_spec = pl.BlockSpec((1, c, qk), lambda b, s: (b, n - 1 - s, 1))
    return pl.pallas_call(
        _gla_kernel,
        out_shape=(jax.ShapeDtypeStruct((b_, lp, vv), BF16), jax.ShapeDtypeStruct((b_, lp, vv), BF16)),
        grid=(b_, n),
        in_specs=[
            pl.BlockSpec((1, c, qk), fwd), pl.BlockSpec((1, c, qk), fwd), pl.BlockSpec((1, c, vv), fwd), lf_spec,
            pl.BlockSpec((1, c, qk), bwd), pl.BlockSpec((1, c, qk), bwd), pl.BlockSpec((1, c, vv), bwd), lb_spec,
        ],
        out_specs=(pl.BlockSpec((1, c, vv), fwd), pl.BlockSpec((1, c, vv), bwd)),
        scratch_shapes=[
            pltpu.VMEM((GLA_HEADS, vv // GLA_HEADS, qk // GLA_HEADS), F32),
            pltpu.VMEM((GLA_HEADS, vv // GLA_HEADS, qk // GLA_HEADS), F32),
        ],
        compiler_params=_params(("parallel", "arbitrary")),
        name="gla_scan",
    )(gq, gk, gv, logf, gq, gk, gv, logf)


def _attn_kernel(q_t_ref, k_ref, v_t_ref, lam_ref, o_ref, m_ref, acc_ref, s_ref, p_ref, a_ref, bm_ref,
                 *, tk, n_kblocks, lam_init):
    tq = q_t_ref.shape[2]
    q_t = q_t_ref[0]
    row = lax.broadcasted_iota(jnp.int32, q_t.shape, 0)
    zero = jnp.zeros_like(q_t)
    qz_t = jnp.concatenate([jnp.where(row < DIFF_DH, q_t, zero), jnp.where(row >= DIFF_DH, q_t, zero)], axis=1)
    m_ref[...] = jnp.full(m_ref.shape, -jnp.inf, F32)
    acc_ref[...] = jnp.zeros(acc_ref.shape, F32)

    groups = list(range(2 * tq // ATTN_QGROUP))
    lanes = lambda g: slice(g * ATTN_QGROUP, (g + 1) * ATTN_QGROUP)

    def scores(kb, slot, g):
        s_t = _dot(kb, qz_t[:, lanes(g)])
        s_ref[slot, g] = s_t
        bm_ref[slot, g] = jnp.max(s_t, axis=0, keepdims=True)

    def softmax(slot, g):
        s_t = s_ref[slot, g]
        m_prev = m_ref[g]
        m_new = jnp.maximum(m_prev, bm_ref[slot, g])
        m_ref[g] = m_new
        a_ref[slot, g] = jnp.exp2(m_prev - m_new)
        p_ref[slot, g] = jnp.exp2(s_t - m_new).astype(BF16)

    ones_rows = jnp.ones((BF16_ROWS, tk), BF16)

    def weighted_values(vb_t, slot, g):
        acc_ref[g] = a_ref[slot, g] * acc_ref[g] + _dot(vb_t, p_ref[slot, g])

    def load_keys(j):
        return k_ref[0, pl.ds(pl.multiple_of(j * tk, tk), tk), :]

    def load_values(j):
        vb_t = v_t_ref[0, :, pl.ds(pl.multiple_of(j * tk, tk), tk)]
        return jnp.concatenate([vb_t, ones_rows], axis=0)

    def stage(j, slot, first=False, last=False):
        kb = None if last else load_keys(j + 1)
        vb_t = None if first else load_values(j - 1)
        for g in groups:
            if not last:
                scores(kb, 1 - slot, g)
            if not first:
                weighted_values(vb_t, 1 - slot, g)
            softmax(slot, g)

    kb0 = load_keys(0)
    for g in groups:
        scores(kb0, 0, g)
    stage(0, 0, first=True, last=n_kblocks == 1)
    n_pairs = max(n_kblocks - 2, 0) // 2

    def pair(i, carry):
        stage(2 * i + 1, 1)
        stage(2 * i + 2, 0)
        return carry

    lax.fori_loop(0, n_pairs, pair, 0)
    for j in range(2 * n_pairs + 1, n_kblocks):
        stage(j, j % 2, last=j == n_kblocks - 1)
    vb_last = load_values(n_kblocks - 1)
    for g in groups:
        weighted_values(vb_last, (n_kblocks - 1) % 2, g)

    lp = lam_ref[...]
    lam = (jnp.exp(jnp.sum(lp[0:1] * lp[1:2], axis=-1, keepdims=True))
           - jnp.exp(jnp.sum(lp[2:3] * lp[3:4], axis=-1, keepdims=True)) + lam_init)
    dv = v_t_ref.shape[1]
    acc = jnp.concatenate([acc_ref[g] for g in groups], axis=1)
    o_t = acc[:dv] / acc[dv:dv + 1]
    o_ref[0] = (o_t[:, :tq] - lam * o_t[:, tq:]).T.astype(BF16)


def _attn(dq_t, dk, dv_t, lam_p, *, seq, n_keys, lam_init):
    b_, lp, width = dk.shape
    tq, tk = ATTN_TQ, ATTN_TK
    heads = width // LANES
    n_groups = 2 * tq // ATTN_QGROUP
    kern = functools.partial(_attn_kernel, tk=tk, n_kblocks=n_keys // tk, lam_init=lam_init)
    return pl.pallas_call(
        kern,
        out_shape=jax.ShapeDtypeStruct((b_, seq, width), BF16),
        grid=(b_, heads, seq // tq),
        in_specs=[
            pl.BlockSpec((1, LANES, tq), lambda b, h, i: (b, h, i)),
            pl.BlockSpec((1, lp, LANES), lambda b, h, i: (b, 0, h)),
            pl.BlockSpec((1, LANES, lp), lambda b, h, i: (b, h, 0)),
            pl.BlockSpec(lam_p.shape, lambda b, h, i: (0, 0)),
        ],
        out_specs=pl.BlockSpec((1, tq, LANES), lambda b, h, i: (b, i, h)),
        scratch_shapes=[
            pltpu.VMEM((n_groups, 1, ATTN_QGROUP), F32),
            pltpu.VMEM((n_groups, LANES + BF16_ROWS, ATTN_QGROUP), F32),
            pltpu.VMEM((2, n_groups, tk, ATTN_QGROUP), F32),
            pltpu.VMEM((2, n_groups, tk, ATTN_QGROUP), BF16),
            pltpu.VMEM((2, n_groups, 1, ATTN_QGROUP), F32),
            pltpu.VMEM((2, n_groups, 1, ATTN_QGROUP), F32),
        ],
        compiler_params=_params(("parallel", "parallel", "parallel")),
        name="diff_attn",
    )(dq_t, dk, dv_t, lam_p)


def _head_rms(x, gain, width):
    parts = []
    for h0 in range(0, x.shape[1], width):
        parts.append(_rms(x[:, h0:h0 + width]))
    return jnp.concatenate(parts, axis=1) * gain


def _merge_kernel(x_ref, mod_ref, of_ref, ob_ref, sg_ref, od_ref, sm_ref, gg_ref, gd_ref,
                  wa_ref, wb_ref, wo_ref, o_ref, *, dv_gla, dv_diff, diff_scale):
    d = x_ref.shape[2]
    o_gla = of_ref[0].astype(F32) + ob_ref[0].astype(F32)
    ya = _head_rms(o_gla, gg_ref[...], dv_gla) * sg_ref[0].astype(F32)
    ya = _dot(ya.astype(BF16), wa_ref[...])
    yb = _head_rms(od_ref[0].astype(F32), gd_ref[...], dv_diff) * diff_scale
    yb = _dot(yb.astype(BF16), wb_ref[...])
    sm = sm_ref[0].astype(F32)
    merged = (sm[:, :d] * ya + sm[:, d:] * yb).astype(BF16)
    gate = mod_ref[0, 0][5:6]
    o_ref[0] = x_ref[0] + gate * _dot(merged, wo_ref[...])


def _merge(x1, modt, o_f, o_b, sgg, o_d, smg, g_gla, g_diff, wa, wb, wo, *, n_lat_tiles, diff_scale):
    b_, _, d = x1.shape
    tm = TOKEN_TILE
    row = lambda width: pl.BlockSpec((1, tm, width), lambda b, j: (b, j, 0))
    kern = functools.partial(_merge_kernel, dv_gla=o_f.shape[2] // GLA_HEADS,
                             dv_diff=o_d.shape[2] // DIFF_HEADS, diff_scale=diff_scale)
    return pl.pallas_call(
        kern,
        out_shape=jax.ShapeDtypeStruct((b_, n_lat_tiles * tm, d), F32),
        grid=(b_, n_lat_tiles),
        in_specs=[
            row(d),
            pl.BlockSpec((1, 1, MOD_ROWS, d), lambda b, j: (b, 0, 0, 0)),
            row(o_f.shape[2]), row(o_b.shape[2]), row(sgg.shape[2]), row(o_d.shape[2]), row(smg.shape[2]),
            _resident(g_gla.shape), _resident(g_diff.shape),
            _resident(wa.shape), _resident(wb.shape), _resident(wo.shape),
        ],
        out_specs=row(d),
        compiler_params=_params(("parallel", "parallel")),
        name="merge",
    )(x1, modt, o_f, o_b, sgg, o_d, smg, g_gla, g_diff, wa, wb, wo)


def _rope_tables(seq, lp):
    half = DIFF_DH // 2
    quarter = half // 2
    inv = ROPE_BASE ** (-np.arange(quarter, dtype=np.float32) / quarter)
    t = jnp.arange(seq, dtype=jnp.int32)
    ang_row = (t // GRID_W).astype(F32)[:, None] * inv[None, :]
    ang_col = (t % GRID_W).astype(F32)[:, None] * inv[None, :]
    ang = jnp.concatenate([ang_row, ang_row, ang_col, ang_col], axis=1)
    sign = np.tile(np.concatenate([-np.ones(quarter), np.ones(quarter)]), 2).astype(np.float32)
    cos = jnp.tile(jnp.cos(ang), (1, LANES // DIFF_DH))
    sin = jnp.tile(jnp.sin(ang) * sign[None, :], (1, LANES // DIFF_DH))
    pad = lp - seq
    cos = jnp.concatenate([cos, jnp.ones((pad, LANES), F32)], axis=0)
    sin = jnp.concatenate([sin, jnp.zeros((pad, LANES), F32)], axis=0)
    tab_t = jnp.concatenate([jnp.cos(ang_row), jnp.sin(ang_row), jnp.cos(ang_col), jnp.sin(ang_col)], axis=1).T
    ident = np.repeat(np.array([1.0, 0.0, 1.0, 0.0], np.float32), quarter)[:, None]
    tab_t = jnp.concatenate([tab_t, jnp.broadcast_to(ident, (DIFF_DH, pad))], axis=1)
    return cos, sin, tab_t


def kernel(x, c, ctx, c_ctx, w_ada, b_ada, ffn1_norm, ffn1_w1, ffn1_w3, ffn1_w2, mix_norm, w_in, gla_gate_w2, gla_gate_b, gla_out_norm, diff_lambda, diff_out_norm, w_branch_gla, w_branch_diff, w_out, ffn2_norm, ffn2_w1, ffn2_w3, ffn2_w2, final_norm):
    b_, seq, d = x.shape
    n_ctx = ctx.shape[1]
    assert w_ada.shape[0] == 1, "single-layer stack"
    tm = TOKEN_TILE
    assert seq % tm == 0 and seq % ATTN_TQ == 0 and n_ctx <= tm
    assert seq % GLA_BLOCK == 0 and n_ctx % GLA_BLOCK == 0 and (seq + n_ctx) % ATTN_TK == 0
    n_lat_tiles = seq // tm
    n_tiles = n_lat_tiles + 1
    lp = n_tiles * tm
    lam_init = 0.8 - 0.6 * math.exp(0.0)

    gla_qk = GLA_HEADS * (d // 8)
    gla_v = GLA_HEADS * (d // 4)
    dif = DIFF_HEADS * 2 * DIFF_DH
    splits = np.cumsum([gla_qk, gla_qk, gla_v, gla_v, 2 * GLA_RANK, dif, dif, dif, 2 * d])
    assert w_in.shape[2] == splits[-1]
    wi = w_in[0].astype(BF16)
    w_gq, w_gk, w_gv, w_gg, w_lr, w_dq, w_dk, w_dv, w_mg = jnp.split(wi, splits[:-1].tolist(), axis=1)
    w_lr = jnp.pad(w_lr, ((0, 0), (0, LANES - 2 * GLA_RANK)))
    w2cat = jnp.zeros((LANES, 2 * gla_qk), F32)
    w2cat = w2cat.at[:GLA_RANK, :gla_qk].set(gla_gate_w2[0, 0])
    w2cat = w2cat.at[GLA_RANK:2 * GLA_RANK, gla_qk:].set(gla_gate_w2[0, 1]).astype(BF16)
    b2cat = gla_gate_b[0].reshape(1, 2 * gla_qk)

    rows = -(-(b_ + 1) // 8) * 8
    c_rows = jnp.concatenate([c, c_ctx[None, :], jnp.zeros((rows - b_ - 1, d), F32)], axis=0)
    mods = _mod_table(c_rows, w_ada[0], b_ada).reshape(rows, N_MOD, d)
    modt = jnp.stack([mods[:b_], jnp.broadcast_to(mods[b_], (b_, N_MOD, d))], axis=1)
    modt = jnp.pad(modt, ((0, 0), (0, 0), (0, MOD_ROWS - N_MOD), (0, 0)))

    ctx_tile = jnp.pad(ctx, ((0, 0), (0, tm - n_ctx), (0, 0)))
    tiles = dict(n_lat_tiles=n_lat_tiles, n_tiles=n_tiles)

    x1 = _ffn(x, modt, ffn1_norm, ffn1_w1[0].astype(BF16), ffn1_w3[0].astype(BF16),
              ffn1_w2[0].astype(BF16), row0=0, ctx_tile=ctx_tile, **tiles)

    gq, gk, gv, sgg, logf = _proj_gla(x1, modt, mix_norm, w_gq, w_gk, w_gv, w_gg, w_lr, w2cat, b2cat, **tiles)
    cos_t, sin_t, tab_t = _rope_tables(seq, lp)
    dq, dk, dv, smg = _proj_dif(x1, modt, mix_norm, cos_t, sin_t, tab_t, w_dq.T, w_dk, w_dv.T, w_mg, **tiles)

    o_f, o_b = _gla(gq, gk, gv, logf, n_lat_blocks=seq // GLA_BLOCK, n_ctx_blocks=n_ctx // GLA_BLOCK)

    lam_p = jnp.pad(diff_lambda[0], ((0, 4), (0, LANES - DIFF_DH)))
    o_d = _attn(dq, dk, dv, lam_p, seq=seq, n_keys=seq + n_ctx, lam_init=lam_init)

    x2 = _merge(x1, modt, o_f, o_b, sgg, o_d, smg, gla_out_norm, diff_out_norm,
                w_branch_gla[0].astype(BF16), w_branch_diff[0].astype(BF16), w_out[0].astype(BF16),
                n_lat_tiles=n_lat_tiles, diff_scale=1.0 - lam_init)

    return _ffn(x2, modt, ffn2_norm, ffn2_w1[0].astype(BF16), ffn2_w3[0].astype(BF16),
                ffn2_w2[0].astype(BF16), row0=6, n_lat_tiles=n_lat_tiles, n_tiles=n_lat_tiles,
                final_gain=final_norm.reshape(1, d))
```

```python
import functools
import math

import jax
import jax.numpy as jnp
import numpy as np
from jax import lax
from jax.experimental import pallas as pl
from jax.experimental.pallas import tpu as pltpu

F32 = jnp.float32
BF16 = jnp.bfloat16

EPS = 1e-6
GRID_W = 64
N_MOD = 9
MOD_ROWS = 16
GLA_HEADS = 4
GLA_RANK = 16
GLA_TAU = 16.0
GLA_CHUNK = 64
GLA_BLOCK = 256
DIFF_HEADS = 8
DIFF_DH = 64
ROPE_BASE = 10000.0
LANES = 128
MXU_TILE = 256
BF16_ROWS = 16
LOG2_E = 1.4426950408889634

TOKEN_TILE = 512
ATTN_TQ = 1024
ATTN_TK = 768
ATTN_QGROUP = 256
VMEM_LIMIT = 52 * 1024 * 1024


def _resident(shape):
    zeros = (0,) * len(shape)
    return pl.BlockSpec(shape, lambda *_: zeros, pipeline_mode=pl.Buffered(1))


def _params(semantics):
    return pltpu.CompilerParams(dimension_semantics=semantics, vmem_limit_bytes=VMEM_LIMIT)


def _dot(a, b):
    return jnp.dot(a, b, preferred_element_type=F32)


def _dot_nt(a, b):
    return lax.dot_general(a, b, (((1,), (1,)), ((), ())), preferred_element_type=F32)


def _dot_tn(a, b):
    return lax.dot_general(a, b, (((0,), (0,)), ((), ())), preferred_element_type=F32)


def _sigmoid(x):
    return 1.0 / (1.0 + jnp.exp(-x))


def _rms(x):
    return x * lax.rsqrt(jnp.mean(x * x, axis=-1, keepdims=True) + EPS)


def _modnorm(x, gain, shift, scale1):
    return (_rms(x) * gain) * scale1 + shift


def _mod_kernel(c_ref, w_ref, b_ref, o_ref):
    k = pl.program_id(0)
    c = c_ref[...]
    s = c * _sigmoid(c)
    acc = jnp.zeros(o_ref.shape, F32)
    w = w_ref[...]
    s_parts = _split3(s)
    w_parts = _split3(w)
    for i in range(3):
        for j in range(3 - i):
            acc = acc + _dot(s_parts[i], w_parts[j])
    val = acc + b_ref[...]
    add = jnp.where(k % 3 == 1, 1.0, 0.0).astype(F32)
    mul = jnp.where((k == 2) | (k == 8), 0.5, 1.0).astype(F32)
    o_ref[...] = (val + add) * mul


def _split3(x):
    hi = x.astype(BF16)
    r1 = x - hi.astype(F32)
    mid = r1.astype(BF16)
    lo = (r1 - mid.astype(F32)).astype(BF16)
    return hi, mid, lo


def _mod_table(c_rows, w_ada, b_ada):
    rows, d = c_rows.shape
    return pl.pallas_call(
        _mod_kernel,
        out_shape=jax.ShapeDtypeStruct((rows, N_MOD * d), F32),
        grid=(N_MOD,),
        in_specs=[
            pl.BlockSpec((rows, d), lambda k: (0, 0)),
            pl.BlockSpec((d, d), lambda k: (0, k)),
            pl.BlockSpec((1, d), lambda k: (0, k)),
        ],
        out_specs=pl.BlockSpec((rows, d), lambda k: (0, k)),
        compiler_params=_params(("arbitrary",)),
        name="mod_table",
    )(c_rows, w_ada, b_ada)


def _ffn_kernel(x_ref, mod_ref, g_ref, w1_ref, w3_ref, w2_ref, *rest, row0, f_chunk, n_lat_tiles, final):
    if final:
        gf_ref, o_ref = rest
        x = x_ref[0]
    else:
        c_ref, o_ref = rest
        x = jnp.where(pl.program_id(1) >= n_lat_tiles, c_ref[0], x_ref[0])
    mod = mod_ref[0, 0]
    shift, scale1, gate = mod[row0:row0 + 1], mod[row0 + 1:row0 + 2], mod[row0 + 2:row0 + 3]
    h = _modnorm(x, g_ref[...], shift, scale1).astype(BF16)
    acc = jnp.zeros(x.shape, F32)
    f_total = w1_ref.shape[1]
    for f0 in range(0, f_total, f_chunk):
        f1 = min(f0 + f_chunk, f_total)
        a = _dot(h, w1_ref[:, f0:f1])
        b = _dot(h, w3_ref[:, f0:f1])
        u = (a * _sigmoid(a) * b).astype(BF16)
        acc = acc + _dot(u, w2_ref[f0:f1, :])
    y = x + gate * acc
    if final:
        y = _rms(y) * gf_ref[...]
    o_ref[0] = y


def _ffn(x, modt, gain, w1, w3, w2, *, row0, n_lat_tiles, n_tiles, ctx_tile=None, final_gain=None):
    b_, _, d = x.shape
    f = w1.shape[1]
    tm = TOKEN_TILE
    final = final_gain is not None
    assert final != (ctx_tile is not None)
    f_chunk = -(-f // (2 * MXU_TILE)) * MXU_TILE
    kern = functools.partial(_ffn_kernel, row0=row0, f_chunk=f_chunk, n_lat_tiles=n_lat_tiles, final=final)
    in_specs = [
        pl.BlockSpec((1, tm, d), lambda b, j: (b, jnp.minimum(j, n_lat_tiles - 1), 0)),
        pl.BlockSpec((1, 1, MOD_ROWS, d), lambda b, j: (b, (j >= n_lat_tiles).astype(jnp.int32), 0, 0)),
        _resident((1, d)),
        _resident((d, f)),
        _resident((d, f)),
        _resident((f, d)),
    ]
    args = [x, modt, gain, w1, w3, w2]
    if final:
        in_specs.append(_resident((1, d)))
        args.append(final_gain)
    else:
        in_specs.append(pl.BlockSpec((1, tm, d), lambda b, j: (b, 0, 0)))
        args.append(ctx_tile)
    return pl.pallas_call(
        kern,
        out_shape=jax.ShapeDtypeStruct((b_, n_tiles * tm, d), F32),
        grid=(b_, n_tiles),
        in_specs=in_specs,
        out_specs=pl.BlockSpec((1, tm, d), lambda b, j: (b, j, 0)),
        compiler_params=_params(("parallel", "parallel")),
        name="ffn_final" if final else "ffn",
    )(*args)


def _proj_gla_kernel(x_ref, mod_ref, g_ref, wq_ref, wk_ref, wv_ref, wg_ref, wlr_ref, w2_ref, b2_ref,
                     q_ref, k_ref, v_ref, sg_ref, lf_ref, *, q_scale):
    x = x_ref[0]
    mod = mod_ref[0, 0]
    h = _modnorm(x, g_ref[...], mod[3:4], mod[4:5]).astype(BF16)
    lr = _dot(h, wlr_ref[...]).astype(BF16)
    z = _dot(lr, w2_ref[...]) + b2_ref[...]
    lf_ref[0] = (jnp.minimum(z, 0.0) - jnp.log(1.0 + jnp.exp(-jnp.abs(z)))) * (1.0 / GLA_TAU)
    gg = _dot(h, wg_ref[...])
    sg_ref[0] = (gg * _sigmoid(gg)).astype(BF16)
    v_ref[0] = _dot(h, wv_ref[...]).astype(BF16)
    q_ref[0] = _dot(h, wq_ref[...]) * q_scale
    k_ref[0] = _dot(h, wk_ref[...])


def _proj_gla(xs, modt, gain, wq, wk, wv, wg, wlr, w2cat, b2cat, *, n_lat_tiles, n_tiles):
    b_, lp, d = xs.shape
    tm = TOKEN_TILE
    qk = wq.shape[1]
    vv = wv.shape[1]
    row = lambda width: pl.BlockSpec((1, tm, width), lambda b, j: (b, j, 0))
    kern = functools.partial(_proj_gla_kernel, q_scale=(qk // GLA_HEADS) ** -0.5)
    return pl.pallas_call(
        kern,
        out_shape=(
            jax.ShapeDtypeStruct((b_, lp, qk), F32),
            jax.ShapeDtypeStruct((b_, lp, qk), F32),
            jax.ShapeDtypeStruct((b_, lp, vv), BF16),
            jax.ShapeDtypeStruct((b_, lp, vv), BF16),
            jax.ShapeDtypeStruct((b_, lp, 2 * qk), F32),
        ),
        grid=(b_, n_tiles),
        in_specs=[
            row(d),
            pl.BlockSpec((1, 1, MOD_ROWS, d), lambda b, j: (b, (j >= n_lat_tiles).astype(jnp.int32), 0, 0)),
            _resident((1, d)),
            _resident(wq.shape), _resident(wk.shape), _resident(wv.shape), _resident(wg.shape),
            _resident(wlr.shape), _resident(w2cat.shape), _resident(b2cat.shape),
        ],
        out_specs=(row(qk), row(qk), row(vv), row(vv), row(2 * qk)),
        compiler_params=_params(("parallel", "parallel")),
        name="proj_gla",
    )(xs, modt, gain, wq, wk, wv, wg, wlr, w2cat, b2cat)


def _rope(u, cos, sin_signed, first_half):
    out = []
    for g in range(u.shape[1] // LANES):
        ug = u[:, g * LANES:(g + 1) * LANES]
        partner = jnp.where(first_half, pltpu.roll(ug, LANES - 16, 1), pltpu.roll(ug, 16, 1))
        out.append(ug * cos + partner * sin_signed)
    return jnp.concatenate(out, axis=1)


def _proj_dif_kernel(x_ref, mod_ref, g_ref, cos_ref, sin_ref, tab_t_ref, wq_t_ref, wk_ref, wv_t_ref, wm_ref,
                     q_t_ref, k_ref, v_t_ref, sm_ref, *, q_scale):
    x = x_ref[0]
    mod = mod_ref[0, 0]
    h = _modnorm(x, g_ref[...], mod[3:4], mod[4:5]).astype(BF16)
    q_t = _dot_nt(wq_t_ref[...], h)
    quarter = DIFF_DH // 4
    tab = tab_t_ref[...]
    for base in range(0, q_t.shape[0], 2 * quarter):
        t0 = base % DIFF_DH
        cos = tab[t0:t0 + quarter]
        sin = tab[t0 + quarter:t0 + 2 * quarter]
        u1 = q_t[base:base + quarter]
        u2 = q_t[base + quarter:base + 2 * quarter]
        q_t_ref[0, base:base + quarter, :] = ((u1 * cos - u2 * sin) * q_scale).astype(BF16)
        q_t_ref[0, base + quarter:base + 2 * quarter, :] = ((u2 * cos + u1 * sin) * q_scale).astype(BF16)
    cos = cos_ref[...]
    sin_signed = sin_ref[...]
    lane = lax.broadcasted_iota(jnp.int32, cos.shape, 1)
    first_half = (lane % 32) < 16
    k_ref[0] = _rope(_dot(h, wk_ref[...]), cos, sin_signed, first_half).astype(BF16)
    v_t_ref[0] = _dot_nt(wv_t_ref[...], h).astype(BF16)
    sm_ref[0] = _sigmoid(_dot(h, wm_ref[...])).astype(BF16)


def _proj_dif(xs, modt, gain, cos_t, sin_t, tab_t, wq_t, wk, wv_t, wm, *, n_lat_tiles, n_tiles):
    b_, lp, d = xs.shape
    tm = TOKEN_TILE
    row = lambda width: pl.BlockSpec((1, tm, width), lambda b, j: (b, j, 0))
    col = lambda height: pl.BlockSpec((1, height, tm), lambda b, j: (b, 0, j))
    tab = pl.BlockSpec((tm, LANES), lambda b, j: (j, 0))
    kern = functools.partial(_proj_dif_kernel, q_scale=DIFF_DH ** -0.5 * LOG2_E)
    return pl.pallas_call(
        kern,
        out_shape=(
            jax.ShapeDtypeStruct((b_, wq_t.shape[0], lp), BF16),
            jax.ShapeDtypeStruct((b_, lp, wk.shape[1]), BF16),
            jax.ShapeDtypeStruct((b_, wv_t.shape[0], lp), BF16),
            jax.ShapeDtypeStruct((b_, lp, wm.shape[1]), BF16),
        ),
        grid=(b_, n_tiles),
        in_specs=[
            row(d),
            pl.BlockSpec((1, 1, MOD_ROWS, d), lambda b, j: (b, (j >= n_lat_tiles).astype(jnp.int32), 0, 0)),
            _resident((1, d)),
            tab, tab,
            pl.BlockSpec((DIFF_DH, tm), lambda b, j: (0, j)),
            _resident(wq_t.shape), _resident(wk.shape), _resident(wv_t.shape), _resident(wm.shape),
        ],
        out_specs=(col(wq_t.shape[0]), row(wk.shape[1]), col(wv_t.shape[0]), row(wm.shape[1])),
        compiler_params=_params(("parallel", "parallel")),
        name="proj_dif",
    )(xs, modt, gain, cos_t, sin_t, tab_t, wq_t, wk, wv_t, wm)


def _gla_decays(q, k, logf, tri, ref_row, end_row):
    c = GLA_CHUNK
    parts = _split3(logf)
    cum = _dot(tri, parts[0]) + _dot(tri, parts[1]) + _dot(tri, parts[2])
    n_chunks = q.shape[0] // c
    totals = [cum[a * c + end_row:a * c + end_row + 1] for a in range(n_chunks)]
    per_chunk = lambda rows: jnp.concatenate([jnp.broadcast_to(r, (c, r.shape[1])) for r in rows], axis=0)
    ref = per_chunk([cum[a * c + ref_row:a * c + ref_row + 1] for a in range(n_chunks)])
    total = per_chunk(totals)
    q_state = (q * jnp.exp(cum)).astype(BF16)
    q_in = (q * jnp.exp(cum - ref)).astype(BF16)
    k_in = (k * jnp.exp(ref - cum)).astype(BF16)
    k_out = (k * jnp.exp(total - cum)).astype(BF16)
    return q_state, q_in, k_in, k_out, [jnp.exp(t) for t in totals]


def _gla_kernel(qf_ref, kf_ref, vf_ref, lf_ref, qb_ref, kb_ref, vb_ref, lb_ref,
                of_ref, ob_ref, sf_ref, sb_ref):
    @pl.when(pl.program_id(1) == 0)
    def _():
        sf_ref[...] = jnp.zeros(sf_ref.shape, F32)
        sb_ref[...] = jnp.zeros(sb_ref.shape, F32)

    c = GLA_CHUNK
    rows = qf_ref.shape[1]
    n_chunks = rows // c
    row = lax.broadcasted_iota(jnp.int32, (rows, rows), 0)
    col = lax.broadcasted_iota(jnp.int32, (rows, rows), 1)
    same_chunk = (row // c) == (col // c)
    lower = same_chunk & (col <= row)
    upper = same_chunk & (col >= row)
    dirs = [
        (_gla_decays(qf_ref[0], kf_ref[0], lf_ref[0], jnp.where(lower, 1.0, 0.0).astype(BF16),
                     c // 2 - 1, c - 1), vf_ref[0], lower, sf_ref, of_ref, list(range(n_chunks))),
        (_gla_decays(qb_ref[0], kb_ref[0], lb_ref[0], jnp.where(upper, 1.0, 0.0).astype(BF16),
                     c // 2, 0), vb_ref[0], upper, sb_ref, ob_ref, list(range(n_chunks))[::-1]),
    ]
    dk = qf_ref.shape[2] // GLA_HEADS
    dv = vf_ref.shape[2] // GLA_HEADS
    chains = []
    for (q_state, q_in, k_in, k_out, decays), v, keep, s_ref, o_ref, order in dirs:
        for hd in range(GLA_HEADS):
            ks = slice(hd * dk, (hd + 1) * dk)
            vs = slice(hd * dv, (hd + 1) * dv)
            chains.append(dict(q_state=q_state[:, ks], q_in=q_in[:, ks], k_in=k_in[:, ks], k_out=k_out[:, ks],
                               decays=[d[:, ks] for d in decays], v=v[:, vs], keep=keep, s_ref=s_ref,
                               o_ref=o_ref, hd=hd, vs=vs, order=order))
    chunk = lambda x, a: x[a * c:(a + 1) * c]
    for ch in chains:
        ch["attn"] = _dot_nt(ch["q_in"], ch["k_in"])
    for ch in chains:
        ch["update"] = [_dot_tn(chunk(ch["v"], a), chunk(ch["k_out"], a)) for a in range(n_chunks)]
    for ch in chains:
        state_t = ch["s_ref"][ch["hd"]]
        ch["before"] = {}
        for a in ch["order"]:
            ch["before"][a] = state_t
            state_t = state_t * ch["decays"][a] + ch["update"][a]
        ch["s_ref"][ch["hd"]] = state_t
    for ch in chains:
        ch["inter"] = jnp.concatenate(
            [_dot_nt(chunk(ch["q_state"], a), ch["before"][a].astype(BF16)) for a in range(n_chunks)], axis=0)
    for ch in chains:
        attn = jnp.where(ch["keep"], ch["attn"], 0.0).astype(BF16)
        ch["o_ref"][0, :, ch["vs"]] = (ch["inter"] + _dot(attn, ch["v"])).astype(BF16)


def _gla(gq, gk, gv, logf, *, n_lat_blocks, n_ctx_blocks):
    b_, lp, qk = gq.shape
    vv = gv.shape[2]
    c = GLA_BLOCK
    n = n_lat_blocks + n_ctx_blocks
    fwd = lambda b, s: (b, (s + n_lat_blocks) % n, 0)
    bwd = lambda b, s: (b, n - 1 - s, 0)
    lf_spec = pl.BlockSpec((1, c, qk), lambda b, s: (b, (s + n_lat_blocks) % n, 0))
    lb_spec = pl.BlockSpec((1, c, qk), lambda b, s: (b, n - 1 - s, 1))
    return pl.pallas_call(
        _gla_kernel,
        out_shape=(jax.ShapeDtypeStruct((b_, lp, vv), BF16), jax.ShapeDtypeStruct((b_, lp, vv), BF16)),
        grid=(b_, n),
        in_specs=[
            pl.BlockSpec((1, c, qk), fwd), pl.BlockSpec((1, c, qk), fwd), pl.BlockSpec((1, c, vv), fwd), lf_spec,
            pl.BlockSpec((1, c, qk), bwd), pl.BlockSpec((1, c, qk), bwd), pl.BlockSpec((1, c, vv), bwd), lb_spec,
        ],
        out_specs=(pl.BlockSpec((1, c, vv), fwd), pl.BlockSpec((1, c, vv), bwd)),
        scratch_shapes=[
            pltpu.VMEM((GLA_HEADS, vv // GLA_HEADS, qk // GLA_HEADS), F32),
            pltpu.VMEM((GLA_HEADS, vv // GLA_HEADS, qk // GLA_HEADS), F32),
        ],
        compiler_params=_params(("parallel", "arbitrary")),
        name="gla_scan",
    )(gq, gk, gv, logf, gq, gk, gv, logf)


def _attn_kernel(q_t_ref, k_ref, v_t_ref, lam_ref, o_ref, m_ref, acc_ref, s_ref, bm_ref,
                 *, tk, n_kblocks, lam_init):
    tq = q_t_ref.shape[2]
    q_t = q_t_ref[0]
    row = lax.broadcasted_iota(jnp.int32, q_t.shape, 0)
    zero = jnp.zeros_like(q_t)
    qz_t = jnp.concatenate([jnp.where(row < DIFF_DH, q_t, zero), jnp.where(row >= DIFF_DH, q_t, zero)], axis=1)
    m_ref[...] = jnp.full(m_ref.shape, -jnp.inf, F32)
    acc_ref[...] = jnp.zeros(acc_ref.shape, F32)

    groups = list(range(2 * tq // ATTN_QGROUP))
    lanes = lambda g: slice(g * ATTN_QGROUP, (g + 1) * ATTN_QGROUP)

    def scores(kb, slot, g):
        s_t = _dot(kb, qz_t[:, lanes(g)])
        s_ref[slot, g] = s_t
        bm_ref[slot, g] = jnp.max(s_t, axis=0, keepdims=True)

    ones_rows = jnp.ones((BF16_ROWS, tk), BF16)

    def softmax_values(vb_t, slot, g):
        s_t = s_ref[slot, g]
        m_prev = m_ref[g]
        m_new = jnp.maximum(m_prev, bm_ref[slot, g])
        m_ref[g] = m_new
        alpha = jnp.exp2(m_prev - m_new)
        p_t = jnp.exp2(s_t - m_new).astype(BF16)
        acc_ref[g] = alpha * acc_ref[g] + _dot(vb_t, p_t)

    def load_keys(j):
        return k_ref[0, pl.ds(pl.multiple_of(j * tk, tk), tk), :]

    def load_values(j):
        vb_t = v_t_ref[0, :, pl.ds(pl.multiple_of(j * tk, tk), tk)]
        return jnp.concatenate([vb_t, ones_rows], axis=0)

    def stage(j, slot, last=False):
        kb = None if last else load_keys(j + 1)
        vb_t = load_values(j)
        for g in groups:
            if not last:
                scores(kb, 1 - slot, g)
            softmax_values(vb_t, slot, g)

    kb0 = load_keys(0)
    for g in groups:
        scores(kb0, 0, g)
    n_pairs = max(n_kblocks - 1, 0) // 2

    def pair(i, carry):
        stage(2 * i, 0)
        stage(2 * i + 1, 1)
        return carry

    lax.fori_loop(0, n_pairs, pair, 0)
    for j in range(2 * n_pairs, n_kblocks):
        stage(j, j % 2, last=j == n_kblocks - 1)

    lp = lam_ref[...]
    lam = (jnp.exp(jnp.sum(lp[0:1] * lp[1:2], axis=-1, keepdims=True))
           - jnp.exp(jnp.sum(lp[2:3] * lp[3:4], axis=-1, keepdims=True)) + lam_init)
    dv = v_t_ref.shape[1]
    acc = jnp.concatenate([acc_ref[g] for g in groups], axis=1)
    o_t = acc[:dv] / acc[dv:dv + 1]
    o_ref[0] = (o_t[:, :tq] - lam * o_t[:, tq:]).T.astype(BF16)


def _attn(dq_t, dk, dv_t, lam_p, *, seq, n_keys, lam_init):
    b_, lp, width = dk.shape
    tq, tk = ATTN_TQ, ATTN_TK
    heads = width // LANES
    n_groups = 2 * tq // ATTN_QGROUP
    kern = functools.partial(_attn_kernel, tk=tk, n_kblocks=n_keys // tk, lam_init=lam_init)
    return pl.pallas_call(
        kern,
        out_shape=jax.ShapeDtypeStruct((b_, seq, width), BF16),
        grid=(b_, heads, seq // tq),
        in_specs=[
            pl.BlockSpec((1, LANES, tq), lambda b, h, i: (b, h, i)),
            pl.BlockSpec((1, lp, LANES), lambda b, h, i: (b, 0, h)),
            pl.BlockSpec((1, LANES, lp), lambda b, h, i: (b, h, 0)),
            pl.BlockSpec(lam_p.shape, lambda b, h, i: (0, 0)),
        ],
        out_specs=pl.BlockSpec((1, tq, LANES), lambda b, h, i: (b, i, h)),
        scratch_shapes=[
            pltpu.VMEM((n_groups, 1, ATTN_QGROUP), F32),
            pltpu.VMEM((n_groups, LANES + BF16_ROWS, ATTN_QGROUP), F32),
            pltpu.VMEM((2, n_groups, tk, ATTN_QGROUP), F32),
            pltpu.VMEM((2, n_groups, 1, ATTN_QGROUP), F32),
        ],
        compiler_params=_params(("parallel", "parallel", "parallel")),
        name="diff_attn",
    )(dq_t, dk, dv_t, lam_p)


def _head_rms(x, gain, width):
    parts = []
    for h0 in range(0, x.shape[1], width):
        parts.append(_rms(x[:, h0:h0 + width]))
    return jnp.concatenate(parts, axis=1) * gain


def _merge_kernel(x_ref, mod_ref, of_ref, ob_ref, sg_ref, od_ref, sm_ref, gg_ref, gd_ref,
                  wa_ref, wb_ref, wo_ref, o_ref, *, dv_gla, dv_diff, diff_scale):
    d = x_ref.shape[2]
    o_gla = of_ref[0].astype(F32) + ob_ref[0].astype(F32)
    ya = _head_rms(o_gla, gg_ref[...], dv_gla) * sg_ref[0].astype(F32)
    ya = _dot(ya.astype(BF16), wa_ref[...])
    yb = _head_rms(od_ref[0].astype(F32), gd_ref[...], dv_diff) * diff_scale
    yb = _dot(yb.astype(BF16), wb_ref[...])
    sm = sm_ref[0].astype(F32)
    merged = (sm[:, :d] * ya + sm[:, d:] * yb).astype(BF16)
    gate = mod_ref[0, 0][5:6]
    o_ref[0] = x_ref[0] + gate * _dot(merged, wo_ref[...])


def _merge(x1, modt, o_f, o_b, sgg, o_d, smg, g_gla, g_diff, wa, wb, wo, *, n_lat_tiles, diff_scale):
    b_, _, d = x1.shape
    tm = TOKEN_TILE
    row = lambda width: pl.BlockSpec((1, tm, width), lambda b, j: (b, j, 0))
    kern = functools.partial(_merge_kernel, dv_gla=o_f.shape[2] // GLA_HEADS,
                             dv_diff=o_d.shape[2] // DIFF_HEADS, diff_scale=diff_scale)
    return pl.pallas_call(
        kern,
        out_shape=jax.ShapeDtypeStruct((b_, n_lat_tiles * tm, d), F32),
        grid=(b_, n_lat_tiles),
        in_specs=[
            row(d),
            pl.BlockSpec((1, 1, MOD_ROWS, d), lambda b, j: (b, 0, 0, 0)),
            row(o_f.shape[2]), row(o_b.shape[2]), row(sgg.shape[2]), row(o_d.shape[2]), row(smg.shape[2]),
            _resident(g_gla.shape), _resident(g_diff.shape),
            _resident(wa.shape), _resident(wb.shape), _resident(wo.shape),
        ],
        out_specs=row(d),
        compiler_params=_params(("parallel", "parallel")),
        name="merge",
    )(x1, modt, o_f, o_b, sgg, o_d, smg, g_gla, g_diff, wa, wb, wo)


def _rope_tables(seq, lp):
    half = DIFF_DH // 2
    quarter = half // 2
    inv = ROPE_BASE ** (-np.arange(quarter, dtype=np.float32) / quarter)
    t = jnp.arange(seq, dtype=jnp.int32)
    ang_row = (t // GRID_W).astype(F32)[:, None] * inv[None, :]
    ang_col = (t % GRID_W).astype(F32)[:, None] * inv[None, :]
    ang = jnp.concatenate([ang_row, ang_row, ang_col, ang_col], axis=1)
    sign = np.tile(np.concatenate([-np.ones(quarter), np.ones(quarter)]), 2).astype(np.float32)
    cos = jnp.tile(jnp.cos(ang), (1, LANES // DIFF_DH))
    sin = jnp.tile(jnp.sin(ang) * sign[None, :], (1, LANES // DIFF_DH))
    pad = lp - seq
    cos = jnp.concatenate([cos, jnp.ones((pad, LANES), F32)], axis=0)
    sin = jnp.concatenate([sin, jnp.zeros((pad, LANES), F32)], axis=0)
    tab_t = jnp.concatenate([jnp.cos(ang_row), jnp.sin(ang_row), jnp.cos(ang_col), jnp.sin(ang_col)], axis=1).T
    ident = np.repeat(np.array([1.0, 0.0, 1.0, 0.0], np.float32), quarter)[:, None]
    tab_t = jnp.concatenate([tab_t, jnp.broadcast_to(ident, (DIFF_DH, pad))], axis=1)
    return cos, sin, tab_t


def kernel(x, c, ctx, c_ctx, w_ada, b_ada, ffn1_norm, ffn1_w1, ffn1_w3, ffn1_w2, mix_norm, w_in, gla_gate_w2, gla_gate_b, gla_out_norm, diff_lambda, diff_out_norm, w_branch_gla, w_branch_diff, w_out, ffn2_norm, ffn2_w1, ffn2_w3, ffn2_w2, final_norm):
    b_, seq, d = x.shape
    n_ctx = ctx.shape[1]
    assert w_ada.shape[0] == 1, "single-layer stack"
    tm = TOKEN_TILE
    assert seq % tm == 0 and seq % ATTN_TQ == 0 and n_ctx <= tm
    assert seq % GLA_BLOCK == 0 and n_ctx % GLA_BLOCK == 0 and (seq + n_ctx) % ATTN_TK == 0
    n_lat_tiles = seq // tm
    n_tiles = n_lat_tiles + 1
    lp = n_tiles * tm
    lam_init = 0.8 - 0.6 * math.exp(0.0)

    gla_qk = GLA_HEADS * (d // 8)
    gla_v = GLA_HEADS * (d // 4)
    dif = DIFF_HEADS * 2 * DIFF_DH
    splits = np.cumsum([gla_qk, gla_qk, gla_v, gla_v, 2 * GLA_RANK, dif, dif, dif, 2 * d])
    assert w_in.shape[2] == splits[-1]
    wi = w_in[0].astype(BF16)
    w_gq, w_gk, w_gv, w_gg, w_lr, w_dq, w_dk, w_dv, w_mg = jnp.split(wi, splits[:-1].tolist(), axis=1)
    w_lr = jnp.pad(w_lr, ((0, 0), (0, LANES - 2 * GLA_RANK)))
    w2cat = jnp.zeros((LANES, 2 * gla_qk), F32)
    w2cat = w2cat.at[:GLA_RANK, :gla_qk].set(gla_gate_w2[0, 0])
    w2cat = w2cat.at[GLA_RANK:2 * GLA_RANK, gla_qk:].set(gla_gate_w2[0, 1]).astype(BF16)
    b2cat = gla_gate_b[0].reshape(1, 2 * gla_qk)

    rows = -(-(b_ + 1) // 8) * 8
    c_rows = jnp.concatenate([c, c_ctx[None, :], jnp.zeros((rows - b_ - 1, d), F32)], axis=0)
    mods = _mod_table(c_rows, w_ada[0], b_ada).reshape(rows, N_MOD, d)
    modt = jnp.stack([mods[:b_], jnp.broadcast_to(mods[b_], (b_, N_MOD, d))], axis=1)
    modt = jnp.pad(modt, ((0, 0), (0, 0), (0, MOD_ROWS - N_MOD), (0, 0)))

    ctx_tile = jnp.pad(ctx, ((0, 0), (0, tm - n_ctx), (0, 0)))
    tiles = dict(n_lat_tiles=n_lat_tiles, n_tiles=n_tiles)

    x1 = _ffn(x, modt, ffn1_norm, ffn1_w1[0].astype(BF16), ffn1_w3[0].astype(BF16),
              ffn1_w2[0].astype(BF16), row0=0, ctx_tile=ctx_tile, **tiles)

    gq, gk, gv, sgg, logf = _proj_gla(x1, modt, mix_norm, w_gq, w_gk, w_gv, w_gg, w_lr, w2cat, b2cat, **tiles)
    cos_t, sin_t, tab_t = _rope_tables(seq, lp)
    dq, dk, dv, smg = _proj_dif(x1, modt, mix_norm, cos_t, sin_t, tab_t, w_dq.T, w_dk, w_dv.T, w_mg, **tiles)

    o_f, o_b = _gla(gq, gk, gv, logf, n_lat_blocks=seq // GLA_BLOCK, n_ctx_blocks=n_ctx // GLA_BLOCK)

    lam_p = jnp.pad(diff_lambda[0], ((0, 4), (0, LANES - DIFF_DH)))
    o_d = _attn(dq, dk, dv, lam_p, seq=seq, n_keys=seq + n_ctx, lam_init=lam_init)

    x2 = _merge(x1, modt, o_f, o_b, sgg, o_d, smg, gla_out_norm, diff_out_norm,
                w_branch_gla[0].astype(BF16), w_branch_diff[0].astype(BF16), w_out[0].astype(BF16),
                n_lat_tiles=n_lat_tiles, diff_scale=1.0 - lam_init)

    return _ffn(x2, modt, ffn2_norm, ffn2_w1[0].astype(BF16), ffn2_w3[0].astype(BF16),
                ffn2_w2[0].astype(BF16), row0=6, n_lat_tiles=n_lat_tiles, n_tiles=n_lat_tiles,
                final_gain=final_norm.reshape(1, d))
```

```python
import functools
import math

import jax
import jax.numpy as jnp
import numpy as np
from jax import lax
from jax.experimental import pallas as pl
from jax.experimental.pallas import tpu as pltpu

F32 = jnp.float32
BF16 = jnp.bfloat16

EPS = 1e-6
GRID_W = 64
N_MOD = 9
MOD_ROWS = 16
GLA_HEADS = 4
GLA_RANK = 16
GLA_TAU = 16.0
GLA_CHUNK = 64
GLA_BLOCK = 256
DIFF_HEADS = 8
DIFF_DH = 64
ROPE_BASE = 10000.0
LANES = 128
MXU_TILE = 256
BF16_ROWS = 16
LOG2_E = 1.4426950408889634

TOKEN_TILE = 512
ATTN_TQ = 1024
ATTN_TK = 768
ATTN_QGROUP = 256
VMEM_LIMIT = 52 * 1024 * 1024


def _resident(shape):
    zeros = (0,) * len(shape)
    return pl.BlockSpec(shape, lambda *_: zeros, pipeline_mode=pl.Buffered(1))


def _params(semantics):
    return pltpu.CompilerParams(dimension_semantics=semantics, vmem_limit_bytes=VMEM_LIMIT)


def _dot(a, b):
    return jnp.dot(a, b, preferred_element_type=F32)


def _dot_nt(a, b):
    return lax.dot_general(a, b, (((1,), (1,)), ((), ())), preferred_element_type=F32)


def _dot_tn(a, b):
    return lax.dot_general(a, b, (((0,), (0,)), ((), ())), preferred_element_type=F32)


def _sigmoid(x):
    return 1.0 / (1.0 + jnp.exp(-x))


def _rms(x):
    return x * lax.rsqrt(jnp.mean(x * x, axis=-1, keepdims=True) + EPS)


def _modnorm(x, gain, shift, scale1):
    return (_rms(x) * gain) * scale1 + shift


def _mod_kernel(c_ref, w_ref, b_ref, o_ref):
    k = pl.program_id(0)
    c = c_ref[...]
    s = c * _sigmoid(c)
    acc = jnp.zeros(o_ref.shape, F32)
    w = w_ref[...]
    s_parts = _split3(s)
    w_parts = _split3(w)
    for i in range(3):
        for j in range(3 - i):
            acc = acc + _dot(s_parts[i], w_parts[j])
    val = acc + b_ref[...]
    add = jnp.where(k % 3 == 1, 1.0, 0.0).astype(F32)
    mul = jnp.where((k == 2) | (k == 8), 0.5, 1.0).astype(F32)
    o_ref[...] = (val + add) * mul


def _split3(x):
    hi = x.astype(BF16)
    r1 = x - hi.astype(F32)
    mid = r1.astype(BF16)
    lo = (r1 - mid.astype(F32)).astype(BF16)
    return hi, mid, lo


def _mod_table(c_rows, w_ada, b_ada):
    rows, d = c_rows.shape
    return pl.pallas_call(
        _mod_kernel,
        out_shape=jax.ShapeDtypeStruct((rows, N_MOD * d), F32),
        grid=(N_MOD,),
        in_specs=[
            pl.BlockSpec((rows, d), lambda k: (0, 0)),
            pl.BlockSpec((d, d), lambda k: (0, k)),
            pl.BlockSpec((1, d), lambda k: (0, k)),
        ],
        out_specs=pl.BlockSpec((rows, d), lambda k: (0, k)),
        compiler_params=_params(("arbitrary",)),
        name="mod_table",
    )(c_rows, w_ada, b_ada)


def _ffn_kernel(x_ref, mod_ref, g_ref, w1_ref, w3_ref, w2_ref, *rest, row0, f_chunk, n_lat_tiles, final):
    if final:
        gf_ref, o_ref = rest
        x = x_ref[0]
    else:
        c_ref, o_ref = rest
        x = jnp.where(pl.program_id(1) >= n_lat_tiles, c_ref[0], x_ref[0])
    mod = mod_ref[0, 0]
    shift, scale1, gate = mod[row0:row0 + 1], mod[row0 + 1:row0 + 2], mod[row0 + 2:row0 + 3]
    h = _modnorm(x, g_ref[...], shift, scale1).astype(BF16)
    acc = jnp.zeros(x.shape, F32)
    f_total = w1_ref.shape[1]
    for f0 in range(0, f_total, f_chunk):
        f1 = min(f0 + f_chunk, f_total)
        a = _dot(h, w1_ref[:, f0:f1])
        b = _dot(h, w3_ref[:, f0:f1])
        u = (a * _sigmoid(a) * b).astype(BF16)
        acc = acc + _dot(u, w2_ref[f0:f1, :])
    y = x + gate * acc
    if final:
        y = _rms(y) * gf_ref[...]
    o_ref[0] = y


def _ffn(x, modt, gain, w1, w3, w2, *, row0, n_lat_tiles, n_tiles, ctx_tile=None, final_gain=None):
    b_, _, d = x.shape
    f = w1.shape[1]
    tm = TOKEN_TILE
    final = final_gain is not None
    assert final != (ctx_tile is not None)
    f_chunk = -(-f // (2 * MXU_TILE)) * MXU_TILE
    kern = functools.partial(_ffn_kernel, row0=row0, f_chunk=f_chunk, n_lat_tiles=n_lat_tiles, final=final)
    in_specs = [
        pl.BlockSpec((1, tm, d), lambda b, j: (b, jnp.minimum(j, n_lat_tiles - 1), 0)),
        pl.BlockSpec((1, 1, MOD_ROWS, d), lambda b, j: (b, (j >= n_lat_tiles).astype(jnp.int32), 0, 0)),
        _resident((1, d)),
        _resident((d, f)),
        _resident((d, f)),
        _resident((f, d)),
    ]
    args = [x, modt, gain, w1, w3, w2]
    if final:
        in_specs.append(_resident((1, d)))
        args.append(final_gain)
    else:
        in_specs.append(pl.BlockSpec((1, tm, d), lambda b, j: (b, 0, 0)))
        args.append(ctx_tile)
    return pl.pallas_call(
        kern,
        out_shape=jax.ShapeDtypeStruct((b_, n_tiles * tm, d), F32),
        grid=(b_, n_tiles),
        in_specs=in_specs,
        out_specs=pl.BlockSpec((1, tm, d), lambda b, j: (b, j, 0)),
        compiler_params=_params(("parallel", "parallel")),
        name="ffn_final" if final else "ffn",
    )(*args)


def _run_pieces(pieces):
    pending = None
    for matmul, tail in pieces:
        result = matmul()
        if pending is not None:
            pending[0](pending[1])
        pending = (tail, result)
    pending[0](pending[1])


def _interleave(a, b):
    n = min(len(a), len(b))
    return [p for pair in zip(a, b) for p in pair] + a[n:] + b[n:]


def _column_pieces(lhs, w_ref, tail_of):
    def piece(cols):
        return (lambda: _dot(lhs, w_ref[:, cols])), tail_of(cols)
    return [piece(slice(c0, c0 + MXU_TILE)) for c0 in range(0, w_ref.shape[1], MXU_TILE)]


def _proj_gla_kernel(x_ref, mod_ref, g_ref, wq_ref, wk_ref, wv_ref, wg_ref, wlr_ref, w2_ref, b2_ref,
                     q_ref, k_ref, v_ref, sg_ref, lf_ref, *, q_scale):
    x = x_ref[0]
    mod = mod_ref[0, 0]
    h = _modnorm(x, g_ref[...], mod[3:4], mod[4:5]).astype(BF16)
    lr = _dot(h, wlr_ref[...]).astype(BF16)

    def decay_tail(cols):
        def tail(z):
            z = z + b2_ref[:, cols]
            lf_ref[0, :, cols] = (jnp.minimum(z, 0.0) - jnp.log(1.0 + jnp.exp(-jnp.abs(z)))) * (1.0 / GLA_TAU)
        return tail

    def gate_tail(cols):
        def tail(gg):
            sg_ref[0, :, cols] = (gg * _sigmoid(gg)).astype(BF16)
        return tail

    def value_tail(cols):
        def tail(v):
            v_ref[0, :, cols] = v.astype(BF16)
        return tail

    def query_tail(cols):
        def tail(q):
            q_ref[0, :, cols] = q * q_scale
        return tail

    def key_tail(cols):
        def tail(k):
            k_ref[0, :, cols] = k
        return tail

    heavy = _interleave(_column_pieces(lr, w2_ref, decay_tail), _column_pieces(h, wg_ref, gate_tail))
    light = (_column_pieces(h, wv_ref, value_tail) + _column_pieces(h, wq_ref, query_tail)
             + _column_pieces(h, wk_ref, key_tail))
    _run_pieces(_interleave(heavy, light))


def _proj_gla(xs, modt, gain, wq, wk, wv, wg, wlr, w2cat, b2cat, *, n_lat_tiles, n_tiles):
    b_, lp, d = xs.shape
    tm = TOKEN_TILE
    qk = wq.shape[1]
    vv = wv.shape[1]
    row = lambda width: pl.BlockSpec((1, tm, width), lambda b, j: (b, j, 0))
    kern = functools.partial(_proj_gla_kernel, q_scale=(qk // GLA_HEADS) ** -0.5)
    return pl.pallas_call(
        kern,
        out_shape=(
            jax.ShapeDtypeStruct((b_, lp, qk), F32),
            jax.ShapeDtypeStruct((b_, lp, qk), F32),
            jax.ShapeDtypeStruct((b_, lp, vv), BF16),
            jax.ShapeDtypeStruct((b_, lp, vv), BF16),
            jax.ShapeDtypeStruct((b_, lp, 2 * qk), F32),
        ),
        grid=(b_, n_tiles),
        in_specs=[
            row(d),
            pl.BlockSpec((1, 1, MOD_ROWS, d), lambda b, j: (b, (j >= n_lat_tiles).astype(jnp.int32), 0, 0)),
            _resident((1, d)),
            _resident(wq.shape), _resident(wk.shape), _resident(wv.shape), _resident(wg.shape),
            _resident(wlr.shape), _resident(w2cat.shape), _resident(b2cat.shape),
        ],
        out_specs=(row(qk), row(qk), row(vv), row(vv), row(2 * qk)),
        compiler_params=_params(("parallel", "parallel")),
        name="proj_gla",
    )(xs, modt, gain, wq, wk, wv, wg, wlr, w2cat, b2cat)


def _rope(u, cos, sin_signed, first_half):
    out = []
    for g in range(u.shape[1] // LANES):
        ug = u[:, g * LANES:(g + 1) * LANES]
        partner = jnp.where(first_half, pltpu.roll(ug, LANES - 16, 1), pltpu.roll(ug, 16, 1))
        out.append(ug * cos + partner * sin_signed)
    return jnp.concatenate(out, axis=1)


def _proj_dif_kernel(x_ref, mod_ref, g_ref, cos_ref, sin_ref, tab_t_ref, wq_t_ref, wk_ref, wv_t_ref, wm_ref,
                     q_t_ref, k_ref, v_t_ref, sm_ref, *, q_scale):
    x = x_ref[0]
    mod = mod_ref[0, 0]
    h = _modnorm(x, g_ref[...], mod[3:4], mod[4:5]).astype(BF16)
    quarter = DIFF_DH // 4
    tab = tab_t_ref[...]
    cos = cos_ref[...]
    sin_signed = sin_ref[...]
    lane = lax.broadcasted_iota(jnp.int32, cos.shape, 1)
    first_half = (lane % 32) < 16

    t_rows = 2 * MXU_TILE

    def row_pieces(w_t_ref, tail_of):
        def piece(r0):
            return (lambda: _dot_nt(w_t_ref[r0:r0 + t_rows, :], h)), tail_of(r0)
        return [piece(r0) for r0 in range(0, w_t_ref.shape[0], t_rows)]

    def query_tail(r0):
        def tail(q_t):
            for base in range(0, t_rows, 2 * quarter):
                t0 = base % DIFF_DH
                c, s = tab[t0:t0 + quarter], tab[t0 + quarter:t0 + 2 * quarter]
                u1 = q_t[base:base + quarter]
                u2 = q_t[base + quarter:base + 2 * quarter]
                q_t_ref[0, r0 + base:r0 + base + quarter, :] = ((u1 * c - u2 * s) * q_scale).astype(BF16)
                q_t_ref[0, r0 + base + quarter:r0 + base + 2 * quarter, :] = (
                    (u2 * c + u1 * s) * q_scale).astype(BF16)
        return tail

    def key_tail(cols):
        def tail(k):
            k_ref[0, :, cols] = _rope(k, cos, sin_signed, first_half).astype(BF16)
        return tail

    def value_tail(r0):
        def tail(v_t):
            v_t_ref[0, r0:r0 + t_rows, :] = v_t.astype(BF16)
        return tail

    def gate_tail(cols):
        def tail(mg):
            sm_ref[0, :, cols] = _sigmoid(mg).astype(BF16)
        return tail

    rotary = _interleave(row_pieces(wq_t_ref, query_tail), _column_pieces(h, wk_ref, key_tail))
    plain = _interleave(_column_pieces(h, wm_ref, gate_tail), row_pieces(wv_t_ref, value_tail))
    _run_pieces(_interleave(rotary, plain))


def _proj_dif(xs, modt, gain, cos_t, sin_t, tab_t, wq_t, wk, wv_t, wm, *, n_lat_tiles, n_tiles):
    b_, lp, d = xs.shape
    tm = TOKEN_TILE
    row = lambda width: pl.BlockSpec((1, tm, width), lambda b, j: (b, j, 0))
    col = lambda height: pl.BlockSpec((1, height, tm), lambda b, j: (b, 0, j))
    tab = pl.BlockSpec((tm, LANES), lambda b, j: (j, 0))
    kern = functools.partial(_proj_dif_kernel, q_scale=DIFF_DH ** -0.5 * LOG2_E)
    return pl.pallas_call(
        kern,
        out_shape=(
            jax.ShapeDtypeStruct((b_, wq_t.shape[0], lp), BF16),
            jax.ShapeDtypeStruct((b_, lp, wk.shape[1]), BF16),
            jax.ShapeDtypeStruct((b_, wv_t.shape[0], lp), BF16),
            jax.ShapeDtypeStruct((b_, lp, wm.shape[1]), BF16),
        ),
        grid=(b_, n_tiles),
        in_specs=[
            row(d),
            pl.BlockSpec((1, 1, MOD_ROWS, d), lambda b, j: (b, (j >= n_lat_tiles).astype(jnp.int32), 0, 0)),
            _resident((1, d)),
            tab, tab,
            pl.BlockSpec((DIFF_DH, tm), lambda b, j: (0, j)),
            _resident(wq_t.shape), _resident(wk.shape), _resident(wv_t.shape), _resident(wm.shape),
        ],
        out_specs=(col(wq_t.shape[0]), row(wk.shape[1]), col(wv_t.shape[0]), row(wm.shape[1])),
        compiler_params=_params(("parallel", "parallel")),
        name="proj_dif",
    )(xs, modt, gain, cos_t, sin_t, tab_t, wq_t, wk, wv_t, wm)


def _gla_decays(q, k, logf, tri, ref_row, end_row):
    c = GLA_CHUNK
    hi, lo, _ = _split3(logf)
    cum = _dot(tri, hi) + _dot(tri, lo)
    n_chunks = q.shape[0] // c
    totals = [cum[a * c + end_row:a * c + end_row + 1] for a in range(n_chunks)]
    per_chunk = lambda rows: jnp.concatenate([jnp.broadcast_to(r, (c, r.shape[1])) for r in rows], axis=0)
    ref = per_chunk([cum[a * c + ref_row:a * c + ref_row + 1] for a in range(n_chunks)])
    total = per_chunk(totals)
    q_state = (q * jnp.exp(cum)).astype(BF16)
    q_in = (q * jnp.exp(cum - ref)).astype(BF16)
    k_in = (k * jnp.exp(ref - cum)).astype(BF16)
    k_out = (k * jnp.exp(total - cum)).astype(BF16)
    return q_state, q_in, k_in, k_out, [jnp.exp(t) for t in totals]


def _gla_kernel(qf_ref, kf_ref, vf_ref, lf_ref, qb_ref, kb_ref, vb_ref, lb_ref,
                of_ref, ob_ref, sf_ref, sb_ref):
    @pl.when(pl.program_id(1) == 0)
    def _():
        sf_ref[...] = jnp.zeros(sf_ref.shape, F32)
        sb_ref[...] = jnp.zeros(sb_ref.shape, F32)

    c = GLA_CHUNK
    rows = qf_ref.shape[1]
    n_chunks = rows // c
    row = lax.broadcasted_iota(jnp.int32, (rows, rows), 0)
    col = lax.broadcasted_iota(jnp.int32, (rows, rows), 1)
    same_chunk = (row // c) == (col // c)
    lower = same_chunk & (col <= row)
    upper = same_chunk & (col >= row)
    dirs = [
        (_gla_decays(qf_ref[0], kf_ref[0], lf_ref[0], jnp.where(lower, 1.0, 0.0).astype(BF16),
                     c // 2 - 1, c - 1), vf_ref[0], lower, sf_ref, of_ref, list(range(n_chunks))),
        (_gla_decays(qb_ref[0], kb_ref[0], lb_ref[0], jnp.where(upper, 1.0, 0.0).astype(BF16),
                     c // 2, 0), vb_ref[0], upper, sb_ref, ob_ref, list(range(n_chunks))[::-1]),
    ]
    dk = qf_ref.shape[2] // GLA_HEADS
    dv = vf_ref.shape[2] // GLA_HEADS
    chains = []
    for (q_state, q_in, k_in, k_out, decays), v, keep, s_ref, o_ref, order in dirs:
        for hd in range(GLA_HEADS):
            ks = slice(hd * dk, (hd + 1) * dk)
            vs = slice(hd * dv, (hd + 1) * dv)
            chains.append(dict(q_state=q_state[:, ks], q_in=q_in[:, ks], k_in=k_in[:, ks], k_out=k_out[:, ks],
                               decays=[d[:, ks] for d in decays], v=v[:, vs], keep=keep, s_ref=s_ref,
                               o_ref=o_ref, hd=hd, vs=vs, order=order))
    chunk = lambda x, a: x[a * c:(a + 1) * c]
    for ch in chains:
        ch["attn"] = _dot_nt(ch["q_in"], ch["k_in"])
    for ch in chains:
        ch["update"] = [_dot_tn(chunk(ch["v"], a), chunk(ch["k_out"], a)) for a in range(n_chunks)]
    for ch in chains:
        state_t = ch["s_ref"][ch["hd"]]
        ch["before"] = {}
        for a in ch["order"]:
            ch["before"][a] = state_t
            state_t = state_t * ch["decays"][a] + ch["update"][a]
        ch["s_ref"][ch["hd"]] = state_t
    for ch in chains:
        ch["inter"] = jnp.concatenate(
            [_dot_nt(chunk(ch["q_state"], a), ch["before"][a].astype(BF16)) for a in range(n_chunks)], axis=0)
    for ch in chains:
        attn = jnp.where(ch["keep"], ch["attn"], 0.0).astype(BF16)
        ch["o_ref"][0, :, ch["vs"]] = (ch["inter"] + _dot(attn, ch["v"])).astype(BF16)


def _gla(gq, gk, gv, logf, *, n_lat_blocks, n_ctx_blocks):
    b_, lp, qk = gq.shape
    vv = gv.shape[2]
    c = GLA_BLOCK
    n = n_lat_blocks + n_ctx_blocks
    fwd = lambda b, s: (b, (s + n_lat_blocks) % n, 0)
    bwd = lambda b, s: (b, n - 1 - s, 0)
    lf_spec = pl.BlockSpec((1, c, qk), lambda b, s: (b, (s + n_lat_blocks) % n, 0))
    lb_spec = pl.BlockSpec((1, c, qk), lambda b, s: (b, n - 1 - s, 1))
    return pl.pallas_call(
        _gla_kernel,
        out_shape=(jax.ShapeDtypeStruct((b_, n * c, vv), BF16), jax.ShapeDtypeStruct((b_, n * c, vv), BF16)),
        grid=(b_, n),
        in_specs=[
            pl.BlockSpec((1, c, qk), fwd), pl.BlockSpec((1, c, qk), fwd), pl.BlockSpec((1, c, vv), fwd), lf_spec,
            pl.BlockSpec((1, c, qk), bwd), pl.BlockSpec((1, c, qk), bwd), pl.BlockSpec((1, c, vv), bwd), lb_spec,
        ],
        out_specs=(pl.BlockSpec((1, c, vv), fwd), pl.BlockSpec((1, c, vv), bwd)),
        scratch_shapes=[
            pltpu.VMEM((GLA_HEADS, vv // GLA_HEADS, qk // GLA_HEADS), F32),
            pltpu.VMEM((GLA_HEADS, vv // GLA_HEADS, qk // GLA_HEADS), F32),
        ],
        compiler_params=_params(("parallel", "arbitrary")),
        name="gla_scan",
    )(gq, gk, gv, logf, gq, gk, gv, logf)


def _attn_kernel(q_t_ref, k_ref, v_t_ref, lam_ref, o_ref, m_ref, acc_ref, s_ref, bm_ref,
                 *, tk, n_kblocks, lam_init):
    tq = q_t_ref.shape[2]
    q_t = q_t_ref[0]
    row = lax.broadcasted_iota(jnp.int32, q_t.shape, 0)
    zero = jnp.zeros_like(q_t)
    qz_t = jnp.concatenate([jnp.where(row < DIFF_DH, q_t, zero), jnp.where(row >= DIFF_DH, q_t, zero)], axis=1)
    m_ref[...] = jnp.full(m_ref.shape, -jnp.inf, F32)
    acc_ref[...] = jnp.zeros(acc_ref.shape, F32)

    groups = list(range(2 * tq // ATTN_QGROUP))
    lanes = lambda g: slice(g * ATTN_QGROUP, (g + 1) * ATTN_QGROUP)

    def scores(kb, slot, g):
        s_t = _dot(kb, qz_t[:, lanes(g)])
        s_ref[slot, g] = s_t
        bm_ref[slot, g] = jnp.max(s_t, axis=0, keepdims=True)

    ones_rows = jnp.ones((BF16_ROWS, tk), BF16)

    def softmax_values(vb_t, slot, g):
        s_t = s_ref[slot, g]
        m_prev = m_ref[g]
        m_new = jnp.maximum(m_prev, bm_ref[slot, g])
        m_ref[g] = m_new
        alpha = jnp.exp2(m_prev - m_new)
        p_t = jnp.exp2(s_t - m_new).astype(BF16)
        acc_ref[g] = alpha * acc_ref[g] + _dot(vb_t, p_t)

    def load_keys(j):
        return k_ref[0, pl.ds(pl.multiple_of(j * tk, tk), tk), :]

    def load_values(j):
        vb_t = v_t_ref[0, :, pl.ds(pl.multiple_of(j * tk, tk), tk)]
        return jnp.concatenate([vb_t, ones_rows], axis=0)

    def stage(j, slot, last=False):
        kb = None if last else load_keys(j + 1)
        vb_t = load_values(j)
        for g in groups:
            if not last:
                scores(kb, 1 - slot, g)
            softmax_values(vb_t, slot, g)

    kb0 = load_keys(0)
    for g in groups:
        scores(kb0, 0, g)
    n_pairs = max(n_kblocks - 1, 0) // 2

    def pair(i, carry):
        stage(2 * i, 0)
        stage(2 * i + 1, 1)
        return carry

    lax.fori_loop(0, n_pairs, pair, 0)
    for j in range(2 * n_pairs, n_kblocks):
        stage(j, j % 2, last=j == n_kblocks - 1)

    lp = lam_ref[...]
    lam = (jnp.exp(jnp.sum(lp[0:1] * lp[1:2], axis=-1, keepdims=True))
           - jnp.exp(jnp.sum(lp[2:3] * lp[3:4], axis=-1, keepdims=True)) + lam_init)
    dv = v_t_ref.shape[1]
    acc = jnp.concatenate([acc_ref[g] for g in groups], axis=1)
    o_t = acc[:dv] / acc[dv:dv + 1]
    o_ref[0] = (o_t[:, :tq] - lam * o_t[:, tq:]).T.astype(BF16)


def _attn(dq_t, dk, dv_t, lam_p, *, seq, n_keys, lam_init):
    b_, lp, width = dk.shape
    tq, tk = ATTN_TQ, ATTN_TK
    heads = width // LANES
    n_groups = 2 * tq // ATTN_QGROUP
    kern = functools.partial(_attn_kernel, tk=tk, n_kblocks=n_keys // tk, lam_init=lam_init)
    return pl.pallas_call(
        kern,
        out_shape=jax.ShapeDtypeStruct((b_, seq, width), BF16),
        grid=(b_, heads, seq // tq),
        in_specs=[
            pl.BlockSpec((1, LANES, tq), lambda b, h, i: (b, h, i)),
            pl.BlockSpec((1, lp, LANES), lambda b, h, i: (b, 0, h)),
            pl.BlockSpec((1, LANES, lp), lambda b, h, i: (b, h, 0)),
            pl.BlockSpec(lam_p.shape, lambda b, h, i: (0, 0)),
        ],
        out_specs=pl.BlockSpec((1, tq, LANES), lambda b, h, i: (b, i, h)),
        scratch_shapes=[
            pltpu.VMEM((n_groups, 1, ATTN_QGROUP), F32),
            pltpu.VMEM((n_groups, LANES + BF16_ROWS, ATTN_QGROUP), F32),
            pltpu.VMEM((2, n_groups, tk, ATTN_QGROUP), F32),
            pltpu.VMEM((2, n_groups, 1, ATTN_QGROUP), F32),
        ],
        compiler_params=_params(("parallel", "parallel", "parallel")),
        name="diff_attn",
    )(dq_t, dk, dv_t, lam_p)


def _head_rms(x, gain, width):
    parts = []
    for h0 in range(0, x.shape[1], width):
        parts.append(_rms(x[:, h0:h0 + width]))
    return jnp.concatenate(parts, axis=1) * gain


def _merge_kernel(x_ref, mod_ref, of_ref, ob_ref, sg_ref, od_ref, sm_ref, gg_ref, gd_ref,
                  wa_ref, wb_ref, wo_ref, o_ref, *, dv_gla, dv_diff, diff_scale):
    d = x_ref.shape[2]
    o_gla = of_ref[0].astype(F32) + ob_ref[0].astype(F32)
    ya = _head_rms(o_gla, gg_ref[...], dv_gla) * sg_ref[0].astype(F32)
    ya = _dot(ya.astype(BF16), wa_ref[...])
    yb = _head_rms(od_ref[0].astype(F32), gd_ref[...], dv_diff) * diff_scale
    yb = _dot(yb.astype(BF16), wb_ref[...])
    sm = sm_ref[0].astype(F32)
    merged = (sm[:, :d] * ya + sm[:, d:] * yb).astype(BF16)
    gate = mod_ref[0, 0][5:6]
    o_ref[0] = x_ref[0] + gate * _dot(merged, wo_ref[...])


def _merge(x1, modt, o_f, o_b, sgg, o_d, smg, g_gla, g_diff, wa, wb, wo, *, n_lat_tiles, diff_scale):
    b_, _, d = x1.shape
    tm = TOKEN_TILE
    row = lambda width: pl.BlockSpec((1, tm, width), lambda b, j: (b, j, 0))
    kern = functools.partial(_merge_kernel, dv_gla=o_f.shape[2] // GLA_HEADS,
                             dv_diff=o_d.shape[2] // DIFF_HEADS, diff_scale=diff_scale)
    return pl.pallas_call(
        kern,
        out_shape=jax.ShapeDtypeStruct((b_, n_lat_tiles * tm, d), F32),
        grid=(b_, n_lat_tiles),
        in_specs=[
            row(d),
            pl.BlockSpec((1, 1, MOD_ROWS, d), lambda b, j: (b, 0, 0, 0)),
            row(o_f.shape[2]), row(o_b.shape[2]), row(sgg.shape[2]), row(o_d.shape[2]), row(smg.shape[2]),
            _resident(g_gla.shape), _resident(g_diff.shape),
            _resident(wa.shape), _resident(wb.shape), _resident(wo.shape),
        ],
        out_specs=row(d),
        compiler_params=_params(("parallel", "parallel")),
        name="merge",
    )(x1, modt, o_f, o_b, sgg, o_d, smg, g_gla, g_diff, wa, wb, wo)


def _rope_tables(seq, lp):
    half = DIFF_DH // 2
    quarter = half // 2
    inv = ROPE_BASE ** (-np.arange(quarter, dtype=np.float32) / quarter)
    t = jnp.arange(seq, dtype=jnp.int32)
    ang_row = (t // GRID_W).astype(F32)[:, None] * inv[None, :]
    ang_col = (t % GRID_W).astype(F32)[:, None] * inv[None, :]
    ang = jnp.concatenate([ang_row, ang_row, ang_col, ang_col], axis=1)
    sign = np.tile(np.concatenate([-np.ones(quarter), np.ones(quarter)]), 2).astype(np.float32)
    cos = jnp.tile(jnp.cos(ang), (1, LANES // DIFF_DH))
    sin = jnp.tile(jnp.sin(ang) * sign[None, :], (1, LANES // DIFF_DH))
    pad = lp - seq
    cos = jnp.concatenate([cos, jnp.ones((pad, LANES), F32)], axis=0)
    sin = jnp.concatenate([sin, jnp.zeros((pad, LANES), F32)], axis=0)
    tab_t = jnp.concatenate([jnp.cos(ang_row), jnp.sin(ang_row), jnp.cos(ang_col), jnp.sin(ang_col)], axis=1).T
    ident = np.repeat(np.array([1.0, 0.0, 1.0, 0.0], np.float32), quarter)[:, None]
    tab_t = jnp.concatenate([tab_t, jnp.broadcast_to(ident, (DIFF_DH, pad))], axis=1)
    return cos, sin, tab_t


def kernel(x, c, ctx, c_ctx, w_ada, b_ada, ffn1_norm, ffn1_w1, ffn1_w3, ffn1_w2, mix_norm, w_in, gla_gate_w2, gla_gate_b, gla_out_norm, diff_lambda, diff_out_norm, w_branch_gla, w_branch_diff, w_out, ffn2_norm, ffn2_w1, ffn2_w3, ffn2_w2, final_norm):
    b_, seq, d = x.shape
    n_ctx = ctx.shape[1]
    assert w_ada.shape[0] == 1, "single-layer stack"
    tm = TOKEN_TILE
    assert seq % tm == 0 and seq % ATTN_TQ == 0 and n_ctx <= tm
    assert seq % GLA_BLOCK == 0 and n_ctx % GLA_BLOCK == 0 and (seq + n_ctx) % ATTN_TK == 0
    n_lat_tiles = seq // tm
    n_tiles = n_lat_tiles + 1
    lp = n_tiles * tm
    lam_init = 0.8 - 0.6 * math.exp(0.0)

    gla_qk = GLA_HEADS * (d // 8)
    gla_v = GLA_HEADS * (d // 4)
    dif = DIFF_HEADS * 2 * DIFF_DH
    splits = np.cumsum([gla_qk, gla_qk, gla_v, gla_v, 2 * GLA_RANK, dif, dif, dif, 2 * d])
    assert w_in.shape[2] == splits[-1]
    wi = w_in[0].astype(BF16)
    w_gq, w_gk, w_gv, w_gg, w_lr, w_dq, w_dk, w_dv, w_mg = jnp.split(wi, splits[:-1].tolist(), axis=1)
    w_lr = jnp.pad(w_lr, ((0, 0), (0, LANES - 2 * GLA_RANK)))
    w2cat = jnp.zeros((LANES, 2 * gla_qk), F32)
    w2cat = w2cat.at[:GLA_RANK, :gla_qk].set(gla_gate_w2[0, 0])
    w2cat = w2cat.at[GLA_RANK:2 * GLA_RANK, gla_qk:].set(gla_gate_w2[0, 1]).astype(BF16)
    b2cat = gla_gate_b[0].reshape(1, 2 * gla_qk)

    rows = -(-(b_ + 1) // 8) * 8
    c_rows = jnp.concatenate([c, c_ctx[None, :], jnp.zeros((rows - b_ - 1, d), F32)], axis=0)
    mods = _mod_table(c_rows, w_ada[0], b_ada).reshape(rows, N_MOD, d)
    modt = jnp.stack([mods[:b_], jnp.broadcast_to(mods[b_], (b_, N_MOD, d))], axis=1)
    modt = jnp.pad(modt, ((0, 0), (0, 0), (0, MOD_ROWS - N_MOD), (0, 0)))

    ctx_tile = jnp.pad(ctx, ((0, 0), (0, tm - n_ctx), (0, 0)))
    tiles = dict(n_lat_tiles=n_lat_tiles, n_tiles=n_tiles)

    x1 = _ffn(x, modt, ffn1_norm, ffn1_w1[0].astype(BF16), ffn1_w3[0].astype(BF16),
              ffn1_w2[0].astype(BF16), row0=0, ctx_tile=ctx_tile, **tiles)

    gq, gk, gv, sgg, logf = _proj_gla(x1, modt, mix_norm, w_gq, w_gk, w_gv, w_gg, w_lr, w2cat, b2cat, **tiles)
    cos_t, sin_t, tab_t = _rope_tables(seq, lp)
    dq, dk, dv, smg = _proj_dif(x1, modt, mix_norm, cos_t, sin_t, tab_t, w_dq.T, w_dk, w_dv.T, w_mg, **tiles)

    o_f, o_b = _gla(gq, gk, gv, logf, n_lat_blocks=seq // GLA_BLOCK, n_ctx_blocks=n_ctx // GLA_BLOCK)

    lam_p = jnp.pad(diff_lambda[0], ((0, 4), (0, LANES - DIFF_DH)))
    o_d = _attn(dq, dk, dv, lam_p, seq=seq, n_keys=seq + n_ctx, lam_init=lam_init)

    x2 = _merge(x1, modt, o_f, o_b, sgg, o_d, smg, gla_out_norm, diff_out_norm,
                w_branch_gla[0].astype(BF16), w_branch_diff[0].astype(BF16), w_out[0].astype(BF16),
                n_lat_tiles=n_lat_tiles, diff_scale=1.0 - lam_init)

    return _ffn(x2, modt, ffn2_norm, ffn2_w1[0].astype(BF16), ffn2_w3[0].astype(BF16),
                ffn2_w2[0].astype(BF16), row0=6, n_lat_tiles=n_lat_tiles, n_tiles=n_lat_tiles,
                final_gain=final_norm.reshape(1, d))
```

```python
import functools
import math

import jax
import jax.numpy as jnp
import numpy as np
from jax import lax
from jax.experimental import pallas as pl
from jax.experimental.pallas import tpu as pltpu

F32 = jnp.float32
BF16 = jnp.bfloat16

EPS = 1e-6
GRID_W = 64
N_MOD = 9
MOD_ROWS = 16
GLA_HEADS = 4
GLA_RANK = 16
GLA_TAU = 16.0
GLA_CHUNK = 64
GLA_BLOCK = 256
DIFF_HEADS = 8
DIFF_DH = 64
ROPE_BASE = 10000.0
LANES = 128
MXU_TILE = 256
BF16_ROWS = 16
LOG2_E = 1.4426950408889634

TOKEN_TILE = 512
ATTN_TQ = 1024
ATTN_TK = 768
ATTN_QGROUP = 256
SCORE_PAD_ROWS = 8
VMEM_LIMIT = 52 * 1024 * 1024


def _resident(shape):
    zeros = (0,) * len(shape)
    return pl.BlockSpec(shape, lambda *_: zeros, pipeline_mode=pl.Buffered(1))


def _params(semantics):
    return pltpu.CompilerParams(dimension_semantics=semantics, vmem_limit_bytes=VMEM_LIMIT)


def _dot(a, b):
    return jnp.dot(a, b, preferred_element_type=F32)


def _dot_nt(a, b):
    return lax.dot_general(a, b, (((1,), (1,)), ((), ())), preferred_element_type=F32)


def _dot_tn(a, b):
    return lax.dot_general(a, b, (((0,), (0,)), ((), ())), preferred_element_type=F32)


def _sigmoid(x):
    return 1.0 / (1.0 + jnp.exp(-x))


def _rms(x):
    return x * lax.rsqrt(jnp.mean(x * x, axis=-1, keepdims=True) + EPS)


def _modnorm(x, gain, shift, scale1):
    return (_rms(x) * gain) * scale1 + shift


def _mod_kernel(c_ref, w_ref, b_ref, o_ref):
    k = pl.program_id(0)
    c = c_ref[...]
    s = c * _sigmoid(c)
    acc = jnp.zeros(o_ref.shape, F32)
    w = w_ref[...]
    s_parts = _split3(s)
    w_parts = _split3(w)
    for i in range(3):
        for j in range(3 - i):
            acc = acc + _dot(s_parts[i], w_parts[j])
    val = acc + b_ref[...]
    add = jnp.where(k % 3 == 1, 1.0, 0.0).astype(F32)
    mul = jnp.where((k == 2) | (k == 8), 0.5, 1.0).astype(F32)
    o_ref[...] = (val + add) * mul


def _split3(x):
    hi = x.astype(BF16)
    r1 = x - hi.astype(F32)
    mid = r1.astype(BF16)
    lo = (r1 - mid.astype(F32)).astype(BF16)
    return hi, mid, lo


def _mod_table(c_rows, w_ada, b_ada):
    rows, d = c_rows.shape
    return pl.pallas_call(
        _mod_kernel,
        out_shape=jax.ShapeDtypeStruct((rows, N_MOD * d), F32),
        grid=(N_MOD,),
        in_specs=[
            pl.BlockSpec((rows, d), lambda k: (0, 0)),
            pl.BlockSpec((d, d), lambda k: (0, k)),
            pl.BlockSpec((1, d), lambda k: (0, k)),
        ],
        out_specs=pl.BlockSpec((rows, d), lambda k: (0, k)),
        compiler_params=_params(("arbitrary",)),
        name="mod_table",
    )(c_rows, w_ada, b_ada)


def _ffn_kernel(x_ref, mod_ref, g_ref, w1_ref, w3_ref, w2_ref, *rest, row0, f_chunk, n_lat_tiles, final):
    if final:
        gf_ref, o_ref = rest
        x = x_ref[0]
    else:
        c_ref, o_ref = rest
        x = jnp.where(pl.program_id(1) >= n_lat_tiles, c_ref[0], x_ref[0])
    mod = mod_ref[0, 0]
    shift, scale1, gate = mod[row0:row0 + 1], mod[row0 + 1:row0 + 2], mod[row0 + 2:row0 + 3]
    h = _modnorm(x, g_ref[...], shift, scale1).astype(BF16)
    acc = jnp.zeros(x.shape, F32)
    f_total = w1_ref.shape[1]
    for f0 in range(0, f_total, f_chunk):
        f1 = min(f0 + f_chunk, f_total)
        a = _dot(h, w1_ref[:, f0:f1])
        b = _dot(h, w3_ref[:, f0:f1])
        u = (a * _sigmoid(a) * b).astype(BF16)
        acc = acc + _dot(u, w2_ref[f0:f1, :])
    y = x + gate * acc
    if final:
        y = _rms(y) * gf_ref[...]
    o_ref[0] = y


def _ffn(x, modt, gain, w1, w3, w2, *, row0, n_lat_tiles, n_tiles, ctx_tile=None, final_gain=None):
    b_, _, d = x.shape
    f = w1.shape[1]
    tm = TOKEN_TILE
    final = final_gain is not None
    assert final != (ctx_tile is not None)
    f_chunk = -(-f // (2 * MXU_TILE)) * MXU_TILE
    kern = functools.partial(_ffn_kernel, row0=row0, f_chunk=f_chunk, n_lat_tiles=n_lat_tiles, final=final)
    in_specs = [
        pl.BlockSpec((1, tm, d), lambda b, j: (b, jnp.minimum(j, n_lat_tiles - 1), 0)),
        pl.BlockSpec((1, 1, MOD_ROWS, d), lambda b, j: (b, (j >= n_lat_tiles).astype(jnp.int32), 0, 0)),
        _resident((1, d)),
        _resident((d, f)),
        _resident((d, f)),
        _resident((f, d)),
    ]
    args = [x, modt, gain, w1, w3, w2]
    if final:
        in_specs.append(_resident((1, d)))
        args.append(final_gain)
    else:
        in_specs.append(pl.BlockSpec((1, tm, d), lambda b, j: (b, 0, 0)))
        args.append(ctx_tile)
    return pl.pallas_call(
        kern,
        out_shape=jax.ShapeDtypeStruct((b_, n_tiles * tm, d), F32),
        grid=(b_, n_tiles),
        in_specs=in_specs,
        out_specs=pl.BlockSpec((1, tm, d), lambda b, j: (b, j, 0)),
        compiler_params=_params(("parallel", "parallel")),
        name="ffn_final" if final else "ffn",
    )(*args)


def _run_pieces(pieces):
    pending = None
    for matmul, tail in pieces:
        result = matmul()
        if pending is not None:
            pending[0](pending[1])
        pending = (tail, result)
    pending[0](pending[1])


def _interleave(a, b):
    n = min(len(a), len(b))
    return [p for pair in zip(a, b) for p in pair] + a[n:] + b[n:]


def _column_pieces(lhs, w_ref, tail_of):
    def piece(cols):
        return (lambda: _dot(lhs, w_ref[:, cols])), tail_of(cols)
    return [piece(slice(c0, c0 + MXU_TILE)) for c0 in range(0, w_ref.shape[1], MXU_TILE)]


def _proj_gla_kernel(x_ref, mod_ref, g_ref, wq_ref, wk_ref, wv_ref, wg_ref, wlr_ref, w2_ref, b2_ref,
                     q_ref, k_ref, v_ref, sg_ref, lf_ref, *, q_scale):
    x = x_ref[0]
    mod = mod_ref[0, 0]
    h = _modnorm(x, g_ref[...], mod[3:4], mod[4:5]).astype(BF16)
    lr = _dot(h, wlr_ref[...]).astype(BF16)

    def decay_tail(cols):
        def tail(z):
            z = z + b2_ref[:, cols]
            lf_ref[0, :, cols] = (jnp.minimum(z, 0.0) - jnp.log(1.0 + jnp.exp(-jnp.abs(z)))) * (1.0 / GLA_TAU)
        return tail

    def gate_tail(cols):
        def tail(gg):
            sg_ref[0, :, cols] = (gg * _sigmoid(gg)).astype(BF16)
        return tail

    def value_tail(cols):
        def tail(v):
            v_ref[0, :, cols] = v.astype(BF16)
        return tail

    def query_tail(cols):
        def tail(q):
            q_ref[0, :, cols] = q * q_scale
        return tail

    def key_tail(cols):
        def tail(k):
            k_ref[0, :, cols] = k
        return tail

    heavy = _interleave(_column_pieces(lr, w2_ref, decay_tail), _column_pieces(h, wg_ref, gate_tail))
    light = (_column_pieces(h, wv_ref, value_tail) + _column_pieces(h, wq_ref, query_tail)
             + _column_pieces(h, wk_ref, key_tail))
    _run_pieces(_interleave(heavy, light))


def _proj_gla(xs, modt, gain, wq, wk, wv, wg, wlr, w2cat, b2cat, *, n_lat_tiles, n_tiles):
    b_, lp, d = xs.shape
    tm = TOKEN_TILE
    qk = wq.shape[1]
    vv = wv.shape[1]
    row = lambda width: pl.BlockSpec((1, tm, width), lambda b, j: (b, j, 0))
    kern = functools.partial(_proj_gla_kernel, q_scale=(qk // GLA_HEADS) ** -0.5)
    return pl.pallas_call(
        kern,
        out_shape=(
            jax.ShapeDtypeStruct((b_, lp, qk), F32),
            jax.ShapeDtypeStruct((b_, lp, qk), F32),
            jax.ShapeDtypeStruct((b_, lp, vv), BF16),
            jax.ShapeDtypeStruct((b_, lp, vv), BF16),
            jax.ShapeDtypeStruct((b_, lp, 2 * qk), F32),
        ),
        grid=(b_, n_tiles),
        in_specs=[
            row(d),
            pl.BlockSpec((1, 1, MOD_ROWS, d), lambda b, j: (b, (j >= n_lat_tiles).astype(jnp.int32), 0, 0)),
            _resident((1, d)),
            _resident(wq.shape), _resident(wk.shape), _resident(wv.shape), _resident(wg.shape),
            _resident(wlr.shape), _resident(w2cat.shape), _resident(b2cat.shape),
        ],
        out_specs=(row(qk), row(qk), row(vv), row(vv), row(2 * qk)),
        compiler_params=_params(("parallel", "parallel")),
        name="proj_gla",
    )(xs, modt, gain, wq, wk, wv, wg, wlr, w2cat, b2cat)


def _rope(u, cos, sin_signed, first_half):
    out = []
    for g in range(u.shape[1] // LANES):
        ug = u[:, g * LANES:(g + 1) * LANES]
        partner = jnp.where(first_half, pltpu.roll(ug, LANES - 16, 1), pltpu.roll(ug, 16, 1))
        out.append(ug * cos + partner * sin_signed)
    return jnp.concatenate(out, axis=1)


def _proj_dif_kernel(x_ref, mod_ref, g_ref, cos_ref, sin_ref, tab_t_ref, wq_t_ref, wk_ref, wv_t_ref, wm_ref,
                     q_t_ref, k_ref, v_t_ref, sm_ref, *, q_scale):
    x = x_ref[0]
    mod = mod_ref[0, 0]
    h = _modnorm(x, g_ref[...], mod[3:4], mod[4:5]).astype(BF16)
    quarter = DIFF_DH // 4
    tab = tab_t_ref[...]
    cos = cos_ref[...]
    sin_signed = sin_ref[...]
    lane = lax.broadcasted_iota(jnp.int32, cos.shape, 1)
    first_half = (lane % 32) < 16

    t_rows = 2 * MXU_TILE

    def row_pieces(w_t_ref, tail_of):
        def piece(r0):
            return (lambda: _dot_nt(w_t_ref[r0:r0 + t_rows, :], h)), tail_of(r0)
        return [piece(r0) for r0 in range(0, w_t_ref.shape[0], t_rows)]

    def query_tail(r0):
        def tail(q_t):
            for base in range(0, t_rows, 2 * quarter):
                t0 = base % DIFF_DH
                c, s = tab[t0:t0 + quarter], tab[t0 + quarter:t0 + 2 * quarter]
                u1 = q_t[base:base + quarter]
                u2 = q_t[base + quarter:base + 2 * quarter]
                q_t_ref[0, r0 + base:r0 + base + quarter, :] = ((u1 * c - u2 * s) * q_scale).astype(BF16)
                q_t_ref[0, r0 + base + quarter:r0 + base + 2 * quarter, :] = (
                    (u2 * c + u1 * s) * q_scale).astype(BF16)
        return tail

    def key_tail(cols):
        def tail(k):
            k_ref[0, :, cols] = _rope(k, cos, sin_signed, first_half).astype(BF16)
        return tail

    def value_tail(r0):
        def tail(v_t):
            v_t_ref[0, r0:r0 + t_rows, :] = v_t.astype(BF16)
        return tail

    def gate_tail(cols):
        def tail(mg):
            sm_ref[0, :, cols] = _sigmoid(mg).astype(BF16)
        return tail

    rotary = _interleave(row_pieces(wq_t_ref, query_tail), _column_pieces(h, wk_ref, key_tail))
    plain = _interleave(_column_pieces(h, wm_ref, gate_tail), row_pieces(wv_t_ref, value_tail))
    _run_pieces(_interleave(rotary, plain))


def _proj_dif(xs, modt, gain, cos_t, sin_t, tab_t, wq_t, wk, wv_t, wm, *, n_lat_tiles, n_tiles):
    b_, lp, d = xs.shape
    tm = TOKEN_TILE
    row = lambda width: pl.BlockSpec((1, tm, width), lambda b, j: (b, j, 0))
    col = lambda height: pl.BlockSpec((1, height, tm), lambda b, j: (b, 0, j))
    tab = pl.BlockSpec((tm, LANES), lambda b, j: (j, 0))
    kern = functools.partial(_proj_dif_kernel, q_scale=DIFF_DH ** -0.5 * LOG2_E)
    return pl.pallas_call(
        kern,
        out_shape=(
            jax.ShapeDtypeStruct((b_, wq_t.shape[0], lp), BF16),
            jax.ShapeDtypeStruct((b_, lp, wk.shape[1]), BF16),
            jax.ShapeDtypeStruct((b_, wv_t.shape[0], lp), BF16),
            jax.ShapeDtypeStruct((b_, lp, wm.shape[1]), BF16),
        ),
        grid=(b_, n_tiles),
        in_specs=[
            row(d),
            pl.BlockSpec((1, 1, MOD_ROWS, d), lambda b, j: (b, (j >= n_lat_tiles).astype(jnp.int32), 0, 0)),
            _resident((1, d)),
            tab, tab,
            pl.BlockSpec((DIFF_DH, tm), lambda b, j: (0, j)),
            _resident(wq_t.shape), _resident(wk.shape), _resident(wv_t.shape), _resident(wm.shape),
        ],
        out_specs=(col(wq_t.shape[0]), row(wk.shape[1]), col(wv_t.shape[0]), row(wm.shape[1])),
        compiler_params=_params(("parallel", "parallel")),
        name="proj_dif",
    )(xs, modt, gain, cos_t, sin_t, tab_t, wq_t, wk, wv_t, wm)


def _gla_decays(q, k, logf, tri, ref_row, end_row):
    c = GLA_CHUNK
    hi, lo, _ = _split3(logf)
    cum = _dot(tri, hi) + _dot(tri, lo)
    n_chunks = q.shape[0] // c
    totals = [cum[a * c + end_row:a * c + end_row + 1] for a in range(n_chunks)]
    per_chunk = lambda rows: jnp.concatenate([jnp.broadcast_to(r, (c, r.shape[1])) for r in rows], axis=0)
    ref = per_chunk([cum[a * c + ref_row:a * c + ref_row + 1] for a in range(n_chunks)])
    total = per_chunk(totals)
    q_state = (q * jnp.exp(cum)).astype(BF16)
    q_in = (q * jnp.exp(cum - ref)).astype(BF16)
    k_in = (k * jnp.exp(ref - cum)).astype(BF16)
    k_out = (k * jnp.exp(total - cum)).astype(BF16)
    return q_state, q_in, k_in, k_out, [jnp.exp(t) for t in totals]


def _gla_kernel(qf_ref, kf_ref, vf_ref, lf_ref, qb_ref, kb_ref, vb_ref, lb_ref,
                of_ref, ob_ref, sf_ref, sb_ref):
    @pl.when(pl.program_id(1) == 0)
    def _():
        sf_ref[...] = jnp.zeros(sf_ref.shape, F32)
        sb_ref[...] = jnp.zeros(sb_ref.shape, F32)

    c = GLA_CHUNK
    rows = qf_ref.shape[1]
    n_chunks = rows // c
    row = lax.broadcasted_iota(jnp.int32, (rows, rows), 0)
    col = lax.broadcasted_iota(jnp.int32, (rows, rows), 1)
    same_chunk = (row // c) == (col // c)
    lower = same_chunk & (col <= row)
    upper = same_chunk & (col >= row)
    dirs = [
        (_gla_decays(qf_ref[0], kf_ref[0], lf_ref[0], jnp.where(lower, 1.0, 0.0).astype(BF16),
                     c // 2 - 1, c - 1), vf_ref[0], lower, sf_ref, of_ref, list(range(n_chunks))),
        (_gla_decays(qb_ref[0], kb_ref[0], lb_ref[0], jnp.where(upper, 1.0, 0.0).astype(BF16),
                     c // 2, 0), vb_ref[0], upper, sb_ref, ob_ref, list(range(n_chunks))[::-1]),
    ]
    dk = qf_ref.shape[2] // GLA_HEADS
    dv = vf_ref.shape[2] // GLA_HEADS
    chains = []
    for (q_state, q_in, k_in, k_out, decays), v, keep, s_ref, o_ref, order in dirs:
        for hd in range(GLA_HEADS):
            ks = slice(hd * dk, (hd + 1) * dk)
            vs = slice(hd * dv, (hd + 1) * dv)
            chains.append(dict(q_state=q_state[:, ks], q_in=q_in[:, ks], k_in=k_in[:, ks], k_out=k_out[:, ks],
                               decays=[d[:, ks] for d in decays], v=v[:, vs], keep=keep, s_ref=s_ref,
                               o_ref=o_ref, hd=hd, vs=vs, order=order))
    chunk = lambda x, a: x[a * c:(a + 1) * c]
    for ch in chains:
        ch["attn"] = _dot_nt(ch["q_in"], ch["k_in"])
    for ch in chains:
        ch["update"] = [_dot_tn(chunk(ch["v"], a), chunk(ch["k_out"], a)) for a in range(n_chunks)]
    for ch in chains:
        state_t = ch["s_ref"][ch["hd"]]
        ch["before"] = {}
        for a in ch["order"]:
            ch["before"][a] = state_t
            state_t = state_t * ch["decays"][a] + ch["update"][a]
        ch["s_ref"][ch["hd"]] = state_t
    for ch in chains:
        ch["inter"] = jnp.concatenate(
            [_dot_nt(chunk(ch["q_state"], a), ch["before"][a].astype(BF16)) for a in range(n_chunks)], axis=0)
    for ch in chains:
        attn = jnp.where(ch["keep"], ch["attn"], 0.0).astype(BF16)
        ch["o_ref"][0, :, ch["vs"]] = (ch["inter"] + _dot(attn, ch["v"])).astype(BF16)


def _gla(gq, gk, gv, logf, *, n_lat_blocks, n_ctx_blocks):
    b_, lp, qk = gq.shape
    vv = gv.shape[2]
    c = GLA_BLOCK
    n = n_lat_blocks + n_ctx_blocks
    fwd = lambda b, s: (b, (s + n_lat_blocks) % n, 0)
    bwd = lambda b, s: (b, n - 1 - s, 0)
    lf_spec = pl.BlockSpec((1, c, qk), lambda b, s: (b, (s + n_lat_blocks) % n, 0))
    lb_spec = pl.BlockSpec((1, c, qk), lambda b, s: (b, n - 1 - s, 1))
    return pl.pallas_call(
        _gla_kernel,
        out_shape=(jax.ShapeDtypeStruct((b_, n * c, vv), BF16), jax.ShapeDtypeStruct((b_, n * c, vv), BF16)),
        grid=(b_, n),
        in_specs=[
            pl.BlockSpec((1, c, qk), fwd), pl.BlockSpec((1, c, qk), fwd), pl.BlockSpec((1, c, vv), fwd), lf_spec,
            pl.BlockSpec((1, c, qk), bwd), pl.BlockSpec((1, c, qk), bwd), pl.BlockSpec((1, c, vv), bwd), lb_spec,
        ],
        out_specs=(pl.BlockSpec((1, c, vv), fwd), pl.BlockSpec((1, c, vv), bwd)),
        scratch_shapes=[
            pltpu.VMEM((GLA_HEADS, vv // GLA_HEADS, qk // GLA_HEADS), F32),
            pltpu.VMEM((GLA_HEADS, vv // GLA_HEADS, qk // GLA_HEADS), F32),
        ],
        compiler_params=_params(("parallel", "arbitrary")),
        name="gla_scan",
    )(gq, gk, gv, logf, gq, gk, gv, logf)


def _attn_kernel(q_t_ref, k_ref, v_t_ref, lam_ref, o_ref, m_ref, acc_ref, s_ref, bm_ref,
                 *, tk, n_kblocks, lam_init):
    tq = q_t_ref.shape[2]
    q_t = q_t_ref[0]
    row = lax.broadcasted_iota(jnp.int32, q_t.shape, 0)
    zero = jnp.zeros_like(q_t)
    qz_t = jnp.concatenate([jnp.where(row < DIFF_DH, q_t, zero), jnp.where(row >= DIFF_DH, q_t, zero)], axis=1)
    m_ref[...] = jnp.full(m_ref.shape, -jnp.inf, F32)
    acc_ref[...] = jnp.zeros(acc_ref.shape, F32)

    groups = list(range(2 * tq // ATTN_QGROUP))
    lanes = lambda g: slice(g * ATTN_QGROUP, (g + 1) * ATTN_QGROUP)
    slab = lambda g: slice(g * tk, (g + 1) * tk)

    def scores(kb, slot, g):
        s_t = _dot(kb, qz_t[:, lanes(g)])
        s_ref[slot, slab(g)] = s_t
        bm_ref[slot, g] = jnp.max(s_t, axis=0, keepdims=True)

    ones_rows = jnp.ones((BF16_ROWS, tk), BF16)

    def softmax_values(vb_t, slot, g):
        s_t = s_ref[slot, slab(g)]
        m_prev = m_ref[g]
        m_new = jnp.maximum(m_prev, bm_ref[slot, g])
        m_ref[g] = m_new
        alpha = jnp.exp2(m_prev - m_new)
        p_t = jnp.exp2(s_t - m_new).astype(BF16)
        acc_ref[g] = alpha * acc_ref[g] + _dot(vb_t, p_t)

    def load_keys(j):
        return k_ref[0, pl.ds(pl.multiple_of(j * tk, tk), tk), :]

    def load_values(j):
        vb_t = v_t_ref[0, :, pl.ds(pl.multiple_of(j * tk, tk), tk)]
        return jnp.concatenate([vb_t, ones_rows], axis=0)

    def stage(j, slot, last=False):
        kb = None if last else load_keys(j + 1)
        vb_t = load_values(j)
        for g in groups:
            if not last:
                scores(kb, 1 - slot, g)
            softmax_values(vb_t, slot, g)

    kb0 = load_keys(0)
    for g in groups:
        scores(kb0, 0, g)
    n_pairs = max(n_kblocks - 1, 0) // 2

    def pair(i, carry):
        stage(2 * i, 0)
        stage(2 * i + 1, 1)
        return carry

    lax.fori_loop(0, n_pairs, pair, 0)
    for j in range(2 * n_pairs, n_kblocks):
        stage(j, j % 2, last=j == n_kblocks - 1)

    lp = lam_ref[...]
    lam = (jnp.exp(jnp.sum(lp[0:1] * lp[1:2], axis=-1, keepdims=True))
           - jnp.exp(jnp.sum(lp[2:3] * lp[3:4], axis=-1, keepdims=True)) + lam_init)
    dv = v_t_ref.shape[1]
    acc = jnp.concatenate([acc_ref[g] for g in groups], axis=1)
    o_t = acc[:dv] / acc[dv:dv + 1]
    o_ref[0] = (o_t[:, :tq] - lam * o_t[:, tq:]).T.astype(BF16)


def _attn(dq_t, dk, dv_t, lam_p, *, seq, n_keys, lam_init):
    b_, lp, width = dk.shape
    tq, tk = ATTN_TQ, ATTN_TK
    heads = width // LANES
    n_groups = 2 * tq // ATTN_QGROUP
    kern = functools.partial(_attn_kernel, tk=tk, n_kblocks=n_keys // tk, lam_init=lam_init)
    return pl.pallas_call(
        kern,
        out_shape=jax.ShapeDtypeStruct((b_, seq, width), BF16),
        grid=(b_, heads, seq // tq),
        in_specs=[
            pl.BlockSpec((1, LANES, tq), lambda b, h, i: (b, h, i)),
            pl.BlockSpec((1, lp, LANES), lambda b, h, i: (b, 0, h)),
            pl.BlockSpec((1, LANES, lp), lambda b, h, i: (b, h, 0)),
            pl.BlockSpec(lam_p.shape, lambda b, h, i: (0, 0)),
        ],
        out_specs=pl.BlockSpec((1, tq, LANES), lambda b, h, i: (b, i, h)),
        scratch_shapes=[
            pltpu.VMEM((n_groups, 1, ATTN_QGROUP), F32),
            pltpu.VMEM((n_groups, LANES + BF16_ROWS, ATTN_QGROUP), F32),
            pltpu.VMEM((2, n_groups * tk + SCORE_PAD_ROWS, ATTN_QGROUP), F32),
            pltpu.VMEM((2, n_groups, 1, ATTN_QGROUP), F32),
        ],
        compiler_params=_params(("parallel", "parallel", "parallel")),
        name="diff_attn",
    )(dq_t, dk, dv_t, lam_p)


def _head_rms(x, gain, width):
    parts = []
    for h0 in range(0, x.shape[1], width):
        parts.append(_rms(x[:, h0:h0 + width]))
    return jnp.concatenate(parts, axis=1) * gain


def _merge_kernel(x_ref, mod_ref, of_ref, ob_ref, sg_ref, od_ref, sm_ref, gg_ref, gd_ref,
                  wa_ref, wb_ref, wo_ref, o_ref, *, dv_gla, dv_diff, diff_scale):
    d = x_ref.shape[2]
    o_gla = of_ref[0].astype(F32) + ob_ref[0].astype(F32)
    ya = _head_rms(o_gla, gg_ref[...], dv_gla) * sg_ref[0].astype(F32)
    ya = _dot(ya.astype(BF16), wa_ref[...])
    yb = _head_rms(od_ref[0].astype(F32), gd_ref[...], dv_diff) * diff_scale
    yb = _dot(yb.astype(BF16), wb_ref[...])
    sm = sm_ref[0].astype(F32)
    merged = (sm[:, :d] * ya + sm[:, d:] * yb).astype(BF16)
    gate = mod_ref[0, 0][5:6]
    o_ref[0] = x_ref[0] + gate * _dot(merged, wo_ref[...])


def _merge(x1, modt, o_f, o_b, sgg, o_d, smg, g_gla, g_diff, wa, wb, wo, *, n_lat_tiles, diff_scale):
    b_, _, d = x1.shape
    tm = TOKEN_TILE
    row = lambda width: pl.BlockSpec((1, tm, width), lambda b, j: (b, j, 0))
    kern = functools.partial(_merge_kernel, dv_gla=o_f.shape[2] // GLA_HEADS,
                             dv_diff=o_d.shape[2] // DIFF_HEADS, diff_scale=diff_scale)
    return pl.pallas_call(
        kern,
        out_shape=jax.ShapeDtypeStruct((b_, n_lat_tiles * tm, d), F32),
        grid=(b_, n_lat_tiles),
        in_specs=[
            row(d),
            pl.BlockSpec((1, 1, MOD_ROWS, d), lambda b, j: (b, 0, 0, 0)),
            row(o_f.shape[2]), row(o_b.shape[2]), row(sgg.shape[2]), row(o_d.shape[2]), row(smg.shape[2]),
            _resident(g_gla.shape), _resident(g_diff.shape),
            _resident(wa.shape), _resident(wb.shape), _resident(wo.shape),
        ],
        out_specs=row(d),
        compiler_params=_params(("parallel", "parallel")),
        name="merge",
    )(x1, modt, o_f, o_b, sgg, o_d, smg, g_gla, g_diff, wa, wb, wo)


def _rope_tables(seq, lp):
    half = DIFF_DH // 2
    quarter = half // 2
    inv = ROPE_BASE ** (-np.arange(quarter, dtype=np.float32) / quarter)
    t = jnp.arange(seq, dtype=jnp.int32)
    ang_row = (t // GRID_W).astype(F32)[:, None] * inv[None, :]
    ang_col = (t % GRID_W).astype(F32)[:, None] * inv[None, :]
    ang = jnp.concatenate([ang_row, ang_row, ang_col, ang_col], axis=1)
    sign = np.tile(np.concatenate([-np.ones(quarter), np.ones(quarter)]), 2).astype(np.float32)
    cos = jnp.tile(jnp.cos(ang), (1, LANES // DIFF_DH))
    sin = jnp.tile(jnp.sin(ang) * sign[None, :], (1, LANES // DIFF_DH))
    pad = lp - seq
    cos = jnp.concatenate([cos, jnp.ones((pad, LANES), F32)], axis=0)
    sin = jnp.concatenate([sin, jnp.zeros((pad, LANES), F32)], axis=0)
    tab_t = jnp.concatenate([jnp.cos(ang_row), jnp.sin(ang_row), jnp.cos(ang_col), jnp.sin(ang_col)], axis=1).T
    ident = np.repeat(np.array([1.0, 0.0, 1.0, 0.0], np.float32), quarter)[:, None]
    tab_t = jnp.concatenate([tab_t, jnp.broadcast_to(ident, (DIFF_DH, pad))], axis=1)
    return cos, sin, tab_t


def kernel(x, c, ctx, c_ctx, w_ada, b_ada, ffn1_norm, ffn1_w1, ffn1_w3, ffn1_w2, mix_norm, w_in, gla_gate_w2, gla_gate_b, gla_out_norm, diff_lambda, diff_out_norm, w_branch_gla, w_branch_diff, w_out, ffn2_norm, ffn2_w1, ffn2_w3, ffn2_w2, final_norm):
    b_, seq, d = x.shape
    n_ctx = ctx.shape[1]
    assert w_ada.shape[0] == 1, "single-layer stack"
    tm = TOKEN_TILE
    assert seq % tm == 0 and seq % ATTN_TQ == 0 and n_ctx <= tm
    assert seq % GLA_BLOCK == 0 and n_ctx % GLA_BLOCK == 0 and (seq + n_ctx) % ATTN_TK == 0
    n_lat_tiles = seq // tm
    n_tiles = n_lat_tiles + 1
    lp = n_tiles * tm
    lam_init = 0.8 - 0.6 * math.exp(0.0)

    gla_qk = GLA_HEADS * (d // 8)
    gla_v = GLA_HEADS * (d // 4)
    dif = DIFF_HEADS * 2 * DIFF_DH
    splits = np.cumsum([gla_qk, gla_qk, gla_v, gla_v, 2 * GLA_RANK, dif, dif, dif, 2 * d])
    assert w_in.shape[2] == splits[-1]
    wi = w_in[0].astype(BF16)
    w_gq, w_gk, w_gv, w_gg, w_lr, w_dq, w_dk, w_dv, w_mg = jnp.split(wi, splits[:-1].tolist(), axis=1)
    w_lr = jnp.pad(w_lr, ((0, 0), (0, LANES - 2 * GLA_RANK)))
    w2cat = jnp.zeros((LANES, 2 * gla_qk), F32)
    w2cat = w2cat.at[:GLA_RANK, :gla_qk].set(gla_gate_w2[0, 0])
    w2cat = w2cat.at[GLA_RANK:2 * GLA_RANK, gla_qk:].set(gla_gate_w2[0, 1]).astype(BF16)
    b2cat = gla_gate_b[0].reshape(1, 2 * gla_qk)

    rows = -(-(b_ + 1) // 8) * 8
    c_rows = jnp.concatenate([c, c_ctx[None, :], jnp.zeros((rows - b_ - 1, d), F32)], axis=0)
    mods = _mod_table(c_rows, w_ada[0], b_ada).reshape(rows, N_MOD, d)
    modt = jnp.stack([mods[:b_], jnp.broadcast_to(mods[b_], (b_, N_MOD, d))], axis=1)
    modt = jnp.pad(modt, ((0, 0), (0, 0), (0, MOD_ROWS - N_MOD), (0, 0)))

    ctx_tile = jnp.pad(ctx, ((0, 0), (0, tm - n_ctx), (0, 0)))
    tiles = dict(n_lat_tiles=n_lat_tiles, n_tiles=n_tiles)

    x1 = _ffn(x, modt, ffn1_norm, ffn1_w1[0].astype(BF16), ffn1_w3[0].astype(BF16),
              ffn1_w2[0].astype(BF16), row0=0, ctx_tile=ctx_tile, **tiles)

    gq, gk, gv, sgg, logf = _proj_gla(x1, modt, mix_norm, w_gq, w_gk, w_gv, w_gg, w_lr, w2cat, b2cat, **tiles)
    cos_t, sin_t, tab_t = _rope_tables(seq, lp)
    dq, dk, dv, smg = _proj_dif(x1, modt, mix_norm, cos_t, sin_t, tab_t, w_dq.T, w_dk, w_dv.T, w_mg, **tiles)

    o_f, o_b = _gla(gq, gk, gv, logf, n_lat_blocks=seq // GLA_BLOCK, n_ctx_blocks=n_ctx // GLA_BLOCK)

    lam_p = jnp.pad(diff_lambda[0], ((0, 4), (0, LANES - DIFF_DH)))
    o_d = _attn(dq, dk, dv, lam_p, seq=seq, n_keys=seq + n_ctx, lam_init=lam_init)

    x2 = _merge(x1, modt, o_f, o_b, sgg, o_d, smg, gla_out_norm, diff_out_norm,
                w_branch_gla[0].astype(BF16), w_branch_diff[0].astype(BF16), w_out[0].astype(BF16),
                n_lat_tiles=n_lat_tiles, diff_scale=1.0 - lam_init)

    return _ffn(x2, modt, ffn2_norm, ffn2_w1[0].astype(BF16), ffn2_w3[0].astype(BF16),
                ffn2_w2[0].astype(BF16), row0=6, n_lat_tiles=n_lat_tiles, n_tiles=n_lat_tiles,
                final_gain=final_norm.reshape(1, d))
```

```python
import functools
import math

import jax
import jax.numpy as jnp
import numpy as np
from jax import lax
from jax.experimental import pallas as pl
from jax.experimental.pallas import tpu as pltpu

F32 = jnp.float32
BF16 = jnp.bfloat16

EPS = 1e-6
GRID_W = 64
N_MOD = 9
MOD_ROWS = 16
GLA_HEADS = 4
GLA_RANK = 16
GLA_TAU = 16.0
GLA_CHUNK = 64
GLA_BLOCK = 256
DIFF_HEADS = 8
DIFF_DH = 64
ROPE_BASE = 10000.0
LANES = 128
MXU_TILE = 256
BF16_ROWS = 16
LOG2_E = 1.4426950408889634

TOKEN_TILE = 512
ATTN_TQ = 2048
ATTN_TK = 768
ATTN_QGROUP = 256
VMEM_LIMIT = 52 * 1024 * 1024


def _resident(shape):
    zeros = (0,) * len(shape)
    return pl.BlockSpec(shape, lambda *_: zeros, pipeline_mode=pl.Buffered(1))


def _params(semantics):
    return pltpu.CompilerParams(dimension_semantics=semantics, vmem_limit_bytes=VMEM_LIMIT)


def _dot(a, b):
    return jnp.dot(a, b, preferred_element_type=F32)


def _dot_nt(a, b):
    return lax.dot_general(a, b, (((1,), (1,)), ((), ())), preferred_element_type=F32)


def _dot_tn(a, b):
    return lax.dot_general(a, b, (((0,), (0,)), ((), ())), preferred_element_type=F32)


def _sigmoid(x):
    return 1.0 / (1.0 + jnp.exp(-x))


def _rms(x):
    return x * lax.rsqrt(jnp.mean(x * x, axis=-1, keepdims=True) + EPS)


def _modnorm(x, gain, shift, scale1):
    return (_rms(x) * gain) * scale1 + shift


def _mod_kernel(c_ref, w_ref, b_ref, o_ref):
    k = pl.program_id(0)
    c = c_ref[...]
    s = c * _sigmoid(c)
    acc = jnp.zeros(o_ref.shape, F32)
    w = w_ref[...]
    s_parts = _split3(s)
    w_parts = _split3(w)
    for i in range(3):
        for j in range(3 - i):
            acc = acc + _dot(s_parts[i], w_parts[j])
    val = acc + b_ref[...]
    add = jnp.where(k % 3 == 1, 1.0, 0.0).astype(F32)
    mul = jnp.where((k == 2) | (k == 8), 0.5, 1.0).astype(F32)
    o_ref[...] = (val + add) * mul


def _split3(x):
    hi = x.astype(BF16)
    r1 = x - hi.astype(F32)
    mid = r1.astype(BF16)
    lo = (r1 - mid.astype(F32)).astype(BF16)
    return hi, mid, lo


def _mod_table(c_rows, w_ada, b_ada):
    rows, d = c_rows.shape
    return pl.pallas_call(
        _mod_kernel,
        out_shape=jax.ShapeDtypeStruct((rows, N_MOD * d), F32),
        grid=(N_MOD,),
        in_specs=[
            pl.BlockSpec((rows, d), lambda k: (0, 0)),
            pl.BlockSpec((d, d), lambda k: (0, k)),
            pl.BlockSpec((1, d), lambda k: (0, k)),
        ],
        out_specs=pl.BlockSpec((rows, d), lambda k: (0, k)),
        compiler_params=_params(("arbitrary",)),
        name="mod_table",
    )(c_rows, w_ada, b_ada)


def _ffn_kernel(x_ref, mod_ref, g_ref, w1_ref, w3_ref, w2_ref, *rest, row0, f_chunk, n_lat_tiles, final):
    if final:
        gf_ref, o_ref = rest
        x = x_ref[0]
    else:
        c_ref, o_ref = rest
        x = jnp.where(pl.program_id(1) >= n_lat_tiles, c_ref[0], x_ref[0])
    mod = mod_ref[0, 0]
    shift, scale1, gate = mod[row0:row0 + 1], mod[row0 + 1:row0 + 2], mod[row0 + 2:row0 + 3]
    h = _modnorm(x, g_ref[...], shift, scale1).astype(BF16)
    acc = jnp.zeros(x.shape, F32)
    f_total = w1_ref.shape[1]
    for f0 in range(0, f_total, f_chunk):
        f1 = min(f0 + f_chunk, f_total)
        a = _dot(h, w1_ref[:, f0:f1])
        b = _dot(h, w3_ref[:, f0:f1])
        u = (a * _sigmoid(a) * b).astype(BF16)
        acc = acc + _dot(u, w2_ref[f0:f1, :])
    y = x + gate * acc
    if final:
        y = _rms(y) * gf_ref[...]
    o_ref[0] = y


def _ffn(x, modt, gain, w1, w3, w2, *, row0, n_lat_tiles, n_tiles, ctx_tile=None, final_gain=None):
    b_, _, d = x.shape
    f = w1.shape[1]
    tm = TOKEN_TILE
    final = final_gain is not None
    assert final != (ctx_tile is not None)
    f_chunk = -(-f // (2 * MXU_TILE)) * MXU_TILE
    kern = functools.partial(_ffn_kernel, row0=row0, f_chunk=f_chunk, n_lat_tiles=n_lat_tiles, final=final)
    in_specs = [
        pl.BlockSpec((1, tm, d), lambda b, j: (b, jnp.minimum(j, n_lat_tiles - 1), 0)),
        pl.BlockSpec((1, 1, MOD_ROWS, d), lambda b, j: (b, (j >= n_lat_tiles).astype(jnp.int32), 0, 0)),
        _resident((1, d)),
        _resident((d, f)),
        _resident((d, f)),
        _resident((f, d)),
    ]
    args = [x, modt, gain, w1, w3, w2]
    if final:
        in_specs.append(_resident((1, d)))
        args.append(final_gain)
    else:
        in_specs.append(pl.BlockSpec((1, tm, d), lambda b, j: (b, 0, 0)))
        args.append(ctx_tile)
    return pl.pallas_call(
        kern,
        out_shape=jax.ShapeDtypeStruct((b_, n_tiles * tm, d), F32),
        grid=(b_, n_tiles),
        in_specs=in_specs,
        out_specs=pl.BlockSpec((1, tm, d), lambda b, j: (b, j, 0)),
        compiler_params=_params(("parallel", "parallel")),
        name="ffn_final" if final else "ffn",
    )(*args)


def _run_pieces(pieces):
    pending = None
    for matmul, tail in pieces:
        result = matmul()
        if pending is not None:
            pending[0](pending[1])
        pending = (tail, result)
    pending[0](pending[1])


def _interleave(a, b):
    n = min(len(a), len(b))
    return [p for pair in zip(a, b) for p in pair] + a[n:] + b[n:]


def _column_pieces(lhs, w_ref, tail_of):
    def piece(cols):
        return (lambda: _dot(lhs, w_ref[:, cols])), tail_of(cols)
    return [piece(slice(c0, c0 + MXU_TILE)) for c0 in range(0, w_ref.shape[1], MXU_TILE)]


def _proj_gla_kernel(x_ref, mod_ref, g_ref, wq_ref, wk_ref, wv_ref, wg_ref, wlr_ref, w2_ref, b2_ref,
                     q_ref, k_ref, v_ref, sg_ref, lf_ref, *, q_scale):
    x = x_ref[0]
    mod = mod_ref[0, 0]
    h = _modnorm(x, g_ref[...], mod[3:4], mod[4:5]).astype(BF16)
    lr = _dot(h, wlr_ref[...]).astype(BF16)

    def decay_tail(cols):
        def tail(z):
            z = z + b2_ref[:, cols]
            lf_ref[0, :, cols] = (jnp.minimum(z, 0.0) - jnp.log(1.0 + jnp.exp(-jnp.abs(z)))) * (1.0 / GLA_TAU)
        return tail

    def gate_tail(cols):
        def tail(gg):
            sg_ref[0, :, cols] = (gg * _sigmoid(gg)).astype(BF16)
        return tail

    def value_tail(cols):
        def tail(v):
            v_ref[0, :, cols] = v.astype(BF16)
        return tail

    def query_tail(cols):
        def tail(q):
            q_ref[0, :, cols] = q * q_scale
        return tail

    def key_tail(cols):
        def tail(k):
            k_ref[0, :, cols] = k
        return tail

    heavy = _interleave(_column_pieces(lr, w2_ref, decay_tail), _column_pieces(h, wg_ref, gate_tail))
    light = (_column_pieces(h, wv_ref, value_tail) + _column_pieces(h, wq_ref, query_tail)
             + _column_pieces(h, wk_ref, key_tail))
    _run_pieces(_interleave(heavy, light))


def _proj_gla(xs, modt, gain, wq, wk, wv, wg, wlr, w2cat, b2cat, *, n_lat_tiles, n_tiles):
    b_, lp, d = xs.shape
    tm = TOKEN_TILE
    qk = wq.shape[1]
    vv = wv.shape[1]
    row = lambda width: pl.BlockSpec((1, tm, width), lambda b, j: (b, j, 0))
    kern = functools.partial(_proj_gla_kernel, q_scale=(qk // GLA_HEADS) ** -0.5)
    return pl.pallas_call(
        kern,
        out_shape=(
            jax.ShapeDtypeStruct((b_, lp, qk), F32),
            jax.ShapeDtypeStruct((b_, lp, qk), F32),
            jax.ShapeDtypeStruct((b_, lp, vv), BF16),
            jax.ShapeDtypeStruct((b_, lp, vv), BF16),
            jax.ShapeDtypeStruct((b_, lp, 2 * qk), F32),
        ),
        grid=(b_, n_tiles),
        in_specs=[
            row(d),
            pl.BlockSpec((1, 1, MOD_ROWS, d), lambda b, j: (b, (j >= n_lat_tiles).astype(jnp.int32), 0, 0)),
            _resident((1, d)),
            _resident(wq.shape), _resident(wk.shape), _resident(wv.shape), _resident(wg.shape),
            _resident(wlr.shape), _resident(w2cat.shape), _resident(b2cat.shape),
        ],
        out_specs=(row(qk), row(qk), row(vv), row(vv), row(2 * qk)),
        compiler_params=_params(("parallel", "parallel")),
        name="proj_gla",
    )(xs, modt, gain, wq, wk, wv, wg, wlr, w2cat, b2cat)


def _rope(u, cos, sin_signed, first_half):
    out = []
    for g in range(u.shape[1] // LANES):
        ug = u[:, g * LANES:(g + 1) * LANES]
        partner = jnp.where(first_half, pltpu.roll(ug, LANES - 16, 1), pltpu.roll(ug, 16, 1))
        out.append(ug * cos + partner * sin_signed)
    return jnp.concatenate(out, axis=1)


def _proj_dif_kernel(x_ref, mod_ref, g_ref, cos_ref, sin_ref, tab_t_ref, wq_t_ref, wk_ref, wv_t_ref, wm_ref,
                     q_t_ref, k_ref, v_t_ref, sm_ref, *, q_scale):
    x = x_ref[0]
    mod = mod_ref[0, 0]
    h = _modnorm(x, g_ref[...], mod[3:4], mod[4:5]).astype(BF16)
    quarter = DIFF_DH // 4
    tab = tab_t_ref[...]
    cos = cos_ref[...]
    sin_signed = sin_ref[...]
    lane = lax.broadcasted_iota(jnp.int32, cos.shape, 1)
    first_half = (lane % 32) < 16

    t_rows = 2 * MXU_TILE

    def row_pieces(w_t_ref, tail_of):
        def piece(r0):
            return (lambda: _dot_nt(w_t_ref[r0:r0 + t_rows, :], h)), tail_of(r0)
        return [piece(r0) for r0 in range(0, w_t_ref.shape[0], t_rows)]

    def query_tail(r0):
        def tail(q_t):
            for base in range(0, t_rows, 2 * quarter):
                t0 = base % DIFF_DH
                c, s = tab[t0:t0 + quarter], tab[t0 + quarter:t0 + 2 * quarter]
                u1 = q_t[base:base + quarter]
                u2 = q_t[base + quarter:base + 2 * quarter]
                q_t_ref[0, r0 + base:r0 + base + quarter, :] = ((u1 * c - u2 * s) * q_scale).astype(BF16)
                q_t_ref[0, r0 + base + quarter:r0 + base + 2 * quarter, :] = (
                    (u2 * c + u1 * s) * q_scale).astype(BF16)
        return tail

    def key_tail(cols):
        def tail(k):
            k_ref[0, :, cols] = _rope(k, cos, sin_signed, first_half).astype(BF16)
        return tail

    def value_tail(r0):
        def tail(v_t):
            v_t_ref[0, r0:r0 + t_rows, :] = v_t.astype(BF16)
        return tail

    def gate_tail(cols):
        def tail(mg):
            sm_ref[0, :, cols] = _sigmoid(mg).astype(BF16)
        return tail

    rotary = _interleave(row_pieces(wq_t_ref, query_tail), _column_pieces(h, wk_ref, key_tail))
    plain = _interleave(_column_pieces(h, wm_ref, gate_tail), row_pieces(wv_t_ref, value_tail))
    _run_pieces(_interleave(rotary, plain))


def _proj_dif(xs, modt, gain, cos_t, sin_t, tab_t, wq_t, wk, wv_t, wm, *, n_lat_tiles, n_tiles):
    b_, lp, d = xs.shape
    tm = TOKEN_TILE
    row = lambda width: pl.BlockSpec((1, tm, width), lambda b, j: (b, j, 0))
    col = lambda height: pl.BlockSpec((1, height, tm), lambda b, j: (b, 0, j))
    tab = pl.BlockSpec((tm, LANES), lambda b, j: (j, 0))
    kern = functools.partial(_proj_dif_kernel, q_scale=DIFF_DH ** -0.5 * LOG2_E)
    return pl.pallas_call(
        kern,
        out_shape=(
            jax.ShapeDtypeStruct((b_, wq_t.shape[0], lp), BF16),
            jax.ShapeDtypeStruct((b_, lp, wk.shape[1]), BF16),
            jax.ShapeDtypeStruct((b_, wv_t.shape[0], lp), BF16),
            jax.ShapeDtypeStruct((b_, lp, wm.shape[1]), BF16),
        ),
        grid=(b_, n_tiles),
        in_specs=[
            row(d),
            pl.BlockSpec((1, 1, MOD_ROWS, d), lambda b, j: (b, (j >= n_lat_tiles).astype(jnp.int32), 0, 0)),
            _resident((1, d)),
            tab, tab,
            pl.BlockSpec((DIFF_DH, tm), lambda b, j: (0, j)),
            _resident(wq_t.shape), _resident(wk.shape), _resident(wv_t.shape), _resident(wm.shape),
        ],
        out_specs=(col(wq_t.shape[0]), row(wk.shape[1]), col(wv_t.shape[0]), row(wm.shape[1])),
        compiler_params=_params(("parallel", "parallel")),
        name="proj_dif",
    )(xs, modt, gain, cos_t, sin_t, tab_t, wq_t, wk, wv_t, wm)


def _gla_decays(q, k, logf, tri, ref_row, end_row):
    c = GLA_CHUNK
    hi, lo, _ = _split3(logf)
    cum = _dot(tri, hi) + _dot(tri, lo)
    n_chunks = q.shape[0] // c
    totals = [cum[a * c + end_row:a * c + end_row + 1] for a in range(n_chunks)]
    per_chunk = lambda rows: jnp.concatenate([jnp.broadcast_to(r, (c, r.shape[1])) for r in rows], axis=0)
    ref = per_chunk([cum[a * c + ref_row:a * c + ref_row + 1] for a in range(n_chunks)])
    total = per_chunk(totals)
    q_state = (q * jnp.exp(cum)).astype(BF16)
    q_in = (q * jnp.exp(cum - ref)).astype(BF16)
    k_in = (k * jnp.exp(ref - cum)).astype(BF16)
    k_out = (k * jnp.exp(total - cum)).astype(BF16)
    return q_state, q_in, k_in, k_out, [jnp.exp(t) for t in totals]


def _gla_kernel(qf_ref, kf_ref, vf_ref, lf_ref, qb_ref, kb_ref, vb_ref, lb_ref,
                of_ref, ob_ref, sf_ref, sb_ref):
    @pl.when(pl.program_id(1) == 0)
    def _():
        sf_ref[...] = jnp.zeros(sf_ref.shape, F32)
        sb_ref[...] = jnp.zeros(sb_ref.shape, F32)

    c = GLA_CHUNK
    rows = qf_ref.shape[1]
    n_chunks = rows // c
    row = lax.broadcasted_iota(jnp.int32, (rows, rows), 0)
    col = lax.broadcasted_iota(jnp.int32, (rows, rows), 1)
    same_chunk = (row // c) == (col // c)
    lower = same_chunk & (col <= row)
    upper = same_chunk & (col >= row)
    dirs = [
        (_gla_decays(qf_ref[0], kf_ref[0], lf_ref[0], jnp.where(lower, 1.0, 0.0).astype(BF16),
                     c // 2 - 1, c - 1), vf_ref[0], lower, sf_ref, of_ref, list(range(n_chunks))),
        (_gla_decays(qb_ref[0], kb_ref[0], lb_ref[0], jnp.where(upper, 1.0, 0.0).astype(BF16),
                     c // 2, 0), vb_ref[0], upper, sb_ref, ob_ref, list(range(n_chunks))[::-1]),
    ]
    dk = qf_ref.shape[2] // GLA_HEADS
    dv = vf_ref.shape[2] // GLA_HEADS
    chains = []
    for (q_state, q_in, k_in, k_out, decays), v, keep, s_ref, o_ref, order in dirs:
        for hd in range(GLA_HEADS):
            ks = slice(hd * dk, (hd + 1) * dk)
            vs = slice(hd * dv, (hd + 1) * dv)
            chains.append(dict(q_state=q_state[:, ks], q_in=q_in[:, ks], k_in=k_in[:, ks], k_out=k_out[:, ks],
                               decays=[d[:, ks] for d in decays], v=v[:, vs], keep=keep, s_ref=s_ref,
                               o_ref=o_ref, hd=hd, vs=vs, order=order))
    chunk = lambda x, a: x[a * c:(a + 1) * c]
    for ch in chains:
        ch["attn"] = _dot_nt(ch["q_in"], ch["k_in"])
    for ch in chains:
        ch["update"] = [_dot_tn(chunk(ch["v"], a), chunk(ch["k_out"], a)) for a in range(n_chunks)]
    for ch in chains:
        state_t = ch["s_ref"][ch["hd"]]
        ch["before"] = {}
        for a in ch["order"]:
            ch["before"][a] = state_t
            state_t = state_t * ch["decays"][a] + ch["update"][a]
        ch["s_ref"][ch["hd"]] = state_t
    for ch in chains:
        ch["inter"] = jnp.concatenate(
            [_dot_nt(chunk(ch["q_state"], a), ch["before"][a].astype(BF16)) for a in range(n_chunks)], axis=0)
    for ch in chains:
        attn = jnp.where(ch["keep"], ch["attn"], 0.0).astype(BF16)
        ch["o_ref"][0, :, ch["vs"]] = (ch["inter"] + _dot(attn, ch["v"])).astype(BF16)


def _gla(gq, gk, gv, logf, *, n_lat_blocks, n_ctx_blocks):
    b_, lp, qk = gq.shape
    vv = gv.shape[2]
    c = GLA_BLOCK
    n = n_lat_blocks + n_ctx_blocks
    fwd = lambda b, s: (b, (s + n_lat_blocks) % n, 0)
    bwd = lambda b, s: (b, n - 1 - s, 0)
    lf_spec = pl.BlockSpec((1, c, qk), lambda b, s: (b, (s + n_lat_blocks) % n, 0))
    lb_spec = pl.BlockSpec((1, c, qk), lambda b, s: (b, n - 1 - s, 1))
    return pl.pallas_call(
        _gla_kernel,
        out_shape=(jax.ShapeDtypeStruct((b_, n * c, vv), BF16), jax.ShapeDtypeStruct((b_, n * c, vv), BF16)),
        grid=(b_, n),
        in_specs=[
            pl.BlockSpec((1, c, qk), fwd), pl.BlockSpec((1, c, qk), fwd), pl.BlockSpec((1, c, vv), fwd), lf_spec,
            pl.BlockSpec((1, c, qk), bwd), pl.BlockSpec((1, c, qk), bwd), pl.BlockSpec((1, c, vv), bwd), lb_spec,
        ],
        out_specs=(pl.BlockSpec((1, c, vv), fwd), pl.BlockSpec((1, c, vv), bwd)),
        scratch_shapes=[
            pltpu.VMEM((GLA_HEADS, vv // GLA_HEADS, qk // GLA_HEADS), F32),
            pltpu.VMEM((GLA_HEADS, vv // GLA_HEADS, qk // GLA_HEADS), F32),
        ],
        compiler_params=_params(("parallel", "arbitrary")),
        name="gla_scan",
    )(gq, gk, gv, logf, gq, gk, gv, logf)


def _attn_kernel(q_t_ref, k_ref, v_t_ref, lam_ref, o_ref, m_ref, acc_ref, s_ref, bm_ref,
                 *, tk, n_kblocks, lam_init):
    tq = q_t_ref.shape[2]
    q_t = q_t_ref[0]
    row = lax.broadcasted_iota(jnp.int32, q_t.shape, 0)
    zero = jnp.zeros_like(q_t)
    qz_t = jnp.concatenate([jnp.where(row < DIFF_DH, q_t, zero), jnp.where(row >= DIFF_DH, q_t, zero)], axis=1)
    m_ref[...] = jnp.full(m_ref.shape, -jnp.inf, F32)
    acc_ref[...] = jnp.zeros(acc_ref.shape, F32)

    groups = list(range(2 * tq // ATTN_QGROUP))
    lanes = lambda g: slice(g * ATTN_QGROUP, (g + 1) * ATTN_QGROUP)

    def scores(kb, slot, g):
        s_t = _dot(kb, qz_t[:, lanes(g)])
        s_ref[slot, g] = s_t
        bm_ref[slot, g] = jnp.max(s_t, axis=0, keepdims=True)

    ones_rows = jnp.ones((BF16_ROWS, tk), BF16)

    def softmax_values(vb_t, slot, g):
        s_t = s_ref[slot, g]
        m_prev = m_ref[g]
        m_new = jnp.maximum(m_prev, bm_ref[slot, g])
        m_ref[g] = m_new
        alpha = jnp.exp2(m_prev - m_new)
        p_t = jnp.exp2(s_t - m_new).astype(BF16)
        acc_ref[g] = alpha * acc_ref[g] + _dot(vb_t, p_t)

    def load_keys(j):
        return k_ref[0, pl.ds(pl.multiple_of(j * tk, tk), tk), :]

    def load_values(j):
        vb_t = v_t_ref[0, :, pl.ds(pl.multiple_of(j * tk, tk), tk)]
        return jnp.concatenate([vb_t, ones_rows], axis=0)

    def stage(j, slot, last=False):
        kb = None if last else load_keys(j + 1)
        vb_t = load_values(j)
        for g in groups:
            if not last:
                scores(kb, 1 - slot, g)
            softmax_values(vb_t, slot, g)

    kb0 = load_keys(0)
    for g in groups:
        scores(kb0, 0, g)
    n_pairs = max(n_kblocks - 1, 0) // 2

    def pair(i, carry):
        stage(2 * i, 0)
        stage(2 * i + 1, 1)
        return carry

    lax.fori_loop(0, n_pairs, pair, 0)
    for j in range(2 * n_pairs, n_kblocks):
        stage(j, j % 2, last=j == n_kblocks - 1)

    lp = lam_ref[...]
    lam = (jnp.exp(jnp.sum(lp[0:1] * lp[1:2], axis=-1, keepdims=True))
           - jnp.exp(jnp.sum(lp[2:3] * lp[3:4], axis=-1, keepdims=True)) + lam_init)
    dv = v_t_ref.shape[1]
    acc = jnp.concatenate([acc_ref[g] for g in groups], axis=1)
    o_t = acc[:dv] / acc[dv:dv + 1]
    o_ref[0] = (o_t[:, :tq] - lam * o_t[:, tq:]).T.astype(BF16)


def _attn(dq_t, dk, dv_t, lam_p, *, seq, n_keys, lam_init):
    b_, lp, width = dk.shape
    tq, tk = ATTN_TQ, ATTN_TK
    heads = width // LANES
    n_groups = 2 * tq // ATTN_QGROUP
    kern = functools.partial(_attn_kernel, tk=tk, n_kblocks=n_keys // tk, lam_init=lam_init)
    return pl.pallas_call(
        kern,
        out_shape=jax.ShapeDtypeStruct((b_, seq, width), BF16),
        grid=(b_, heads, seq // tq),
        in_specs=[
            pl.BlockSpec((1, LANES, tq), lambda b, h, i: (b, h, i)),
            pl.BlockSpec((1, lp, LANES), lambda b, h, i: (b, 0, h)),
            pl.BlockSpec((1, LANES, lp), lambda b, h, i: (b, h, 0)),
            pl.BlockSpec(lam_p.shape, lambda b, h, i: (0, 0)),
        ],
        out_specs=pl.BlockSpec((1, tq, LANES), lambda b, h, i: (b, i, h)),
        scratch_shapes=[
            pltpu.VMEM((n_groups, 1, ATTN_QGROUP), F32),
            pltpu.VMEM((n_groups, LANES + BF16_ROWS, ATTN_QGROUP), F32),
            pltpu.VMEM((2, n_groups, tk, ATTN_QGROUP), F32),
            pltpu.VMEM((2, n_groups, 1, ATTN_QGROUP), F32),
        ],
        compiler_params=_params(("parallel", "parallel", "parallel")),
        name="diff_attn",
    )(dq_t, dk, dv_t, lam_p)


def _head_rms(x, gain, width):
    parts = []
    for h0 in range(0, x.shape[1], width):
        parts.append(_rms(x[:, h0:h0 + width]))
    return jnp.concatenate(parts, axis=1) * gain


def _merge_kernel(x_ref, mod_ref, of_ref, ob_ref, sg_ref, od_ref, sm_ref, gg_ref, gd_ref,
                  wa_ref, wb_ref, wo_ref, o_ref, *, dv_gla, dv_diff, diff_scale):
    d = x_ref.shape[2]
    o_gla = of_ref[0].astype(F32) + ob_ref[0].astype(F32)
    ya = _head_rms(o_gla, gg_ref[...], dv_gla) * sg_ref[0].astype(F32)
    ya = _dot(ya.astype(BF16), wa_ref[...])
    yb = _head_rms(od_ref[0].astype(F32), gd_ref[...], dv_diff) * diff_scale
    yb = _dot(yb.astype(BF16), wb_ref[...])
    sm = sm_ref[0].astype(F32)
    merged = (sm[:, :d] * ya + sm[:, d:] * yb).astype(BF16)
    gate = mod_ref[0, 0][5:6]
    o_ref[0] = x_ref[0] + gate * _dot(merged, wo_ref[...])


def _merge(x1, modt, o_f, o_b, sgg, o_d, smg, g_gla, g_diff, wa, wb, wo, *, n_lat_tiles, diff_scale):
    b_, _, d = x1.shape
    tm = TOKEN_TILE
    row = lambda width: pl.BlockSpec((1, tm, width), lambda b, j: (b, j, 0))
    kern = functools.partial(_merge_kernel, dv_gla=o_f.shape[2] // GLA_HEADS,
                             dv_diff=o_d.shape[2] // DIFF_HEADS, diff_scale=diff_scale)
    return pl.pallas_call(
        kern,
        out_shape=jax.ShapeDtypeStruct((b_, n_lat_tiles * tm, d), F32),
        grid=(b_, n_lat_tiles),
        in_specs=[
            row(d),
            pl.BlockSpec((1, 1, MOD_ROWS, d), lambda b, j: (b, 0, 0, 0)),
            row(o_f.shape[2]), row(o_b.shape[2]), row(sgg.shape[2]), row(o_d.shape[2]), row(smg.shape[2]),
            _resident(g_gla.shape), _resident(g_diff.shape),
            _resident(wa.shape), _resident(wb.shape), _resident(wo.shape),
        ],
        out_specs=row(d),
        compiler_params=_params(("parallel", "parallel")),
        name="merge",
    )(x1, modt, o_f, o_b, sgg, o_d, smg, g_gla, g_diff, wa, wb, wo)


def _rope_tables(seq, lp):
    half = DIFF_DH // 2
    quarter = half // 2
    inv = ROPE_BASE ** (-np.arange(quarter, dtype=np.float32) / quarter)
    t = jnp.arange(seq, dtype=jnp.int32)
    ang_row = (t // GRID_W).astype(F32)[:, None] * inv[None, :]
    ang_col = (t % GRID_W).astype(F32)[:, None] * inv[None, :]
    ang = jnp.concatenate([ang_row, ang_row, ang_col, ang_col], axis=1)
    sign = np.tile(np.concatenate([-np.ones(quarter), np.ones(quarter)]), 2).astype(np.float32)
    cos = jnp.tile(jnp.cos(ang), (1, LANES // DIFF_DH))
    sin = jnp.tile(jnp.sin(ang) * sign[None, :], (1, LANES // DIFF_DH))
    pad = lp - seq
    cos = jnp.concatenate([cos, jnp.ones((pad, LANES), F32)], axis=0)
    sin = jnp.concatenate([sin, jnp.zeros((pad, LANES), F32)], axis=0)
    tab_t = jnp.concatenate([jnp.cos(ang_row), jnp.sin(ang_row), jnp.cos(ang_col), jnp.sin(ang_col)], axis=1).T
    ident = np.repeat(np.array([1.0, 0.0, 1.0, 0.0], np.float32), quarter)[:, None]
    tab_t = jnp.concatenate([tab_t, jnp.broadcast_to(ident, (DIFF_DH, pad))], axis=1)
    return cos, sin, tab_t


def kernel(x, c, ctx, c_ctx, w_ada, b_ada, ffn1_norm, ffn1_w1, ffn1_w3, ffn1_w2, mix_norm, w_in, gla_gate_w2, gla_gate_b, gla_out_norm, diff_lambda, diff_out_norm, w_branch_gla, w_branch_diff, w_out, ffn2_norm, ffn2_w1, ffn2_w3, ffn2_w2, final_norm):
    b_, seq, d = x.shape
    n_ctx = ctx.shape[1]
    assert w_ada.shape[0] == 1, "single-layer stack"
    tm = TOKEN_TILE
    assert seq % tm == 0 and seq % ATTN_TQ == 0 and n_ctx <= tm
    assert seq % GLA_BLOCK == 0 and n_ctx % GLA_BLOCK == 0 and (seq + n_ctx) % ATTN_TK == 0
    n_lat_tiles = seq // tm
    n_tiles = n_lat_tiles + 1
    lp = n_tiles * tm
    lam_init = 0.8 - 0.6 * math.exp(0.0)

    gla_qk = GLA_HEADS * (d // 8)
    gla_v = GLA_HEADS * (d // 4)
    dif = DIFF_HEADS * 2 * DIFF_DH
    splits = np.cumsum([gla_qk, gla_qk, gla_v, gla_v, 2 * GLA_RANK, dif, dif, dif, 2 * d])
    assert w_in.shape[2] == splits[-1]
    wi = w_in[0].astype(BF16)
    w_gq, w_gk, w_gv, w_gg, w_lr, w_dq, w_dk, w_dv, w_mg = jnp.split(wi, splits[:-1].tolist(), axis=1)
    w_lr = jnp.pad(w_lr, ((0, 0), (0, LANES - 2 * GLA_RANK)))
    w2cat = jnp.zeros((LANES, 2 * gla_qk), F32)
    w2cat = w2cat.at[:GLA_RANK, :gla_qk].set(gla_gate_w2[0, 0])
    w2cat = w2cat.at[GLA_RANK:2 * GLA_RANK, gla_qk:].set(gla_gate_w2[0, 1]).astype(BF16)
    b2cat = gla_gate_b[0].reshape(1, 2 * gla_qk)

    rows = -(-(b_ + 1) // 8) * 8
    c_rows = jnp.concatenate([c, c_ctx[None, :], jnp.zeros((rows - b_ - 1, d), F32)], axis=0)
    mods = _mod_table(c_rows, w_ada[0], b_ada).reshape(rows, N_MOD, d)
    modt = jnp.stack([mods[:b_], jnp.broadcast_to(mods[b_], (b_, N_MOD, d))], axis=1)
    modt = jnp.pad(modt, ((0, 0), (0, 0), (0, MOD_ROWS - N_MOD), (0, 0)))

    ctx_tile = jnp.pad(ctx, ((0, 0), (0, tm - n_ctx), (0, 0)))
    tiles = dict(n_lat_tiles=n_lat_tiles, n_tiles=n_tiles)

    x1 = _ffn(x, modt, ffn1_norm, ffn1_w1[0].astype(BF16), ffn1_w3[0].astype(BF16),
              ffn1_w2[0].astype(BF16), row0=0, ctx_tile=ctx_tile, **tiles)

    gq, gk, gv, sgg, logf = _proj_gla(x1, modt, mix_norm, w_gq, w_gk, w_gv, w_gg, w_lr, w2cat, b2cat, **tiles)
    cos_t, sin_t, tab_t = _rope_tables(seq, lp)
    dq, dk, dv, smg = _proj_dif(x1, modt, mix_norm, cos_t, sin_t, tab_t, w_dq.T, w_dk, w_dv.T, w_mg, **tiles)

    o_f, o_b = _gla(gq, gk, gv, logf, n_lat_blocks=seq // GLA_BLOCK, n_ctx_blocks=n_ctx // GLA_BLOCK)

    lam_p = jnp.pad(diff_lambda[0], ((0, 4), (0, LANES - DIFF_DH)))
    o_d = _attn(dq, dk, dv, lam_p, seq=seq, n_keys=seq + n_ctx, lam_init=lam_init)

    x2 = _merge(x1, modt, o_f, o_b, sgg, o_d, smg, gla_out_norm, diff_out_norm,
                w_branch_gla[0].astype(BF16), w_branch_diff[0].astype(BF16), w_out[0].astype(BF16),
                n_lat_tiles=n_lat_tiles, diff_scale=1.0 - lam_init)

    return _ffn(x2, modt, ffn2_norm, ffn2_w1[0].astype(BF16), ffn2_w3[0].astype(BF16),
                ffn2_w2[0].astype(BF16), row0=6, n_lat_tiles=n_lat_tiles, n_tiles=n_lat_tiles,
                final_gain=final_norm.reshape(1, d))
```

```python
import functools
import math

import jax
import jax.numpy as jnp
import numpy as np
from jax import lax
from jax.experimental import pallas as pl
from jax.experimental.pallas import tpu as pltpu

F32 = jnp.float32
BF16 = jnp.bfloat16

EPS = 1e-6
GRID_W = 64
N_MOD = 9
MOD_ROWS = 16
GLA_HEADS = 4
GLA_RANK = 16
GLA_TAU = 16.0
GLA_CHUNK = 64
GLA_BLOCK = 256
DIFF_HEADS = 8
DIFF_DH = 64
ROPE_BASE = 10000.0
ROPE_PAIR = DIFF_DH // 4
LANES = 128
MXU_TILE = 256
BF16_ROWS = 16
LOG2_E = 1.4426950408889634

TOKEN_TILE = 512
ATTN_TQ = 2048
ATTN_TK = 768
ATTN_QGROUP = 256
V7X_VMEM_BYTES = 64 * 1024 * 1024
VMEM_LIMIT = V7X_VMEM_BYTES * 13 // 16


def _resident(shape):
    zeros = (0,) * len(shape)
    return pl.BlockSpec(shape, lambda *_: zeros, pipeline_mode=pl.Buffered(1))


def _params(semantics):
    return pltpu.CompilerParams(dimension_semantics=semantics, vmem_limit_bytes=VMEM_LIMIT)


def _dot(a, b):
    return jnp.dot(a, b, preferred_element_type=F32)


def _dot_nt(a, b):
    return lax.dot_general(a, b, (((1,), (1,)), ((), ())), preferred_element_type=F32)


def _dot_tn(a, b):
    return lax.dot_general(a, b, (((0,), (0,)), ((), ())), preferred_element_type=F32)


def _sigmoid(x):
    return 1.0 / (1.0 + jnp.exp(-x))


def _rms(x):
    return x * lax.rsqrt(jnp.mean(x * x, axis=-1, keepdims=True) + EPS)


def _modnorm(x, gain, shift, scale1):
    return (_rms(x) * gain) * scale1 + shift


def _mod_kernel(c_ref, w_ref, b_ref, o_ref):
    k = pl.program_id(0)
    c = c_ref[...]
    s = c * _sigmoid(c)
    acc = jnp.zeros(o_ref.shape, F32)
    w = w_ref[...]
    s_parts = _bf16_terms(s, 3)
    w_parts = _bf16_terms(w, 3)
    for i in range(3):
        for j in range(3 - i):
            acc = acc + _dot(s_parts[i], w_parts[j])
    val = acc + b_ref[...]
    add = jnp.where(k % 3 == 1, 1.0, 0.0).astype(F32)
    mul = jnp.where((k == 2) | (k == 8), 0.5, 1.0).astype(F32)
    o_ref[...] = (val + add) * mul


def _bf16_terms(x, n):
    terms = []
    for _ in range(n):
        t = x.astype(BF16)
        terms.append(t)
        x = x - t.astype(F32)
    return terms


def _mod_table(c_rows, w_ada, b_ada):
    rows, d = c_rows.shape
    return pl.pallas_call(
        _mod_kernel,
        out_shape=jax.ShapeDtypeStruct((rows, N_MOD * d), F32),
        grid=(N_MOD,),
        in_specs=[
            pl.BlockSpec((rows, d), lambda k: (0, 0)),
            pl.BlockSpec((d, d), lambda k: (0, k)),
            pl.BlockSpec((1, d), lambda k: (0, k)),
        ],
        out_specs=pl.BlockSpec((rows, d), lambda k: (0, k)),
        compiler_params=_params(("arbitrary",)),
        name="mod_table",
    )(c_rows, w_ada, b_ada)


def _ffn_kernel(x_ref, mod_ref, g_ref, w1_ref, w3_ref, w2_ref, *rest, row0, f_chunk, n_lat_tiles, final):
    if final:
        gf_ref, o_ref = rest
        x = x_ref[0]
    else:
        c_ref, o_ref = rest
        x = jnp.where(pl.program_id(1) >= n_lat_tiles, c_ref[0], x_ref[0])
    mod = mod_ref[0, 0]
    shift, scale1, gate = mod[row0:row0 + 1], mod[row0 + 1:row0 + 2], mod[row0 + 2:row0 + 3]
    h = _modnorm(x, g_ref[...], shift, scale1).astype(BF16)
    acc = jnp.zeros(x.shape, F32)
    f_total = w1_ref.shape[1]
    for f0 in range(0, f_total, f_chunk):
        f1 = min(f0 + f_chunk, f_total)
        a = _dot(h, w1_ref[:, f0:f1])
        b = _dot(h, w3_ref[:, f0:f1])
        u = (a * _sigmoid(a) * b).astype(BF16)
        acc = acc + _dot(u, w2_ref[f0:f1, :])
    y = x + gate * acc
    if final:
        y = _rms(y) * gf_ref[...]
    o_ref[0] = y


def _ffn(x, modt, gain, w1, w3, w2, *, row0, n_lat_tiles, n_tiles, ctx_tile=None, final_gain=None):
    b_, _, d = x.shape
    f = w1.shape[1]
    tm = TOKEN_TILE
    final = final_gain is not None
    assert final != (ctx_tile is not None)
    f_chunk = -(-f // (2 * MXU_TILE)) * MXU_TILE
    kern = functools.partial(_ffn_kernel, row0=row0, f_chunk=f_chunk, n_lat_tiles=n_lat_tiles, final=final)
    in_specs = [
        pl.BlockSpec((1, tm, d), lambda b, j: (b, jnp.minimum(j, n_lat_tiles - 1), 0)),
        pl.BlockSpec((1, 1, MOD_ROWS, d), lambda b, j: (b, (j >= n_lat_tiles).astype(jnp.int32), 0, 0)),
        _resident((1, d)),
        _resident((d, f)),
        _resident((d, f)),
        _resident((f, d)),
    ]
    args = [x, modt, gain, w1, w3, w2]
    if final:
        in_specs.append(_resident((1, d)))
        args.append(final_gain)
    else:
        in_specs.append(pl.BlockSpec((1, tm, d), lambda b, j: (b, 0, 0)))
        args.append(ctx_tile)
    return pl.pallas_call(
        kern,
        out_shape=jax.ShapeDtypeStruct((b_, n_tiles * tm, d), F32),
        grid=(b_, n_tiles),
        in_specs=in_specs,
        out_specs=pl.BlockSpec((1, tm, d), lambda b, j: (b, j, 0)),
        compiler_params=_params(("parallel", "parallel")),
        name="ffn_final" if final else "ffn",
    )(*args)


def _run_pieces(pieces):
    pending = None
    for matmul, tail in pieces:
        result = matmul()
        if pending is not None:
            pending[0](pending[1])
        pending = (tail, result)
    pending[0](pending[1])


def _interleave(a, b):
    n = min(len(a), len(b))
    return [p for pair in zip(a, b) for p in pair] + a[n:] + b[n:]


def _column_pieces(lhs, w_ref, tail_of):
    def piece(cols):
        return (lambda: _dot(lhs, w_ref[:, cols])), tail_of(cols)
    return [piece(slice(c0, c0 + MXU_TILE)) for c0 in range(0, w_ref.shape[1], MXU_TILE)]


def _proj_gla_kernel(x_ref, mod_ref, g_ref, wq_ref, wk_ref, wv_ref, wg_ref, wlr_ref, w2_ref, b2_ref,
                     q_ref, k_ref, v_ref, sg_ref, lf_ref, *, q_scale):
    x = x_ref[0]
    mod = mod_ref[0, 0]
    h = _modnorm(x, g_ref[...], mod[3:4], mod[4:5]).astype(BF16)
    lr = _dot(h, wlr_ref[...]).astype(BF16)

    def decay_tail(cols):
        def tail(z):
            z = z + b2_ref[:, cols]
            lf_ref[0, :, cols] = (jnp.minimum(z, 0.0) - jnp.log(1.0 + jnp.exp(-jnp.abs(z)))) * (1.0 / GLA_TAU)
        return tail

    def gate_tail(cols):
        def tail(gg):
            sg_ref[0, :, cols] = (gg * _sigmoid(gg)).astype(BF16)
        return tail

    def value_tail(cols):
        def tail(v):
            v_ref[0, :, cols] = v.astype(BF16)
        return tail

    def query_tail(cols):
        def tail(q):
            q_ref[0, :, cols] = q * q_scale
        return tail

    def key_tail(cols):
        def tail(k):
            k_ref[0, :, cols] = k
        return tail

    heavy = _interleave(_column_pieces(lr, w2_ref, decay_tail), _column_pieces(h, wg_ref, gate_tail))
    light = (_column_pieces(h, wv_ref, value_tail) + _column_pieces(h, wq_ref, query_tail)
             + _column_pieces(h, wk_ref, key_tail))
    _run_pieces(_interleave(heavy, light))


def _proj_gla(xs, modt, gain, wq, wk, wv, wg, wlr, w2cat, b2cat, *, n_lat_tiles, n_tiles):
    b_, lp, d = xs.shape
    tm = TOKEN_TILE
    qk = wq.shape[1]
    vv = wv.shape[1]
    row = lambda width: pl.BlockSpec((1, tm, width), lambda b, j: (b, j, 0))
    kern = functools.partial(_proj_gla_kernel, q_scale=(qk // GLA_HEADS) ** -0.5)
    return pl.pallas_call(
        kern,
        out_shape=(
            jax.ShapeDtypeStruct((b_, lp, qk), F32),
            jax.ShapeDtypeStruct((b_, lp, qk), F32),
            jax.ShapeDtypeStruct((b_, lp, vv), BF16),
            jax.ShapeDtypeStruct((b_, lp, vv), BF16),
            jax.ShapeDtypeStruct((b_, lp, 2 * qk), F32),
        ),
        grid=(b_, n_tiles),
        in_specs=[
            row(d),
            pl.BlockSpec((1, 1, MOD_ROWS, d), lambda b, j: (b, (j >= n_lat_tiles).astype(jnp.int32), 0, 0)),
            _resident((1, d)),
            _resident(wq.shape), _resident(wk.shape), _resident(wv.shape), _resident(wg.shape),
            _resident(wlr.shape), _resident(w2cat.shape), _resident(b2cat.shape),
        ],
        out_specs=(row(qk), row(qk), row(vv), row(vv), row(2 * qk)),
        compiler_params=_params(("parallel", "parallel")),
        name="proj_gla",
    )(xs, modt, gain, wq, wk, wv, wg, wlr, w2cat, b2cat)


def _rope(u, cos, sin_signed, first_half):
    out = []
    for g in range(u.shape[1] // LANES):
        ug = u[:, g * LANES:(g + 1) * LANES]
        partner = jnp.where(first_half, pltpu.roll(ug, LANES - ROPE_PAIR, 1), pltpu.roll(ug, ROPE_PAIR, 1))
        out.append(ug * cos + partner * sin_signed)
    return jnp.concatenate(out, axis=1)


def _proj_dif_kernel(x_ref, mod_ref, g_ref, cos_ref, sin_ref, tab_t_ref, wq_t_ref, wk_ref, wv_t_ref, wm_ref,
                     q_t_ref, k_ref, v_t_ref, sm_ref, *, q_scale):
    x = x_ref[0]
    mod = mod_ref[0, 0]
    h = _modnorm(x, g_ref[...], mod[3:4], mod[4:5]).astype(BF16)
    quarter = DIFF_DH // 4
    tab = tab_t_ref[...]
    cos = cos_ref[...]
    sin_signed = sin_ref[...]
    lane = lax.broadcasted_iota(jnp.int32, cos.shape, 1)
    first_half = (lane % (2 * ROPE_PAIR)) < ROPE_PAIR

    t_rows = 2 * MXU_TILE

    def row_pieces(w_t_ref, tail_of):
        def piece(r0):
            return (lambda: _dot_nt(w_t_ref[r0:r0 + t_rows, :], h)), tail_of(r0)
        return [piece(r0) for r0 in range(0, w_t_ref.shape[0], t_rows)]

    def query_tail(r0):
        def tail(q_t):
            for base in range(0, t_rows, 2 * quarter):
                t0 = base % DIFF_DH
                c, s = tab[t0:t0 + quarter], tab[t0 + quarter:t0 + 2 * quarter]
                u1 = q_t[base:base + quarter]
                u2 = q_t[base + quarter:base + 2 * quarter]
                q_t_ref[0, r0 + base:r0 + base + quarter, :] = ((u1 * c - u2 * s) * q_scale).astype(BF16)
                q_t_ref[0, r0 + base + quarter:r0 + base + 2 * quarter, :] = (
                    (u2 * c + u1 * s) * q_scale).astype(BF16)
        return tail

    def key_tail(cols):
        def tail(k):
            k_ref[0, :, cols] = _rope(k, cos, sin_signed, first_half).astype(BF16)
        return tail

    def value_tail(r0):
        def tail(v_t):
            v_t_ref[0, r0:r0 + t_rows, :] = v_t.astype(BF16)
        return tail

    def gate_tail(cols):
        def tail(mg):
            sm_ref[0, :, cols] = _sigmoid(mg).astype(BF16)
        return tail

    rotary = _interleave(row_pieces(wq_t_ref, query_tail), _column_pieces(h, wk_ref, key_tail))
    plain = _interleave(_column_pieces(h, wm_ref, gate_tail), row_pieces(wv_t_ref, value_tail))
    _run_pieces(_interleave(rotary, plain))


def _proj_dif(xs, modt, gain, cos_t, sin_t, tab_t, wq_t, wk, wv_t, wm, *, n_lat_tiles, n_tiles):
    b_, lp, d = xs.shape
    tm = TOKEN_TILE
    row = lambda width: pl.BlockSpec((1, tm, width), lambda b, j: (b, j, 0))
    col = lambda height: pl.BlockSpec((1, height, tm), lambda b, j: (b, 0, j))
    tab = pl.BlockSpec((tm, LANES), lambda b, j: (j, 0))
    kern = functools.partial(_proj_dif_kernel, q_scale=DIFF_DH ** -0.5 * LOG2_E)
    return pl.pallas_call(
        kern,
        out_shape=(
            jax.ShapeDtypeStruct((b_, wq_t.shape[0], lp), BF16),
            jax.ShapeDtypeStruct((b_, lp, wk.shape[1]), BF16),
            jax.ShapeDtypeStruct((b_, wv_t.shape[0], lp), BF16),
            jax.ShapeDtypeStruct((b_, lp, wm.shape[1]), BF16),
        ),
        grid=(b_, n_tiles),
        in_specs=[
            row(d),
            pl.BlockSpec((1, 1, MOD_ROWS, d), lambda b, j: (b, (j >= n_lat_tiles).astype(jnp.int32), 0, 0)),
            _resident((1, d)),
            tab, tab,
            pl.BlockSpec((DIFF_DH, tm), lambda b, j: (0, j)),
            _resident(wq_t.shape), _resident(wk.shape), _resident(wv_t.shape), _resident(wm.shape),
        ],
        out_specs=(col(wq_t.shape[0]), row(wk.shape[1]), col(wv_t.shape[0]), row(wm.shape[1])),
        compiler_params=_params(("parallel", "parallel")),
        name="proj_dif",
    )(xs, modt, gain, cos_t, sin_t, tab_t, wq_t, wk, wv_t, wm)


def _gla_decays(q, k, logf, tri, ref_row, end_row):
    c = GLA_CHUNK
    hi, lo = _bf16_terms(logf, 2)
    cum = _dot(tri, hi) + _dot(tri, lo)
    n_chunks = q.shape[0] // c
    totals = [cum[a * c + end_row:a * c + end_row + 1] for a in range(n_chunks)]
    per_chunk = lambda rows: jnp.concatenate([jnp.broadcast_to(r, (c, r.shape[1])) for r in rows], axis=0)
    ref = per_chunk([cum[a * c + ref_row:a * c + ref_row + 1] for a in range(n_chunks)])
    total = per_chunk(totals)
    q_state = (q * jnp.exp(cum)).astype(BF16)
    q_in = (q * jnp.exp(cum - ref)).astype(BF16)
    k_in = (k * jnp.exp(ref - cum)).astype(BF16)
    k_out = (k * jnp.exp(total - cum)).astype(BF16)
    return q_state, q_in, k_in, k_out, [jnp.exp(t) for t in totals]


def _gla_kernel(qf_ref, kf_ref, vf_ref, lf_ref, qb_ref, kb_ref, vb_ref, lb_ref,
                of_ref, ob_ref, sf_ref, sb_ref):
    @pl.when(pl.program_id(1) == 0)
    def _():
        sf_ref[...] = jnp.zeros(sf_ref.shape, F32)
        sb_ref[...] = jnp.zeros(sb_ref.shape, F32)

    c = GLA_CHUNK
    rows = qf_ref.shape[1]
    n_chunks = rows // c
    row = lax.broadcasted_iota(jnp.int32, (rows, rows), 0)
    col = lax.broadcasted_iota(jnp.int32, (rows, rows), 1)
    same_chunk = (row // c) == (col // c)
    lower = same_chunk & (col <= row)
    upper = same_chunk & (col >= row)
    dirs = [
        (_gla_decays(qf_ref[0], kf_ref[0], lf_ref[0], jnp.where(lower, 1.0, 0.0).astype(BF16),
                     c // 2 - 1, c - 1), vf_ref[0], lower, sf_ref, of_ref, list(range(n_chunks))),
        (_gla_decays(qb_ref[0], kb_ref[0], lb_ref[0], jnp.where(upper, 1.0, 0.0).astype(BF16),
                     c // 2, 0), vb_ref[0], upper, sb_ref, ob_ref, list(range(n_chunks))[::-1]),
    ]
    dk = qf_ref.shape[2] // GLA_HEADS
    dv = vf_ref.shape[2] // GLA_HEADS
    chains = []
    for (q_state, q_in, k_in, k_out, decays), v, keep, s_ref, o_ref, order in dirs:
        for hd in range(GLA_HEADS):
            ks = slice(hd * dk, (hd + 1) * dk)
            vs = slice(hd * dv, (hd + 1) * dv)
            chains.append(dict(q_state=q_state[:, ks], q_in=q_in[:, ks], k_in=k_in[:, ks], k_out=k_out[:, ks],
                               decays=[d[:, ks] for d in decays], v=v[:, vs], keep=keep, s_ref=s_ref,
                               o_ref=o_ref, hd=hd, vs=vs, order=order))
    chunk = lambda x, a: x[a * c:(a + 1) * c]
    for ch in chains:
        ch["attn"] = _dot_nt(ch["q_in"], ch["k_in"])
    for ch in chains:
        ch["update"] = [_dot_tn(chunk(ch["v"], a), chunk(ch["k_out"], a)) for a in range(n_chunks)]
    for ch in chains:
        state_t = ch["s_ref"][ch["hd"]]
        ch["before"] = {}
        for a in ch["order"]:
            ch["before"][a] = state_t
            state_t = state_t * ch["decays"][a] + ch["update"][a]
        ch["s_ref"][ch["hd"]] = state_t
    for ch in chains:
        ch["inter"] = jnp.concatenate(
            [_dot_nt(chunk(ch["q_state"], a), ch["before"][a].astype(BF16)) for a in range(n_chunks)], axis=0)
    for ch in chains:
        attn = jnp.where(ch["keep"], ch["attn"], 0.0).astype(BF16)
        ch["o_ref"][0, :, ch["vs"]] = (ch["inter"] + _dot(attn, ch["v"])).astype(BF16)


def _gla(gq, gk, gv, logf, *, n_lat_blocks, n_ctx_blocks):
    b_, lp, qk = gq.shape
    vv = gv.shape[2]
    c = GLA_BLOCK
    n = n_lat_blocks + n_ctx_blocks
    fwd = lambda b, s: (b, (s + n_lat_blocks) % n, 0)
    bwd = lambda b, s: (b, n - 1 - s, 0)
    lf_spec = pl.BlockSpec((1, c, qk), lambda b, s: (b, (s + n_lat_blocks) % n, 0))
    lb_spec = pl.BlockSpec((1, c, qk), lambda b, s: (b, n - 1 - s, 1))
    return pl.pallas_call(
        _gla_kernel,
        out_shape=(jax.ShapeDtypeStruct((b_, n * c, vv), BF16), jax.ShapeDtypeStruct((b_, n * c, vv), BF16)),
        grid=(b_, n),
        in_specs=[
            pl.BlockSpec((1, c, qk), fwd), pl.BlockSpec((1, c, qk), fwd), pl.BlockSpec((1, c, vv), fwd), lf_spec,
            pl.BlockSpec((1, c, qk), bwd), pl.BlockSpec((1, c, qk), bwd), pl.BlockSpec((1, c, vv), bwd), lb_spec,
        ],
        out_specs=(pl.BlockSpec((1, c, vv), fwd), pl.BlockSpec((1, c, vv), bwd)),
        scratch_shapes=[
            pltpu.VMEM((GLA_HEADS, vv // GLA_HEADS, qk // GLA_HEADS), F32),
            pltpu.VMEM((GLA_HEADS, vv // GLA_HEADS, qk // GLA_HEADS), F32),
        ],
        compiler_params=_params(("parallel", "arbitrary")),
        name="gla_scan",
    )(gq, gk, gv, logf, gq, gk, gv, logf)


def _attn_kernel(q_t_ref, k_ref, v_t_ref, lam_ref, o_ref, m_ref, acc_ref, s_ref, bm_ref,
                 *, tk, n_kblocks, lam_init):
    tq = q_t_ref.shape[2]
    q_t = q_t_ref[0]
    row = lax.broadcasted_iota(jnp.int32, q_t.shape, 0)
    zero = jnp.zeros_like(q_t)
    qz_t = jnp.concatenate([jnp.where(row < DIFF_DH, q_t, zero), jnp.where(row >= DIFF_DH, q_t, zero)], axis=1)
    m_ref[...] = jnp.full(m_ref.shape, -jnp.inf, F32)
    acc_ref[...] = jnp.zeros(acc_ref.shape, F32)

    groups = list(range(2 * tq // ATTN_QGROUP))
    lanes = lambda g: slice(g * ATTN_QGROUP, (g + 1) * ATTN_QGROUP)

    def scores(kb, slot, g):
        s_t = _dot(kb, qz_t[:, lanes(g)])
        s_ref[slot, g] = s_t
        bm_ref[slot, g] = jnp.max(s_t, axis=0, keepdims=True)

    ones_rows = jnp.ones((BF16_ROWS, tk), BF16)

    def softmax_values(vb_t, slot, g):
        s_t = s_ref[slot, g]
        m_prev = m_ref[g]
        m_new = jnp.maximum(m_prev, bm_ref[slot, g])
        m_ref[g] = m_new
        alpha = jnp.exp2(m_prev - m_new)
        p_t = jnp.exp2(s_t - m_new).astype(BF16)
        acc_ref[g] = alpha * acc_ref[g] + _dot(vb_t, p_t)

    def load_keys(j):
        return k_ref[0, pl.ds(pl.multiple_of(j * tk, tk), tk), :]

    def load_values(j):
        vb_t = v_t_ref[0, :, pl.ds(pl.multiple_of(j * tk, tk), tk)]
        return jnp.concatenate([vb_t, ones_rows], axis=0)

    def stage(j, slot, last=False):
        kb = None if last else load_keys(j + 1)
        vb_t = load_values(j)
        for g in groups:
            if not last:
                scores(kb, 1 - slot, g)
            softmax_values(vb_t, slot, g)

    kb0 = load_keys(0)
    for g in groups:
        scores(kb0, 0, g)
    n_pairs = max(n_kblocks - 1, 0) // 2

    def pair(i, carry):
        stage(2 * i, 0)
        stage(2 * i + 1, 1)
        return carry

    lax.fori_loop(0, n_pairs, pair, 0)
    for j in range(2 * n_pairs, n_kblocks):
        stage(j, j % 2, last=j == n_kblocks - 1)

    lp = lam_ref[...]
    lam = (jnp.exp(jnp.sum(lp[0:1] * lp[1:2], axis=-1, keepdims=True))
           - jnp.exp(jnp.sum(lp[2:3] * lp[3:4], axis=-1, keepdims=True)) + lam_init)
    dv = v_t_ref.shape[1]
    acc = jnp.concatenate([acc_ref[g] for g in groups], axis=1)
    o_t = acc[:dv] / acc[dv:dv + 1]
    o_ref[0] = (o_t[:, :tq] - lam * o_t[:, tq:]).T.astype(BF16)


def _attn(dq_t, dk, dv_t, lam_p, *, seq, n_keys, lam_init):
    b_, lp, width = dk.shape
    tq, tk = ATTN_TQ, ATTN_TK
    heads = width // LANES
    n_groups = 2 * tq // ATTN_QGROUP
    kern = functools.partial(_attn_kernel, tk=tk, n_kblocks=n_keys // tk, lam_init=lam_init)
    return pl.pallas_call(
        kern,
        out_shape=jax.ShapeDtypeStruct((b_, seq, width), BF16),
        grid=(b_, heads, seq // tq),
        in_specs=[
            pl.BlockSpec((1, LANES, tq), lambda b, h, i: (b, h, i)),
            pl.BlockSpec((1, lp, LANES), lambda b, h, i: (b, 0, h)),
            pl.BlockSpec((1, LANES, lp), lambda b, h, i: (b, h, 0)),
            pl.BlockSpec(lam_p.shape, lambda b, h, i: (0, 0)),
        ],
        out_specs=pl.BlockSpec((1, tq, LANES), lambda b, h, i: (b, i, h)),
        scratch_shapes=[
            pltpu.VMEM((n_groups, 1, ATTN_QGROUP), F32),
            pltpu.VMEM((n_groups, LANES + BF16_ROWS, ATTN_QGROUP), F32),
            pltpu.VMEM((2, n_groups, tk, ATTN_QGROUP), F32),
            pltpu.VMEM((2, n_groups, 1, ATTN_QGROUP), F32),
        ],
        compiler_params=_params(("parallel", "parallel", "parallel")),
        name="diff_attn",
    )(dq_t, dk, dv_t, lam_p)


def _head_rms(x, gain, width):
    parts = []
    for h0 in range(0, x.shape[1], width):
        parts.append(_rms(x[:, h0:h0 + width]))
    return jnp.concatenate(parts, axis=1) * gain


def _merge_kernel(x_ref, mod_ref, of_ref, ob_ref, sg_ref, od_ref, sm_ref, gg_ref, gd_ref,
                  wa_ref, wb_ref, wo_ref, o_ref, *, dv_gla, dv_diff, diff_scale):
    d = x_ref.shape[2]
    o_gla = of_ref[0].astype(F32) + ob_ref[0].astype(F32)
    ya = _head_rms(o_gla, gg_ref[...], dv_gla) * sg_ref[0].astype(F32)
    ya = _dot(ya.astype(BF16), wa_ref[...])
    yb = _head_rms(od_ref[0].astype(F32), gd_ref[...], dv_diff) * diff_scale
    yb = _dot(yb.astype(BF16), wb_ref[...])
    sm = sm_ref[0].astype(F32)
    merged = (sm[:, :d] * ya + sm[:, d:] * yb).astype(BF16)
    gate = mod_ref[0, 0][5:6]
    o_ref[0] = x_ref[0] + gate * _dot(merged, wo_ref[...])


def _merge(x1, modt, o_f, o_b, sgg, o_d, smg, g_gla, g_diff, wa, wb, wo, *, n_lat_tiles, diff_scale):
    b_, _, d = x1.shape
    tm = TOKEN_TILE
    row = lambda width: pl.BlockSpec((1, tm, width), lambda b, j: (b, j, 0))
    kern = functools.partial(_merge_kernel, dv_gla=o_f.shape[2] // GLA_HEADS,
                             dv_diff=o_d.shape[2] // DIFF_HEADS, diff_scale=diff_scale)
    return pl.pallas_call(
        kern,
        out_shape=jax.ShapeDtypeStruct((b_, n_lat_tiles * tm, d), F32),
        grid=(b_, n_lat_tiles),
        in_specs=[
            row(d),
            pl.BlockSpec((1, 1, MOD_ROWS, d), lambda b, j: (b, 0, 0, 0)),
            row(o_f.shape[2]), row(o_b.shape[2]), row(sgg.shape[2]), row(o_d.shape[2]), row(smg.shape[2]),
            _resident(g_gla.shape), _resident(g_diff.shape),
            _resident(wa.shape), _resident(wb.shape), _resident(wo.shape),
        ],
        out_specs=row(d),
        compiler_params=_params(("parallel", "parallel")),
        name="merge",
    )(x1, modt, o_f, o_b, sgg, o_d, smg, g_gla, g_diff, wa, wb, wo)


def _rope_tables(seq, lp):
    half = DIFF_DH // 2
    quarter = half // 2
    inv = ROPE_BASE ** (-np.arange(quarter, dtype=np.float32) / quarter)
    t = jnp.arange(seq, dtype=jnp.int32)
    ang_row = (t // GRID_W).astype(F32)[:, None] * inv[None, :]
    ang_col = (t % GRID_W).astype(F32)[:, None] * inv[None, :]
    ang = jnp.concatenate([ang_row, ang_row, ang_col, ang_col], axis=1)
    sign = np.tile(np.concatenate([-np.ones(quarter), np.ones(quarter)]), 2).astype(np.float32)
    cos = jnp.tile(jnp.cos(ang), (1, LANES // DIFF_DH))
    sin = jnp.tile(jnp.sin(ang) * sign[None, :], (1, LANES // DIFF_DH))
    pad = lp - seq
    cos = jnp.concatenate([cos, jnp.ones((pad, LANES), F32)], axis=0)
    sin = jnp.concatenate([sin, jnp.zeros((pad, LANES), F32)], axis=0)
    tab_t = jnp.concatenate([jnp.cos(ang_row), jnp.sin(ang_row), jnp.cos(ang_col), jnp.sin(ang_col)], axis=1).T
    ident = np.repeat(np.array([1.0, 0.0, 1.0, 0.0], np.float32), quarter)[:, None]
    tab_t = jnp.concatenate([tab_t, jnp.broadcast_to(ident, (DIFF_DH, pad))], axis=1)
    return cos, sin, tab_t


def kernel(x, c, ctx, c_ctx, w_ada, b_ada, ffn1_norm, ffn1_w1, ffn1_w3, ffn1_w2, mix_norm, w_in, gla_gate_w2, gla_gate_b, gla_out_norm, diff_lambda, diff_out_norm, w_branch_gla, w_branch_diff, w_out, ffn2_norm, ffn2_w1, ffn2_w3, ffn2_w2, final_norm):
    b_, seq, d = x.shape
    n_ctx = ctx.shape[1]
    assert w_ada.shape[0] == 1, "single-layer stack"
    tm = TOKEN_TILE
    assert seq % tm == 0 and seq % ATTN_TQ == 0 and n_ctx <= tm
    assert seq % GLA_BLOCK == 0 and n_ctx % GLA_BLOCK == 0 and (seq + n_ctx) % ATTN_TK == 0
    n_lat_tiles = seq // tm
    n_tiles = n_lat_tiles + 1
    lp = n_tiles * tm
    lam_init = 0.8 - 0.6 * math.exp(0.0)

    gla_qk = GLA_HEADS * (d // 8)
    gla_v = GLA_HEADS * (d // 4)
    dif = DIFF_HEADS * 2 * DIFF_DH
    splits = np.cumsum([gla_qk, gla_qk, gla_v, gla_v, 2 * GLA_RANK, dif, dif, dif, 2 * d])
    assert w_in.shape[2] == splits[-1]
    wi = w_in[0].astype(BF16)
    w_gq, w_gk, w_gv, w_gg, w_lr, w_dq, w_dk, w_dv, w_mg = jnp.split(wi, splits[:-1].tolist(), axis=1)
    w_lr = jnp.pad(w_lr, ((0, 0), (0, LANES - 2 * GLA_RANK)))
    w2cat = jnp.zeros((LANES, 2 * gla_qk), F32)
    w2cat = w2cat.at[:GLA_RANK, :gla_qk].set(gla_gate_w2[0, 0])
    w2cat = w2cat.at[GLA_RANK:2 * GLA_RANK, gla_qk:].set(gla_gate_w2[0, 1]).astype(BF16)
    b2cat = gla_gate_b[0].reshape(1, 2 * gla_qk)

    rows = -(-(b_ + 1) // 8) * 8
    c_rows = jnp.concatenate([c, c_ctx[None, :], jnp.zeros((rows - b_ - 1, d), F32)], axis=0)
    mods = _mod_table(c_rows, w_ada[0], b_ada).reshape(rows, N_MOD, d)
    modt = jnp.stack([mods[:b_], jnp.broadcast_to(mods[b_], (b_, N_MOD, d))], axis=1)
    modt = jnp.pad(modt, ((0, 0), (0, 0), (0, MOD_ROWS - N_MOD), (0, 0)))

    ctx_tile = jnp.pad(ctx, ((0, 0), (0, tm - n_ctx), (0, 0)))
    tiles = dict(n_lat_tiles=n_lat_tiles, n_tiles=n_tiles)

    x1 = _ffn(x, modt, ffn1_norm, ffn1_w1[0].astype(BF16), ffn1_w3[0].astype(BF16),
              ffn1_w2[0].astype(BF16), row0=0, ctx_tile=ctx_tile, **tiles)

    gq, gk, gv, sgg, logf = _proj_gla(x1, modt, mix_norm, w_gq, w_gk, w_gv, w_gg, w_lr, w2cat, b2cat, **tiles)
    cos_t, sin_t, tab_t = _rope_tables(seq, lp)
    dq, dk, dv, smg = _proj_dif(x1, modt, mix_norm, cos_t, sin_t, tab_t, w_dq.T, w_dk, w_dv.T, w_mg, **tiles)

    o_f, o_b = _gla(gq, gk, gv, logf, n_lat_blocks=seq // GLA_BLOCK, n_ctx_blocks=n_ctx // GLA_BLOCK)

    lam_p = jnp.pad(diff_lambda[0], ((0, 4), (0, LANES - DIFF_DH)))
    o_d = _attn(dq, dk, dv, lam_p, seq=seq, n_keys=seq + n_ctx, lam_init=lam_init)

    x2 = _merge(x1, modt, o_f, o_b, sgg, o_d, smg, gla_out_norm, diff_out_norm,
                w_branch_gla[0].astype(BF16), w_branch_diff[0].astype(BF16), w_out[0].astype(BF16),
                n_lat_tiles=n_lat_tiles, diff_scale=1.0 - lam_init)

    return _ffn(x2, modt, ffn2_norm, ffn2_w1[0].astype(BF16), ffn2_w3[0].astype(BF16),
                ffn2_w2[0].astype(BF16), row0=6, n_lat_tiles=n_lat_tiles, n_tiles=n_lat_tiles,
                final_gain=final_norm.reshape(1, d))
```

```python
import functools
import math

import jax
import jax.numpy as jnp
import numpy as np
from jax import lax
from jax.experimental import pallas as pl
from jax.experimental.pallas import tpu as pltpu

F32 = jnp.float32
BF16 = jnp.bfloat16

EPS = 1e-6
GRID_W = 64
N_MOD = 9
MOD_ROWS = 16
GLA_HEADS = 4
GLA_RANK = 16
GLA_TAU = 16.0
GLA_CHUNK = 64
GLA_BLOCK = 256
DIFF_HEADS = 8
DIFF_DH = 64
ROPE_BASE = 10000.0
ROPE_PAIR = DIFF_DH // 4
LANES = 128
MXU_TILE = 256
BF16_ROWS = 16
LOG2_E = 1.4426950408889634

TOKEN_TILE = 512
ATTN_TQ = 2048
ATTN_TK = 768
ATTN_QGROUP = 256
V7X_VMEM_BYTES = 64 * 1024 * 1024
VMEM_LIMIT = V7X_VMEM_BYTES * 13 // 16
ATTN_VMEM_LIMIT = V7X_VMEM_BYTES * 7 // 8


def _resident(shape):
    zeros = (0,) * len(shape)
    return pl.BlockSpec(shape, lambda *_: zeros, pipeline_mode=pl.Buffered(1))


def _params(semantics):
    return pltpu.CompilerParams(dimension_semantics=semantics, vmem_limit_bytes=VMEM_LIMIT)


def _dot(a, b):
    return jnp.dot(a, b, preferred_element_type=F32)


def _dot_nt(a, b):
    return lax.dot_general(a, b, (((1,), (1,)), ((), ())), preferred_element_type=F32)


def _dot_tn(a, b):
    return lax.dot_general(a, b, (((0,), (0,)), ((), ())), preferred_element_type=F32)


def _sigmoid(x):
    return 1.0 / (1.0 + jnp.exp(-x))


def _rms(x):
    return x * lax.rsqrt(jnp.mean(x * x, axis=-1, keepdims=True) + EPS)


def _modnorm(x, gain, shift, scale1):
    return (_rms(x) * gain) * scale1 + shift


def _mod_kernel(c_ref, w_ref, b_ref, o_ref):
    k = pl.program_id(0)
    c = c_ref[...]
    s = c * _sigmoid(c)
    acc = jnp.zeros(o_ref.shape, F32)
    w = w_ref[...]
    s_parts = _bf16_terms(s, 3)
    w_parts = _bf16_terms(w, 3)
    for i in range(3):
        for j in range(3 - i):
            acc = acc + _dot(s_parts[i], w_parts[j])
    val = acc + b_ref[...]
    add = jnp.where(k % 3 == 1, 1.0, 0.0).astype(F32)
    mul = jnp.where((k == 2) | (k == 8), 0.5, 1.0).astype(F32)
    o_ref[...] = (val + add) * mul


def _bf16_terms(x, n):
    terms = []
    for _ in range(n):
        t = x.astype(BF16)
        terms.append(t)
        x = x - t.astype(F32)
    return terms


def _mod_table(c_rows, w_ada, b_ada):
    rows, d = c_rows.shape
    return pl.pallas_call(
        _mod_kernel,
        out_shape=jax.ShapeDtypeStruct((rows, N_MOD * d), F32),
        grid=(N_MOD,),
        in_specs=[
            pl.BlockSpec((rows, d), lambda k: (0, 0)),
            pl.BlockSpec((d, d), lambda k: (0, k)),
            pl.BlockSpec((1, d), lambda k: (0, k)),
        ],
        out_specs=pl.BlockSpec((rows, d), lambda k: (0, k)),
        compiler_params=_params(("arbitrary",)),
        name="mod_table",
    )(c_rows, w_ada, b_ada)


def _ffn_kernel(x_ref, mod_ref, g_ref, w1_ref, w3_ref, w2_ref, *rest, row0, f_chunk, n_lat_tiles, final):
    if final:
        gf_ref, o_ref = rest
        x = x_ref[0]
    else:
        c_ref, o_ref = rest
        x = jnp.where(pl.program_id(1) >= n_lat_tiles, c_ref[0], x_ref[0])
    mod = mod_ref[0, 0]
    shift, scale1, gate = mod[row0:row0 + 1], mod[row0 + 1:row0 + 2], mod[row0 + 2:row0 + 3]
    h = _modnorm(x, g_ref[...], shift, scale1).astype(BF16)
    acc = jnp.zeros(x.shape, F32)
    f_total = w1_ref.shape[1]
    for f0 in range(0, f_total, f_chunk):
        f1 = min(f0 + f_chunk, f_total)
        a = _dot(h, w1_ref[:, f0:f1])
        b = _dot(h, w3_ref[:, f0:f1])
        u = (a * _sigmoid(a) * b).astype(BF16)
        acc = acc + _dot(u, w2_ref[f0:f1, :])
    y = x + gate * acc
    if final:
        y = _rms(y) * gf_ref[...]
    o_ref[0] = y


def _ffn(x, modt, gain, w1, w3, w2, *, row0, n_lat_tiles, n_tiles, ctx_tile=None, final_gain=None):
    b_, _, d = x.shape
    f = w1.shape[1]
    tm = TOKEN_TILE
    final = final_gain is not None
    assert final != (ctx_tile is not None)
    f_chunk = -(-f // (2 * MXU_TILE)) * MXU_TILE
    kern = functools.partial(_ffn_kernel, row0=row0, f_chunk=f_chunk, n_lat_tiles=n_lat_tiles, final=final)
    in_specs = [
        pl.BlockSpec((1, tm, d), lambda b, j: (b, jnp.minimum(j, n_lat_tiles - 1), 0)),
        pl.BlockSpec((1, 1, MOD_ROWS, d), lambda b, j: (b, (j >= n_lat_tiles).astype(jnp.int32), 0, 0)),
        _resident((1, d)),
        _resident((d, f)),
        _resident((d, f)),
        _resident((f, d)),
    ]
    args = [x, modt, gain, w1, w3, w2]
    if final:
        in_specs.append(_resident((1, d)))
        args.append(final_gain)
    else:
        in_specs.append(pl.BlockSpec((1, tm, d), lambda b, j: (b, 0, 0)))
        args.append(ctx_tile)
    return pl.pallas_call(
        kern,
        out_shape=jax.ShapeDtypeStruct((b_, n_tiles * tm, d), F32),
        grid=(b_, n_tiles),
        in_specs=in_specs,
        out_specs=pl.BlockSpec((1, tm, d), lambda b, j: (b, j, 0)),
        compiler_params=_params(("parallel", "parallel")),
        name="ffn_final" if final else "ffn",
    )(*args)


def _run_pieces(pieces):
    pending = None
    for matmul, tail in pieces:
        result = matmul()
        if pending is not None:
            pending[0](pending[1])
        pending = (tail, result)
    pending[0](pending[1])


def _interleave(a, b):
    n = min(len(a), len(b))
    return [p for pair in zip(a, b) for p in pair] + a[n:] + b[n:]


def _column_pieces(lhs, w_ref, tail_of):
    def piece(cols):
        return (lambda: _dot(lhs, w_ref[:, cols])), tail_of(cols)
    return [piece(slice(c0, c0 + MXU_TILE)) for c0 in range(0, w_ref.shape[1], MXU_TILE)]


def _proj_gla_kernel(x_ref, mod_ref, g_ref, wq_ref, wk_ref, wv_ref, wg_ref, wlr_ref, w2_ref, b2_ref,
                     q_ref, k_ref, v_ref, sg_ref, lf_ref, *, q_scale):
    x = x_ref[0]
    mod = mod_ref[0, 0]
    h = _modnorm(x, g_ref[...], mod[3:4], mod[4:5]).astype(BF16)
    lr = _dot(h, wlr_ref[...]).astype(BF16)

    def decay_tail(cols):
        def tail(z):
            z = z + b2_ref[:, cols]
            lf_ref[0, :, cols] = (jnp.minimum(z, 0.0) - jnp.log(1.0 + jnp.exp(-jnp.abs(z)))) * (1.0 / GLA_TAU)
        return tail

    def gate_tail(cols):
        def tail(gg):
            sg_ref[0, :, cols] = (gg * _sigmoid(gg)).astype(BF16)
        return tail

    def value_tail(cols):
        def tail(v):
            v_ref[0, :, cols] = v.astype(BF16)
        return tail

    def query_tail(cols):
        def tail(q):
            q_ref[0, :, cols] = q * q_scale
        return tail

    def key_tail(cols):
        def tail(k):
            k_ref[0, :, cols] = k
        return tail

    heavy = _interleave(_column_pieces(lr, w2_ref, decay_tail), _column_pieces(h, wg_ref, gate_tail))
    light = (_column_pieces(h, wv_ref, value_tail) + _column_pieces(h, wq_ref, query_tail)
             + _column_pieces(h, wk_ref, key_tail))
    _run_pieces(_interleave(heavy, light))


def _proj_gla(xs, modt, gain, wq, wk, wv, wg, wlr, w2cat, b2cat, *, n_lat_tiles, n_tiles):
    b_, lp, d = xs.shape
    tm = TOKEN_TILE
    qk = wq.shape[1]
    vv = wv.shape[1]
    row = lambda width: pl.BlockSpec((1, tm, width), lambda b, j: (b, j, 0))
    kern = functools.partial(_proj_gla_kernel, q_scale=(qk // GLA_HEADS) ** -0.5)
    return pl.pallas_call(
        kern,
        out_shape=(
            jax.ShapeDtypeStruct((b_, lp, qk), F32),
            jax.ShapeDtypeStruct((b_, lp, qk), F32),
            jax.ShapeDtypeStruct((b_, lp, vv), BF16),
            jax.ShapeDtypeStruct((b_, lp, vv), BF16),
            jax.ShapeDtypeStruct((b_, lp, 2 * qk), F32),
        ),
        grid=(b_, n_tiles),
        in_specs=[
            row(d),
            pl.BlockSpec((1, 1, MOD_ROWS, d), lambda b, j: (b, (j >= n_lat_tiles).astype(jnp.int32), 0, 0)),
            _resident((1, d)),
            _resident(wq.shape), _resident(wk.shape), _resident(wv.shape), _resident(wg.shape),
            _resident(wlr.shape), _resident(w2cat.shape), _resident(b2cat.shape),
        ],
        out_specs=(row(qk), row(qk), row(vv), row(vv), row(2 * qk)),
        compiler_params=_params(("parallel", "parallel")),
        name="proj_gla",
    )(xs, modt, gain, wq, wk, wv, wg, wlr, w2cat, b2cat)


def _rope(u, cos, sin_signed, first_half):
    out = []
    for g in range(u.shape[1] // LANES):
        ug = u[:, g * LANES:(g + 1) * LANES]
        partner = jnp.where(first_half, pltpu.roll(ug, LANES - ROPE_PAIR, 1), pltpu.roll(ug, ROPE_PAIR, 1))
        out.append(ug * cos + partner * sin_signed)
    return jnp.concatenate(out, axis=1)


def _proj_dif_kernel(x_ref, mod_ref, g_ref, cos_ref, sin_ref, tab_t_ref, wq_t_ref, wk_ref, wv_t_ref, wm_ref,
                     q_t_ref, k_ref, v_t_ref, sm_ref, *, q_scale):
    x = x_ref[0]
    mod = mod_ref[0, 0]
    h = _modnorm(x, g_ref[...], mod[3:4], mod[4:5]).astype(BF16)
    quarter = DIFF_DH // 4
    tab = tab_t_ref[...]
    cos = cos_ref[...]
    sin_signed = sin_ref[...]
    lane = lax.broadcasted_iota(jnp.int32, cos.shape, 1)
    first_half = (lane % (2 * ROPE_PAIR)) < ROPE_PAIR

    t_rows = 2 * MXU_TILE

    def row_pieces(w_t_ref, tail_of):
        def piece(r0):
            return (lambda: _dot_nt(w_t_ref[r0:r0 + t_rows, :], h)), tail_of(r0)
        return [piece(r0) for r0 in range(0, w_t_ref.shape[0], t_rows)]

    def query_tail(r0):
        def tail(q_t):
            for base in range(0, t_rows, 2 * quarter):
                t0 = base % DIFF_DH
                c, s = tab[t0:t0 + quarter], tab[t0 + quarter:t0 + 2 * quarter]
                u1 = q_t[base:base + quarter]
                u2 = q_t[base + quarter:base + 2 * quarter]
                q_t_ref[0, r0 + base:r0 + base + quarter, :] = ((u1 * c - u2 * s) * q_scale).astype(BF16)
                q_t_ref[0, r0 + base + quarter:r0 + base + 2 * quarter, :] = (
                    (u2 * c + u1 * s) * q_scale).astype(BF16)
        return tail

    def key_tail(cols):
        def tail(k):
            k_ref[0, :, cols] = _rope(k, cos, sin_signed, first_half).astype(BF16)
        return tail

    def value_tail(r0):
        def tail(v_t):
            v_t_ref[0, r0:r0 + t_rows, :] = v_t.astype(BF16)
        return tail

    def gate_tail(cols):
        def tail(mg):
            sm_ref[0, :, cols] = _sigmoid(mg).astype(BF16)
        return tail

    rotary = _interleave(row_pieces(wq_t_ref, query_tail), _column_pieces(h, wk_ref, key_tail))
    plain = _interleave(_column_pieces(h, wm_ref, gate_tail), row_pieces(wv_t_ref, value_tail))
    _run_pieces(_interleave(rotary, plain))


def _proj_dif(xs, modt, gain, cos_t, sin_t, tab_t, wq_t, wk, wv_t, wm, *, n_lat_tiles, n_tiles):
    b_, lp, d = xs.shape
    tm = TOKEN_TILE
    row = lambda width: pl.BlockSpec((1, tm, width), lambda b, j: (b, j, 0))
    col = lambda height: pl.BlockSpec((1, height, tm), lambda b, j: (b, 0, j))
    tab = pl.BlockSpec((tm, LANES), lambda b, j: (j, 0))
    kern = functools.partial(_proj_dif_kernel, q_scale=DIFF_DH ** -0.5 * LOG2_E)
    return pl.pallas_call(
        kern,
        out_shape=(
            jax.ShapeDtypeStruct((b_, wq_t.shape[0], lp), BF16),
            jax.ShapeDtypeStruct((b_, lp, wk.shape[1]), BF16),
            jax.ShapeDtypeStruct((b_, wv_t.shape[0], lp), BF16),
            jax.ShapeDtypeStruct((b_, lp, wm.shape[1]), BF16),
        ),
        grid=(b_, n_tiles),
        in_specs=[
            row(d),
            pl.BlockSpec((1, 1, MOD_ROWS, d), lambda b, j: (b, (j >= n_lat_tiles).astype(jnp.int32), 0, 0)),
            _resident((1, d)),
            tab, tab,
            pl.BlockSpec((DIFF_DH, tm), lambda b, j: (0, j)),
            _resident(wq_t.shape), _resident(wk.shape), _resident(wv_t.shape), _resident(wm.shape),
        ],
        out_specs=(col(wq_t.shape[0]), row(wk.shape[1]), col(wv_t.shape[0]), row(wm.shape[1])),
        compiler_params=_params(("parallel", "parallel")),
        name="proj_dif",
    )(xs, modt, gain, cos_t, sin_t, tab_t, wq_t, wk, wv_t, wm)


def _gla_decays(q, k, logf, tri, ref_row, end_row):
    c = GLA_CHUNK
    hi, lo = _bf16_terms(logf, 2)
    cum = _dot(tri, hi) + _dot(tri, lo)
    n_chunks = q.shape[0] // c
    totals = [cum[a * c + end_row:a * c + end_row + 1] for a in range(n_chunks)]
    per_chunk = lambda rows: jnp.concatenate([jnp.broadcast_to(r, (c, r.shape[1])) for r in rows], axis=0)
    ref = per_chunk([cum[a * c + ref_row:a * c + ref_row + 1] for a in range(n_chunks)])
    total = per_chunk(totals)
    q_state = (q * jnp.exp(cum)).astype(BF16)
    q_in = (q * jnp.exp(cum - ref)).astype(BF16)
    k_in = (k * jnp.exp(ref - cum)).astype(BF16)
    k_out = (k * jnp.exp(total - cum)).astype(BF16)
    return q_state, q_in, k_in, k_out, [jnp.exp(t) for t in totals]


def _gla_kernel(qf_ref, kf_ref, vf_ref, lf_ref, qb_ref, kb_ref, vb_ref, lb_ref,
                of_ref, ob_ref, sf_ref, sb_ref):
    @pl.when(pl.program_id(1) == 0)
    def _():
        sf_ref[...] = jnp.zeros(sf_ref.shape, F32)
        sb_ref[...] = jnp.zeros(sb_ref.shape, F32)

    c = GLA_CHUNK
    rows = qf_ref.shape[1]
    n_chunks = rows // c
    row = lax.broadcasted_iota(jnp.int32, (rows, rows), 0)
    col = lax.broadcasted_iota(jnp.int32, (rows, rows), 1)
    same_chunk = (row // c) == (col // c)
    lower = same_chunk & (col <= row)
    upper = same_chunk & (col >= row)
    dirs = [
        (_gla_decays(qf_ref[0], kf_ref[0], lf_ref[0], jnp.where(lower, 1.0, 0.0).astype(BF16),
                     c // 2 - 1, c - 1), vf_ref[0], lower, sf_ref, of_ref, list(range(n_chunks))),
        (_gla_decays(qb_ref[0], kb_ref[0], lb_ref[0], jnp.where(upper, 1.0, 0.0).astype(BF16),
                     c // 2, 0), vb_ref[0], upper, sb_ref, ob_ref, list(range(n_chunks))[::-1]),
    ]
    dk = qf_ref.shape[2] // GLA_HEADS
    dv = vf_ref.shape[2] // GLA_HEADS
    chains = []
    for (q_state, q_in, k_in, k_out, decays), v, keep, s_ref, o_ref, order in dirs:
        for hd in range(GLA_HEADS):
            ks = slice(hd * dk, (hd + 1) * dk)
            vs = slice(hd * dv, (hd + 1) * dv)
            chains.append(dict(q_state=q_state[:, ks], q_in=q_in[:, ks], k_in=k_in[:, ks], k_out=k_out[:, ks],
                               decays=[d[:, ks] for d in decays], v=v[:, vs], keep=keep, s_ref=s_ref,
                               o_ref=o_ref, hd=hd, vs=vs, order=order))
    chunk = lambda x, a: x[a * c:(a + 1) * c]
    for ch in chains:
        ch["attn"] = _dot_nt(ch["q_in"], ch["k_in"])
    for ch in chains:
        ch["update"] = [_dot_tn(chunk(ch["v"], a), chunk(ch["k_out"], a)) for a in range(n_chunks)]
    for ch in chains:
        state_t = ch["s_ref"][ch["hd"]]
        ch["before"] = {}
        for a in ch["order"]:
            ch["before"][a] = state_t
            state_t = state_t * ch["decays"][a] + ch["update"][a]
        ch["s_ref"][ch["hd"]] = state_t
    for ch in chains:
        ch["inter"] = jnp.concatenate(
            [_dot_nt(chunk(ch["q_state"], a), ch["before"][a].astype(BF16)) for a in range(n_chunks)], axis=0)
    for ch in chains:
        attn = jnp.where(ch["keep"], ch["attn"], 0.0).astype(BF16)
        ch["o_ref"][0, :, ch["vs"]] = (ch["inter"] + _dot(attn, ch["v"])).astype(BF16)


def _gla(gq, gk, gv, logf, *, n_lat_blocks, n_ctx_blocks):
    b_, lp, qk = gq.shape
    vv = gv.shape[2]
    c = GLA_BLOCK
    n = n_lat_blocks + n_ctx_blocks
    fwd = lambda b, s: (b, (s + n_lat_blocks) % n, 0)
    bwd = lambda b, s: (b, n - 1 - s, 0)
    lf_spec = pl.BlockSpec((1, c, qk), lambda b, s: (b, (s + n_lat_blocks) % n, 0))
    lb_spec = pl.BlockSpec((1, c, qk), lambda b, s: (b, n - 1 - s, 1))
    return pl.pallas_call(
        _gla_kernel,
        out_shape=(jax.ShapeDtypeStruct((b_, n * c, vv), BF16), jax.ShapeDtypeStruct((b_, n * c, vv), BF16)),
        grid=(b_, n),
        in_specs=[
            pl.BlockSpec((1, c, qk), fwd), pl.BlockSpec((1, c, qk), fwd), pl.BlockSpec((1, c, vv), fwd), lf_spec,
            pl.BlockSpec((1, c, qk), bwd), pl.BlockSpec((1, c, qk), bwd), pl.BlockSpec((1, c, vv), bwd), lb_spec,
        ],
        out_specs=(pl.BlockSpec((1, c, vv), fwd), pl.BlockSpec((1, c, vv), bwd)),
        scratch_shapes=[
            pltpu.VMEM((GLA_HEADS, vv // GLA_HEADS, qk // GLA_HEADS), F32),
            pltpu.VMEM((GLA_HEADS, vv // GLA_HEADS, qk // GLA_HEADS), F32),
        ],
        compiler_params=_params(("parallel", "arbitrary")),
        name="gla_scan",
    )(gq, gk, gv, logf, gq, gk, gv, logf)


def _attn_kernel(q_t_ref, qn_t_ref, k_ref, kn_ref, v_t_ref, lam_ref, o_ref,
                 m_ref, acc_ref, s_ref, bm_ref, s0_ref, bm0_ref, *, tk, n_kblocks, lam_init):
    tq = q_t_ref.shape[2]

    def stacked_queries(ref):
        q_t = ref[0]
        row = lax.broadcasted_iota(jnp.int32, q_t.shape, 0)
        zero = jnp.zeros_like(q_t)
        return jnp.concatenate([jnp.where(row < DIFF_DH, q_t, zero), jnp.where(row >= DIFF_DH, q_t, zero)], axis=1)

    qz_t = stacked_queries(q_t_ref)
    m_ref[...] = jnp.full(m_ref.shape, -jnp.inf, F32)
    acc_ref[...] = jnp.zeros(acc_ref.shape, F32)

    groups = list(range(2 * tq // ATTN_QGROUP))
    lanes = lambda g: slice(g * ATTN_QGROUP, (g + 1) * ATTN_QGROUP)

    def scores(kb, q_all, s_out, bm_out, g):
        s_t = _dot(kb, q_all[:, lanes(g)])
        s_out[g] = s_t
        bm_out[g] = jnp.max(s_t, axis=0, keepdims=True)

    ones_rows = jnp.ones((BF16_ROWS, tk), BF16)

    def softmax_values(vb_t, s_in, bm_in, g):
        s_t = s_in[g]
        m_prev = m_ref[g]
        m_new = jnp.maximum(m_prev, bm_in[g])
        m_ref[g] = m_new
        alpha = jnp.exp2(m_prev - m_new)
        p_t = jnp.exp2(s_t - m_new).astype(BF16)
        acc_ref[g] = alpha * acc_ref[g] + _dot(vb_t, p_t)

    def load_keys(j):
        return k_ref[0, pl.ds(pl.multiple_of(j * tk, tk), tk), :]

    def load_values(j):
        vb_t = v_t_ref[0, :, pl.ds(pl.multiple_of(j * tk, tk), tk)]
        return jnp.concatenate([vb_t, ones_rows], axis=0)

    def stage(j, parity, last=False):
        s_in, bm_in = (s0_ref, bm0_ref) if parity is None else (s_ref.at[parity], bm_ref.at[parity])
        if last:
            kb, q_all, s_out, bm_out = kn_ref[0], stacked_queries(qn_t_ref), s0_ref, bm0_ref
        else:
            nxt = 1 if parity is None else 1 - parity
            kb, q_all, s_out, bm_out = load_keys(j + 1), qz_t, s_ref.at[nxt], bm_ref.at[nxt]
        vb_t = load_values(j)
        for g in groups:
            scores(kb, q_all, s_out, bm_out, g)
            softmax_values(vb_t, s_in, bm_in, g)

    @pl.when((pl.program_id(0) == 0) & (pl.program_id(1) == 0) & (pl.program_id(2) == 0))
    def _():
        kb0 = load_keys(0)
        for g in groups:
            scores(kb0, qz_t, s0_ref, bm0_ref, g)

    assert n_kblocks >= 2
    stage(0, None)
    n_pairs = (n_kblocks - 2) // 2

    def pair(i, carry):
        stage(2 * i + 1, 1)
        stage(2 * i + 2, 0)
        return carry

    lax.fori_loop(0, n_pairs, pair, 0)
    for j in range(2 * n_pairs + 1, n_kblocks):
        stage(j, j % 2, last=j == n_kblocks - 1)

    lp = lam_ref[...]
    lam = (jnp.exp(jnp.sum(lp[0:1] * lp[1:2], axis=-1, keepdims=True))
           - jnp.exp(jnp.sum(lp[2:3] * lp[3:4], axis=-1, keepdims=True)) + lam_init)
    dv = v_t_ref.shape[1]
    acc = jnp.concatenate([acc_ref[g] for g in groups], axis=1)
    o_t = acc[:dv] / acc[dv:dv + 1]
    o_ref[0] = (o_t[:, :tq] - lam * o_t[:, tq:]).T.astype(BF16)


def _attn(dq_t, dk, dv_t, lam_p, *, seq, n_keys, lam_init):
    b_, lp, width = dk.shape
    tq, tk = ATTN_TQ, ATTN_TK
    heads = width // LANES
    n_q = seq // tq
    n_groups = 2 * tq // ATTN_QGROUP
    kern = functools.partial(_attn_kernel, tk=tk, n_kblocks=n_keys // tk, lam_init=lam_init)

    def following(b, h, i):
        n = jnp.minimum((b * heads + h) * n_q + i + 1, b_ * heads * n_q - 1)
        return n // (heads * n_q), (n // n_q) % heads, n % n_q

    def next_keys(b, h, i):
        b2, h2, _ = following(b, h, i)
        return b2, 0, h2

    return pl.pallas_call(
        kern,
        out_shape=jax.ShapeDtypeStruct((b_, seq, width), BF16),
        grid=(b_, heads, n_q),
        in_specs=[
            pl.BlockSpec((1, LANES, tq), lambda b, h, i: (b, h, i)),
            pl.BlockSpec((1, LANES, tq), following),
            pl.BlockSpec((1, lp, LANES), lambda b, h, i: (b, 0, h)),
            pl.BlockSpec((1, tk, LANES), next_keys),
            pl.BlockSpec((1, LANES, lp), lambda b, h, i: (b, h, 0)),
            pl.BlockSpec(lam_p.shape, lambda b, h, i: (0, 0)),
        ],
        out_specs=pl.BlockSpec((1, tq, LANES), lambda b, h, i: (b, i, h)),
        scratch_shapes=[
            pltpu.VMEM((n_groups, 1, ATTN_QGROUP), F32),
            pltpu.VMEM((n_groups, LANES + BF16_ROWS, ATTN_QGROUP), F32),
            pltpu.VMEM((2, n_groups, tk, ATTN_QGROUP), F32),
            pltpu.VMEM((2, n_groups, 1, ATTN_QGROUP), F32),
            pltpu.VMEM((n_groups, tk, ATTN_QGROUP), F32),
            pltpu.VMEM((n_groups, 1, ATTN_QGROUP), F32),
        ],
        compiler_params=pltpu.CompilerParams(dimension_semantics=("arbitrary", "arbitrary", "arbitrary"),
                                             vmem_limit_bytes=ATTN_VMEM_LIMIT),
        name="diff_attn",
    )(dq_t, dq_t, dk, dk, dv_t, lam_p)


def _head_rms(x, gain, width):
    parts = []
    for h0 in range(0, x.shape[1], width):
        parts.append(_rms(x[:, h0:h0 + width]))
    return jnp.concatenate(parts, axis=1) * gain


def _merge_kernel(x_ref, mod_ref, of_ref, ob_ref, sg_ref, od_ref, sm_ref, gg_ref, gd_ref,
                  wa_ref, wb_ref, wo_ref, o_ref, *, dv_gla, dv_diff, diff_scale):
    d = x_ref.shape[2]
    o_gla = of_ref[0].astype(F32) + ob_ref[0].astype(F32)
    ya = _head_rms(o_gla, gg_ref[...], dv_gla) * sg_ref[0].astype(F32)
    ya = _dot(ya.astype(BF16), wa_ref[...])
    yb = _head_rms(od_ref[0].astype(F32), gd_ref[...], dv_diff) * diff_scale
    yb = _dot(yb.astype(BF16), wb_ref[...])
    sm = sm_ref[0].astype(F32)
    merged = (sm[:, :d] * ya + sm[:, d:] * yb).astype(BF16)
    gate = mod_ref[0, 0][5:6]
    o_ref[0] = x_ref[0] + gate * _dot(merged, wo_ref[...])


def _merge(x1, modt, o_f, o_b, sgg, o_d, smg, g_gla, g_diff, wa, wb, wo, *, n_lat_tiles, diff_scale):
    b_, _, d = x1.shape
    tm = TOKEN_TILE
    row = lambda width: pl.BlockSpec((1, tm, width), lambda b, j: (b, j, 0))
    kern = functools.partial(_merge_kernel, dv_gla=o_f.shape[2] // GLA_HEADS,
                             dv_diff=o_d.shape[2] // DIFF_HEADS, diff_scale=diff_scale)
    return pl.pallas_call(
        kern,
        out_shape=jax.ShapeDtypeStruct((b_, n_lat_tiles * tm, d), F32),
        grid=(b_, n_lat_tiles),
        in_specs=[
            row(d),
            pl.BlockSpec((1, 1, MOD_ROWS, d), lambda b, j: (b, 0, 0, 0)),
            row(o_f.shape[2]), row(o_b.shape[2]), row(sgg.shape[2]), row(o_d.shape[2]), row(smg.shape[2]),
            _resident(g_gla.shape), _resident(g_diff.shape),
            _resident(wa.shape), _resident(wb.shape), _resident(wo.shape),
        ],
        out_specs=row(d),
        compiler_params=_params(("parallel", "parallel")),
        name="merge",
    )(x1, modt, o_f, o_b, sgg, o_d, smg, g_gla, g_diff, wa, wb, wo)


def _rope_tables(seq, lp):
    half = DIFF_DH // 2
    quarter = half // 2
    inv = ROPE_BASE ** (-np.arange(quarter, dtype=np.float32) / quarter)
    t = jnp.arange(seq, dtype=jnp.int32)
    ang_row = (t // GRID_W).astype(F32)[:, None] * inv[None, :]
    ang_col = (t % GRID_W).astype(F32)[:, None] * inv[None, :]
    ang = jnp.concatenate([ang_row, ang_row, ang_col, ang_col], axis=1)
    sign = np.tile(np.concatenate([-np.ones(quarter), np.ones(quarter)]), 2).astype(np.float32)
    cos = jnp.tile(jnp.cos(ang), (1, LANES // DIFF_DH))
    sin = jnp.tile(jnp.sin(ang) * sign[None, :], (1, LANES // DIFF_DH))
    pad = lp - seq
    cos = jnp.concatenate([cos, jnp.ones((pad, LANES), F32)], axis=0)
    sin = jnp.concatenate([sin, jnp.zeros((pad, LANES), F32)], axis=0)
    tab_t = jnp.concatenate([jnp.cos(ang_row), jnp.sin(ang_row), jnp.cos(ang_col), jnp.sin(ang_col)], axis=1).T
    ident = np.repeat(np.array([1.0, 0.0, 1.0, 0.0], np.float32), quarter)[:, None]
    tab_t = jnp.concatenate([tab_t, jnp.broadcast_to(ident, (DIFF_DH, pad))], axis=1)
    return cos, sin, tab_t


def kernel(x, c, ctx, c_ctx, w_ada, b_ada, ffn1_norm, ffn1_w1, ffn1_w3, ffn1_w2, mix_norm, w_in, gla_gate_w2, gla_gate_b, gla_out_norm, diff_lambda, diff_out_norm, w_branch_gla, w_branch_diff, w_out, ffn2_norm, ffn2_w1, ffn2_w3, ffn2_w2, final_norm):
    b_, seq, d = x.shape
    n_ctx = ctx.shape[1]
    assert w_ada.shape[0] == 1, "single-layer stack"
    tm = TOKEN_TILE
    assert seq % tm == 0 and seq % ATTN_TQ == 0 and n_ctx <= tm
    assert seq % GLA_BLOCK == 0 and n_ctx % GLA_BLOCK == 0 and (seq + n_ctx) % ATTN_TK == 0
    n_lat_tiles = seq // tm
    n_tiles = n_lat_tiles + 1
    lp = n_tiles * tm
    lam_init = 0.8 - 0.6 * math.exp(0.0)

    gla_qk = GLA_HEADS * (d // 8)
    gla_v = GLA_HEADS * (d // 4)
    dif = DIFF_HEADS * 2 * DIFF_DH
    splits = np.cumsum([gla_qk, gla_qk, gla_v, gla_v, 2 * GLA_RANK, dif, dif, dif, 2 * d])
    assert w_in.shape[2] == splits[-1]
    wi = w_in[0].astype(BF16)
    w_gq, w_gk, w_gv, w_gg, w_lr, w_dq, w_dk, w_dv, w_mg = jnp.split(wi, splits[:-1].tolist(), axis=1)
    w_lr = jnp.pad(w_lr, ((0, 0), (0, LANES - 2 * GLA_RANK)))
    w2cat = jnp.zeros((LANES, 2 * gla_qk), F32)
    w2cat = w2cat.at[:GLA_RANK, :gla_qk].set(gla_gate_w2[0, 0])
    w2cat = w2cat.at[GLA_RANK:2 * GLA_RANK, gla_qk:].set(gla_gate_w2[0, 1]).astype(BF16)
    b2cat = gla_gate_b[0].reshape(1, 2 * gla_qk)

    rows = -(-(b_ + 1) // 8) * 8
    c_rows = jnp.concatenate([c, c_ctx[None, :], jnp.zeros((rows - b_ - 1, d), F32)], axis=0)
    mods = _mod_table(c_rows, w_ada[0], b_ada).reshape(rows, N_MOD, d)
    modt = jnp.stack([mods[:b_], jnp.broadcast_to(mods[b_], (b_, N_MOD, d))], axis=1)
    modt = jnp.pad(modt, ((0, 0), (0, 0), (0, MOD_ROWS - N_MOD), (0, 0)))

    ctx_tile = jnp.pad(ctx, ((0, 0), (0, tm - n_ctx), (0, 0)))
    tiles = dict(n_lat_tiles=n_lat_tiles, n_tiles=n_tiles)

    x1 = _ffn(x, modt, ffn1_norm, ffn1_w1[0].astype(BF16), ffn1_w3[0].astype(BF16),
              ffn1_w2[0].astype(BF16), row0=0, ctx_tile=ctx_tile, **tiles)

    gq, gk, gv, sgg, logf = _proj_gla(x1, modt, mix_norm, w_gq, w_gk, w_gv, w_gg, w_lr, w2cat, b2cat, **tiles)
    cos_t, sin_t, tab_t = _rope_tables(seq, lp)
    dq, dk, dv, smg = _proj_dif(x1, modt, mix_norm, cos_t, sin_t, tab_t, w_dq.T, w_dk, w_dv.T, w_mg, **tiles)

    o_f, o_b = _gla(gq, gk, gv, logf, n_lat_blocks=seq // GLA_BLOCK, n_ctx_blocks=n_ctx // GLA_BLOCK)

    lam_p = jnp.pad(diff_lambda[0], ((0, 4), (0, LANES - DIFF_DH)))
    o_d = _attn(dq, dk, dv, lam_p, seq=seq, n_keys=seq + n_ctx, lam_init=lam_init)

    x2 = _merge(x1, modt, o_f, o_b, sgg, o_d, smg, gla_out_norm, diff_out_norm,
                w_branch_gla[0].astype(BF16), w_branch_diff[0].astype(BF16), w_out[0].astype(BF16),
                n_lat_tiles=n_lat_tiles, diff_scale=1.0 - lam_init)

    return _ffn(x2, modt, ffn2_norm, ffn2_w1[0].astype(BF16), ffn2_w3[0].astype(BF16),
                ffn2_w2[0].astype(BF16), row0=6, n_lat_tiles=n_lat_tiles, n_tiles=n_lat_tiles,
                final_gain=final_norm.reshape(1, d))
```

```python
import functools
import math

import jax
import jax.numpy as jnp
import numpy as np
from jax import lax
from jax.experimental import pallas as pl
from jax.experimental.pallas import tpu as pltpu

F32 = jnp.float32
BF16 = jnp.bfloat16

EPS = 1e-6
GRID_W = 64
N_MOD = 9
MOD_ROWS = 16
GLA_HEADS = 4
GLA_RANK = 16
GLA_TAU = 16.0
GLA_CHUNK = 64
GLA_BLOCK = 256
DIFF_HEADS = 8
DIFF_DH = 64
ROPE_BASE = 10000.0
ROPE_PAIR = DIFF_DH // 4
LANES = 128
MXU_TILE = 256
BF16_ROWS = 16
LOG2_E = 1.4426950408889634

TOKEN_TILE = 512
ATTN_TQ = 2048
ATTN_TK = 768
ATTN_QGROUP = 256
V7X_VMEM_BYTES = 64 * 1024 * 1024
VMEM_LIMIT = V7X_VMEM_BYTES * 13 // 16
ATTN_VMEM_LIMIT = V7X_VMEM_BYTES * 7 // 8


def _resident(shape):
    zeros = (0,) * len(shape)
    return pl.BlockSpec(shape, lambda *_: zeros, pipeline_mode=pl.Buffered(1))


def _params(semantics):
    return pltpu.CompilerParams(dimension_semantics=semantics, vmem_limit_bytes=VMEM_LIMIT)


def _dot(a, b):
    return jnp.dot(a, b, preferred_element_type=F32)


def _dot_nt(a, b):
    return lax.dot_general(a, b, (((1,), (1,)), ((), ())), preferred_element_type=F32)


def _dot_tn(a, b):
    return lax.dot_general(a, b, (((0,), (0,)), ((), ())), preferred_element_type=F32)


def _sigmoid(x):
    return 1.0 / (1.0 + jnp.exp(-x))


def _rms(x):
    return x * lax.rsqrt(jnp.mean(x * x, axis=-1, keepdims=True) + EPS)


def _modnorm(x, gain, shift, scale1):
    return (_rms(x) * gain) * scale1 + shift


def _mod_kernel(c_ref, w_ref, b_ref, o_ref):
    k = pl.program_id(0)
    c = c_ref[...]
    s = c * _sigmoid(c)
    acc = jnp.zeros(o_ref.shape, F32)
    w = w_ref[...]
    s_parts = _bf16_terms(s, 3)
    w_parts = _bf16_terms(w, 3)
    for i in range(3):
        for j in range(3 - i):
            acc = acc + _dot(s_parts[i], w_parts[j])
    val = acc + b_ref[...]
    add = jnp.where(k % 3 == 1, 1.0, 0.0).astype(F32)
    mul = jnp.where((k == 2) | (k == 8), 0.5, 1.0).astype(F32)
    o_ref[...] = (val + add) * mul


def _bf16_terms(x, n):
    terms = []
    for _ in range(n):
        t = x.astype(BF16)
        terms.append(t)
        x = x - t.astype(F32)
    return terms


def _mod_table(c_rows, w_ada, b_ada):
    rows, d = c_rows.shape
    return pl.pallas_call(
        _mod_kernel,
        out_shape=jax.ShapeDtypeStruct((rows, N_MOD * d), F32),
        grid=(N_MOD,),
        in_specs=[
            pl.BlockSpec((rows, d), lambda k: (0, 0)),
            pl.BlockSpec((d, d), lambda k: (0, k)),
            pl.BlockSpec((1, d), lambda k: (0, k)),
        ],
        out_specs=pl.BlockSpec((rows, d), lambda k: (0, k)),
        compiler_params=_params(("arbitrary",)),
        name="mod_table",
    )(c_rows, w_ada, b_ada)


def _ffn_kernel(x_ref, mod_ref, g_ref, w1_ref, w3_ref, w2_ref, *rest, row0, f_chunk, n_lat_tiles, final):
    if final:
        gf_ref, o_ref = rest
        x = x_ref[0]
    else:
        c_ref, o_ref = rest
        x = jnp.where(pl.program_id(1) >= n_lat_tiles, c_ref[0], x_ref[0])
    mod = mod_ref[0, 0]
    shift, scale1, gate = mod[row0:row0 + 1], mod[row0 + 1:row0 + 2], mod[row0 + 2:row0 + 3]
    h = _modnorm(x, g_ref[...], shift, scale1).astype(BF16)
    acc = jnp.zeros(x.shape, F32)
    f_total = w1_ref.shape[1]
    for f0 in range(0, f_total, f_chunk):
        f1 = min(f0 + f_chunk, f_total)
        a = _dot(h, w1_ref[:, f0:f1])
        b = _dot(h, w3_ref[:, f0:f1])
        u = (a * _sigmoid(a) * b).astype(BF16)
        acc = acc + _dot(u, w2_ref[f0:f1, :])
    y = x + gate * acc
    if final:
        y = _rms(y) * gf_ref[...]
    o_ref[0] = y


def _ffn(x, modt, gain, w1, w3, w2, *, row0, n_lat_tiles, n_tiles, ctx_tile=None, final_gain=None):
    b_, _, d = x.shape
    f = w1.shape[1]
    tm = TOKEN_TILE
    final = final_gain is not None
    assert final != (ctx_tile is not None)
    f_chunk = -(-f // (2 * MXU_TILE)) * MXU_TILE
    kern = functools.partial(_ffn_kernel, row0=row0, f_chunk=f_chunk, n_lat_tiles=n_lat_tiles, final=final)
    in_specs = [
        pl.BlockSpec((1, tm, d), lambda b, j: (b, jnp.minimum(j, n_lat_tiles - 1), 0)),
        pl.BlockSpec((1, 1, MOD_ROWS, d), lambda b, j: (b, (j >= n_lat_tiles).astype(jnp.int32), 0, 0)),
        _resident((1, d)),
        _resident((d, f)),
        _resident((d, f)),
        _resident((f, d)),
    ]
    args = [x, modt, gain, w1, w3, w2]
    if final:
        in_specs.append(_resident((1, d)))
        args.append(final_gain)
    else:
        in_specs.append(pl.BlockSpec((1, tm, d), lambda b, j: (b, 0, 0)))
        args.append(ctx_tile)
    return pl.pallas_call(
        kern,
        out_shape=jax.ShapeDtypeStruct((b_, n_tiles * tm, d), F32),
        grid=(b_, n_tiles),
        in_specs=in_specs,
        out_specs=pl.BlockSpec((1, tm, d), lambda b, j: (b, j, 0)),
        compiler_params=_params(("parallel", "parallel")),
        name="ffn_final" if final else "ffn",
    )(*args)


def _run_pieces(pieces):
    pending = None
    for matmul, tail in pieces:
        result = matmul()
        if pending is not None:
            pending[0](pending[1])
        pending = (tail, result)
    pending[0](pending[1])


def _interleave(a, b):
    n = min(len(a), len(b))
    return [p for pair in zip(a, b) for p in pair] + a[n:] + b[n:]


def _column_pieces(lhs, w_ref, tail_of):
    def piece(cols):
        return (lambda: _dot(lhs, w_ref[:, cols])), tail_of(cols)
    return [piece(slice(c0, c0 + MXU_TILE)) for c0 in range(0, w_ref.shape[1], MXU_TILE)]


def _gla_pieces(h, wq_ref, wk_ref, wv_ref, wg_ref, wlr_ref, w2_ref, b2_ref,
                q_ref, k_ref, v_ref, sg_ref, lf_ref, *, q_scale):
    lr = _dot(h, wlr_ref[...]).astype(BF16)

    def decay_tail(cols):
        def tail(z):
            z = z + b2_ref[:, cols]
            lf_ref[0, :, cols] = (jnp.minimum(z, 0.0) - jnp.log(1.0 + jnp.exp(-jnp.abs(z)))) * (1.0 / GLA_TAU)
        return tail

    def gate_tail(cols):
        def tail(gg):
            sg_ref[0, :, cols] = (gg * _sigmoid(gg)).astype(BF16)
        return tail

    def value_tail(cols):
        def tail(v):
            v_ref[0, :, cols] = v.astype(BF16)
        return tail

    def query_tail(cols):
        def tail(q):
            q_ref[0, :, cols] = q * q_scale
        return tail

    def key_tail(cols):
        def tail(k):
            k_ref[0, :, cols] = k
        return tail

    heavy = _interleave(_column_pieces(lr, w2_ref, decay_tail), _column_pieces(h, wg_ref, gate_tail))
    light = (_column_pieces(h, wv_ref, value_tail) + _column_pieces(h, wq_ref, query_tail)
             + _column_pieces(h, wk_ref, key_tail))
    return _interleave(heavy, light)


def _rope(u, cos, sin_signed, first_half):
    out = []
    for g in range(u.shape[1] // LANES):
        ug = u[:, g * LANES:(g + 1) * LANES]
        partner = jnp.where(first_half, pltpu.roll(ug, LANES - ROPE_PAIR, 1), pltpu.roll(ug, ROPE_PAIR, 1))
        out.append(ug * cos + partner * sin_signed)
    return jnp.concatenate(out, axis=1)


def _dif_pieces(h, cos_ref, sin_ref, tab_t_ref, wq_t_ref, wk_ref, wv_t_ref, wm_ref,
                q_t_ref, k_ref, v_t_ref, sm_ref, *, q_scale):
    quarter = DIFF_DH // 4
    tab = tab_t_ref[...]
    cos = cos_ref[...]
    sin_signed = sin_ref[...]
    lane = lax.broadcasted_iota(jnp.int32, cos.shape, 1)
    first_half = (lane % (2 * ROPE_PAIR)) < ROPE_PAIR

    t_rows = 2 * MXU_TILE

    def row_pieces(w_t_ref, tail_of):
        def piece(r0):
            return (lambda: _dot_nt(w_t_ref[r0:r0 + t_rows, :], h)), tail_of(r0)
        return [piece(r0) for r0 in range(0, w_t_ref.shape[0], t_rows)]

    def query_tail(r0):
        def tail(q_t):
            for base in range(0, t_rows, 2 * quarter):
                t0 = base % DIFF_DH
                c, s = tab[t0:t0 + quarter], tab[t0 + quarter:t0 + 2 * quarter]
                u1 = q_t[base:base + quarter]
                u2 = q_t[base + quarter:base + 2 * quarter]
                q_t_ref[0, r0 + base:r0 + base + quarter, :] = ((u1 * c - u2 * s) * q_scale).astype(BF16)
                q_t_ref[0, r0 + base + quarter:r0 + base + 2 * quarter, :] = (
                    (u2 * c + u1 * s) * q_scale).astype(BF16)
        return tail

    def key_tail(cols):
        def tail(k):
            k_ref[0, :, cols] = _rope(k, cos, sin_signed, first_half).astype(BF16)
        return tail

    def value_tail(r0):
        def tail(v_t):
            v_t_ref[0, r0:r0 + t_rows, :] = v_t.astype(BF16)
        return tail

    def gate_tail(cols):
        def tail(mg):
            sm_ref[0, :, cols] = _sigmoid(mg).astype(BF16)
        return tail

    rotary = _interleave(row_pieces(wq_t_ref, query_tail), _column_pieces(h, wk_ref, key_tail))
    plain = _interleave(_column_pieces(h, wm_ref, gate_tail), row_pieces(wv_t_ref, value_tail))
    return _interleave(rotary, plain)


N_GLA_IN, N_GLA_OUT, N_DIF_IN, N_DIF_OUT = 7, 5, 7, 4


def _proj_kernel(x_ref, mod_ref, g_ref, *refs, gla_q_scale, dif_q_scale):
    x = x_ref[0]
    mod = mod_ref[0, 0]
    h = _modnorm(x, g_ref[...], mod[3:4], mod[4:5]).astype(BF16)
    gla_in, refs = refs[:N_GLA_IN], refs[N_GLA_IN:]
    dif_in, refs = refs[:N_DIF_IN], refs[N_DIF_IN:]
    gla_out, dif_out = refs[:N_GLA_OUT], refs[N_GLA_OUT:]
    assert len(dif_out) == N_DIF_OUT
    _run_pieces(_interleave(_gla_pieces(h, *gla_in, *gla_out, q_scale=gla_q_scale),
                            _dif_pieces(h, *dif_in, *dif_out, q_scale=dif_q_scale)))


def _proj(xs, modt, gain, gla_w, dif_tabs, dif_w, *, n_lat_tiles, n_tiles):
    b_, lp, d = xs.shape
    tm = TOKEN_TILE
    wq, _, wv = gla_w[:3]
    qk, vv = wq.shape[1], wv.shape[1]
    wq_t, wk_d, wv_t, wm = dif_w
    assert len(gla_w) == N_GLA_IN and len(dif_tabs) + len(dif_w) == N_DIF_IN
    row = lambda width: pl.BlockSpec((1, tm, width), lambda b, j: (b, j, 0))
    col = lambda height: pl.BlockSpec((1, height, tm), lambda b, j: (b, 0, j))
    tab = pl.BlockSpec((tm, LANES), lambda b, j: (j, 0))
    kern = functools.partial(_proj_kernel, gla_q_scale=(qk // GLA_HEADS) ** -0.5,
                             dif_q_scale=DIFF_DH ** -0.5 * LOG2_E)
    return pl.pallas_call(
        kern,
        out_shape=(
            jax.ShapeDtypeStruct((b_, lp, qk), F32),
            jax.ShapeDtypeStruct((b_, lp, qk), F32),
            jax.ShapeDtypeStruct((b_, lp, vv), BF16),
            jax.ShapeDtypeStruct((b_, lp, vv), BF16),
            jax.ShapeDtypeStruct((b_, lp, 2 * qk), F32),
            jax.ShapeDtypeStruct((b_, wq_t.shape[0], lp), BF16),
            jax.ShapeDtypeStruct((b_, lp, wk_d.shape[1]), BF16),
            jax.ShapeDtypeStruct((b_, wv_t.shape[0], lp), BF16),
            jax.ShapeDtypeStruct((b_, lp, wm.shape[1]), BF16),
        ),
        grid=(b_, n_tiles),
        in_specs=[
            row(d),
            pl.BlockSpec((1, 1, MOD_ROWS, d), lambda b, j: (b, (j >= n_lat_tiles).astype(jnp.int32), 0, 0)),
            _resident((1, d)),
            *[_resident(w.shape) for w in gla_w],
            tab, tab, pl.BlockSpec((DIFF_DH, tm), lambda b, j: (0, j)),
            *[_resident(w.shape) for w in dif_w],
        ],
        out_specs=(row(qk), row(qk), row(vv), row(vv), row(2 * qk),
                   col(wq_t.shape[0]), row(wk_d.shape[1]), col(wv_t.shape[0]), row(wm.shape[1])),
        compiler_params=_params(("parallel", "parallel")),
        name="proj",
    )(xs, modt, gain, *gla_w, *dif_tabs, *dif_w)


def _gla_decays(q, k, logf, tri, ref_row, end_row):
    c = GLA_CHUNK
    hi, lo = _bf16_terms(logf, 2)
    cum = _dot(tri, hi) + _dot(tri, lo)
    n_chunks = q.shape[0] // c
    totals = [cum[a * c + end_row:a * c + end_row + 1] for a in range(n_chunks)]
    per_chunk = lambda rows: jnp.concatenate([jnp.broadcast_to(r, (c, r.shape[1])) for r in rows], axis=0)
    ref = per_chunk([cum[a * c + ref_row:a * c + ref_row + 1] for a in range(n_chunks)])
    total = per_chunk(totals)
    q_state = (q * jnp.exp(cum)).astype(BF16)
    q_in = (q * jnp.exp(cum - ref)).astype(BF16)
    k_in = (k * jnp.exp(ref - cum)).astype(BF16)
    k_out = (k * jnp.exp(total - cum)).astype(BF16)
    return q_state, q_in, k_in, k_out, [jnp.exp(t) for t in totals]


def _gla_chains(chains, chunk, n_chunks):
    for ch in chains:
        ch["attn"] = _dot_nt(ch["q_in"], ch["k_in"])
    for ch in chains:
        ch["update"] = [_dot_tn(chunk(ch["v"], a), chunk(ch["k_out"], a)) for a in range(n_chunks)]
    for ch in chains:
        state_t = ch["s_ref"][ch["hd"]]
        ch["before"] = {}
        for a in ch["order"]:
            ch["before"][a] = state_t
            state_t = state_t * ch["decays"][a] + ch["update"][a]
        ch["s_ref"][ch["hd"]] = state_t
    for ch in chains:
        ch["inter"] = jnp.concatenate(
            [_dot_nt(chunk(ch["q_state"], a), ch["before"][a].astype(BF16)) for a in range(n_chunks)], axis=0)
    for ch in chains:
        attn = jnp.where(ch["keep"], ch["attn"], 0.0).astype(BF16)
        ch["o_ref"][0, :, ch["vs"]] = (ch["inter"] + _dot(attn, ch["v"])).astype(BF16)


def _gla_kernel(qf_ref, kf_ref, vf_ref, lf_ref, qb_ref, kb_ref, vb_ref, lb_ref,
                of_ref, ob_ref, sf_ref, sb_ref):
    @pl.when(pl.program_id(1) == 0)
    def _():
        sf_ref[...] = jnp.zeros(sf_ref.shape, F32)
        sb_ref[...] = jnp.zeros(sb_ref.shape, F32)

    c = GLA_CHUNK
    rows = qf_ref.shape[1]
    n_chunks = rows // c
    row = lax.broadcasted_iota(jnp.int32, (rows, rows), 0)
    col = lax.broadcasted_iota(jnp.int32, (rows, rows), 1)
    same_chunk = (row // c) == (col // c)
    lower = same_chunk & (col <= row)
    upper = same_chunk & (col >= row)
    dirs = [
        (_gla_decays(qf_ref[0], kf_ref[0], lf_ref[0], jnp.where(lower, 1.0, 0.0).astype(BF16),
                     c // 2 - 1, c - 1), vf_ref[0], lower, sf_ref, of_ref, list(range(n_chunks))),
        (_gla_decays(qb_ref[0], kb_ref[0], lb_ref[0], jnp.where(upper, 1.0, 0.0).astype(BF16),
                     c // 2, 0), vb_ref[0], upper, sb_ref, ob_ref, list(range(n_chunks))[::-1]),
    ]
    dk = qf_ref.shape[2] // GLA_HEADS
    dv = vf_ref.shape[2] // GLA_HEADS
    chains = []
    for (q_state, q_in, k_in, k_out, decays), v, keep, s_ref, o_ref, order in dirs:
        for hd in range(GLA_HEADS):
            ks = slice(hd * dk, (hd + 1) * dk)
            vs = slice(hd * dv, (hd + 1) * dv)
            chains.append(dict(q_state=q_state[:, ks], q_in=q_in[:, ks], k_in=k_in[:, ks], k_out=k_out[:, ks],
                               decays=[d[:, ks] for d in decays], v=v[:, vs], keep=keep, s_ref=s_ref,
                               o_ref=o_ref, hd=hd, vs=vs, order=order))
    chunk = lambda x, a: x[a * c:(a + 1) * c]
    for lo in range(0, len(chains), GLA_HEADS):
        _gla_chains(chains[lo:lo + GLA_HEADS], chunk, n_chunks)


def _gla(gq, gk, gv, logf, *, n_lat_blocks, n_ctx_blocks):
    b_, lp, qk = gq.shape
    vv = gv.shape[2]
    c = GLA_BLOCK
    n = n_lat_blocks + n_ctx_blocks
    fwd = lambda b, s: (b, (s + n_lat_blocks) % n, 0)
    bwd = lambda b, s: (b, n - 1 - s, 0)
    lf_spec = pl.BlockSpec((1, c, qk), lambda b, s: (b, (s + n_lat_blocks) % n, 0))
    lb_spec = pl.BlockSpec((1, c, qk), lambda b, s: (b, n - 1 - s, 1))
    return pl.pallas_call(
        _gla_kernel,
        out_shape=(jax.ShapeDtypeStruct((b_, n * c, vv), BF16), jax.ShapeDtypeStruct((b_, n * c, vv), BF16)),
        grid=(b_, n),
        in_specs=[
            pl.BlockSpec((1, c, qk), fwd), pl.BlockSpec((1, c, qk), fwd), pl.BlockSpec((1, c, vv), fwd), lf_spec,
            pl.BlockSpec((1, c, qk), bwd), pl.BlockSpec((1, c, qk), bwd), pl.BlockSpec((1, c, vv), bwd), lb_spec,
        ],
        out_specs=(pl.BlockSpec((1, c, vv), fwd), pl.BlockSpec((1, c, vv), bwd)),
        scratch_shapes=[
            pltpu.VMEM((GLA_HEADS, vv // GLA_HEADS, qk // GLA_HEADS), F32),
            pltpu.VMEM((GLA_HEADS, vv // GLA_HEADS, qk // GLA_HEADS), F32),
        ],
        compiler_params=_params(("parallel", "arbitrary")),
        name="gla_scan",
    )(gq, gk, gv, logf, gq, gk, gv, logf)


def _attn_kernel(q_t_ref, qn_t_ref, k_ref, kn_ref, v_t_ref, lam_ref, o_ref,
                 m_ref, acc_ref, s_ref, bm_ref, s0_ref, bm0_ref, *, tk, n_kblocks, lam_init):
    tq = q_t_ref.shape[2]

    def stacked_queries(ref):
        q_t = ref[0]
        row = lax.broadcasted_iota(jnp.int32, q_t.shape, 0)
        zero = jnp.zeros_like(q_t)
        return jnp.concatenate([jnp.where(row < DIFF_DH, q_t, zero), jnp.where(row >= DIFF_DH, q_t, zero)], axis=1)

    qz_t = stacked_queries(q_t_ref)
    m_ref[...] = jnp.full(m_ref.shape, -jnp.inf, F32)
    acc_ref[...] = jnp.zeros(acc_ref.shape, F32)

    groups = list(range(2 * tq // ATTN_QGROUP))
    lanes = lambda g: slice(g * ATTN_QGROUP, (g + 1) * ATTN_QGROUP)

    def scores(kb, q_all, s_out, bm_out, g):
        s_t = _dot(kb, q_all[:, lanes(g)])
        s_out[g] = s_t
        bm_out[g] = jnp.max(s_t, axis=0, keepdims=True)

    ones_rows = jnp.ones((BF16_ROWS, tk), BF16)

    def softmax_values(vb_t, s_in, bm_in, g):
        s_t = s_in[g]
        m_prev = m_ref[g]
        m_new = jnp.maximum(m_prev, bm_in[g])
        m_ref[g] = m_new
        alpha = jnp.exp2(m_prev - m_new)
        p_t = jnp.exp2(s_t - m_new).astype(BF16)
        acc_ref[g] = alpha * acc_ref[g] + _dot(vb_t, p_t)

    def load_keys(j):
        return k_ref[0, pl.ds(pl.multiple_of(j * tk, tk), tk), :]

    def load_values(j):
        vb_t = v_t_ref[0, :, pl.ds(pl.multiple_of(j * tk, tk), tk)]
        return jnp.concatenate([vb_t, ones_rows], axis=0)

    def stage(j, parity, last=False):
        s_in, bm_in = (s0_ref, bm0_ref) if parity is None else (s_ref.at[parity], bm_ref.at[parity])
        if last:
            kb, q_all, s_out, bm_out = kn_ref[0], stacked_queries(qn_t_ref), s0_ref, bm0_ref
        else:
            nxt = 1 if parity is None else 1 - parity
            kb, q_all, s_out, bm_out = load_keys(j + 1), qz_t, s_ref.at[nxt], bm_ref.at[nxt]
        vb_t = load_values(j)
        for g in groups:
            scores(kb, q_all, s_out, bm_out, g)
            softmax_values(vb_t, s_in, bm_in, g)

    @pl.when((pl.program_id(0) == 0) & (pl.program_id(1) == 0) & (pl.program_id(2) == 0))
    def _():
        kb0 = load_keys(0)
        for g in groups:
            scores(kb0, qz_t, s0_ref, bm0_ref, g)

    assert n_kblocks >= 2
    stage(0, None)
    n_pairs = (n_kblocks - 2) // 2

    def pair(i, carry):
        stage(2 * i + 1, 1)
        stage(2 * i + 2, 0)
        return carry

    lax.fori_loop(0, n_pairs, pair, 0)
    for j in range(2 * n_pairs + 1, n_kblocks):
        stage(j, j % 2, last=j == n_kblocks - 1)

    lp = lam_ref[...]
    lam = (jnp.exp(jnp.sum(lp[0:1] * lp[1:2], axis=-1, keepdims=True))
           - jnp.exp(jnp.sum(lp[2:3] * lp[3:4], axis=-1, keepdims=True)) + lam_init)
    dv = v_t_ref.shape[1]
    acc = jnp.concatenate([acc_ref[g] for g in groups], axis=1)
    o_t = acc[:dv] / acc[dv:dv + 1]
    o_ref[0] = (o_t[:, :tq] - lam * o_t[:, tq:]).T.astype(BF16)


def _attn(dq_t, dk, dv_t, lam_p, *, seq, n_keys, lam_init):
    b_, lp, width = dk.shape
    tq, tk = ATTN_TQ, ATTN_TK
    heads = width // LANES
    n_q = seq // tq
    n_groups = 2 * tq // ATTN_QGROUP
    kern = functools.partial(_attn_kernel, tk=tk, n_kblocks=n_keys // tk, lam_init=lam_init)

    def following(b, h, i):
        n = jnp.minimum((b * heads + h) * n_q + i + 1, b_ * heads * n_q - 1)
        return n // (heads * n_q), (n // n_q) % heads, n % n_q

    def next_keys(b, h, i):
        b2, h2, _ = following(b, h, i)
        return b2, 0, h2

    return pl.pallas_call(
        kern,
        out_shape=jax.ShapeDtypeStruct((b_, seq, width), BF16),
        grid=(b_, heads, n_q),
        in_specs=[
            pl.BlockSpec((1, LANES, tq), lambda b, h, i: (b, h, i)),
            pl.BlockSpec((1, LANES, tq), following),
            pl.BlockSpec((1, lp, LANES), lambda b, h, i: (b, 0, h)),
            pl.BlockSpec((1, tk, LANES), next_keys),
            pl.BlockSpec((1, LANES, lp), lambda b, h, i: (b, h, 0)),
            pl.BlockSpec(lam_p.shape, lambda b, h, i: (0, 0)),
        ],
        out_specs=pl.BlockSpec((1, tq, LANES), lambda b, h, i: (b, i, h)),
        scratch_shapes=[
            pltpu.VMEM((n_groups, 1, ATTN_QGROUP), F32),
            pltpu.VMEM((n_groups, LANES + BF16_ROWS, ATTN_QGROUP), F32),
            pltpu.VMEM((2, n_groups, tk, ATTN_QGROUP), F32),
            pltpu.VMEM((2, n_groups, 1, ATTN_QGROUP), F32),
            pltpu.VMEM((n_groups, tk, ATTN_QGROUP), F32),
            pltpu.VMEM((n_groups, 1, ATTN_QGROUP), F32),
        ],
        compiler_params=pltpu.CompilerParams(dimension_semantics=("arbitrary", "arbitrary", "arbitrary"),
                                             vmem_limit_bytes=ATTN_VMEM_LIMIT),
        name="diff_attn",
    )(dq_t, dq_t, dk, dk, dv_t, lam_p)


def _head_rms(x, gain, width):
    parts = []
    for h0 in range(0, x.shape[1], width):
        parts.append(_rms(x[:, h0:h0 + width]))
    return jnp.concatenate(parts, axis=1) * gain


def _merge_kernel(x_ref, mod_ref, of_ref, ob_ref, sg_ref, od_ref, sm_ref, gg_ref, gd_ref,
                  wa_ref, wb_ref, wo_ref, o_ref, *, dv_gla, dv_diff, diff_scale):
    d = x_ref.shape[2]
    rows = x_ref.shape[1]
    ya = jnp.zeros((rows, d), F32)
    yb = jnp.zeros((rows, d), F32)
    out = jnp.zeros((rows, d), F32)
    for c0 in range(0, d, MXU_TILE):
        cols = slice(c0, c0 + MXU_TILE)
        o_gla = of_ref[0, :, cols].astype(F32) + ob_ref[0, :, cols].astype(F32)
        lhs = _head_rms(o_gla, gg_ref[:, cols], dv_gla) * sg_ref[0, :, cols].astype(F32)
        ya = ya + _dot(lhs.astype(BF16), wa_ref[cols, :])
    for c0 in range(0, d, MXU_TILE):
        cols = slice(c0, c0 + MXU_TILE)
        lhs = _head_rms(od_ref[0, :, cols].astype(F32), gd_ref[:, cols], dv_diff) * diff_scale
        yb = yb + _dot(lhs.astype(BF16), wb_ref[cols, :])
    for c0 in range(0, d, MXU_TILE):
        cols = slice(c0, c0 + MXU_TILE)
        gate_a = sm_ref[0, :, cols].astype(F32)
        gate_b = sm_ref[0, :, d + c0:d + c0 + MXU_TILE].astype(F32)
        merged = (gate_a * ya[:, cols] + gate_b * yb[:, cols]).astype(BF16)
        out = out + _dot(merged, wo_ref[cols, :])
    gate = mod_ref[0, 0][5:6]
    o_ref[0] = x_ref[0] + gate * out


def _merge(x1, modt, o_f, o_b, sgg, o_d, smg, g_gla, g_diff, wa, wb, wo, *, n_lat_tiles, diff_scale):
    b_, _, d = x1.shape
    tm = TOKEN_TILE
    row = lambda width: pl.BlockSpec((1, tm, width), lambda b, j: (b, j, 0))
    kern = functools.partial(_merge_kernel, dv_gla=o_f.shape[2] // GLA_HEADS,
                             dv_diff=o_d.shape[2] // DIFF_HEADS, diff_scale=diff_scale)
    return pl.pallas_call(
        kern,
        out_shape=jax.ShapeDtypeStruct((b_, n_lat_tiles * tm, d), F32),
        grid=(b_, n_lat_tiles),
        in_specs=[
            row(d),
            pl.BlockSpec((1, 1, MOD_ROWS, d), lambda b, j: (b, 0, 0, 0)),
            row(o_f.shape[2]), row(o_b.shape[2]), row(sgg.shape[2]), row(o_d.shape[2]), row(smg.shape[2]),
            _resident(g_gla.shape), _resident(g_diff.shape),
            _resident(wa.shape), _resident(wb.shape), _resident(wo.shape),
        ],
        out_specs=row(d),
        compiler_params=_params(("parallel", "parallel")),
        name="merge",
    )(x1, modt, o_f, o_b, sgg, o_d, smg, g_gla, g_diff, wa, wb, wo)


def _rope_tables(seq, lp):
    half = DIFF_DH // 2
    quarter = half // 2
    inv = ROPE_BASE ** (-np.arange(quarter, dtype=np.float32) / quarter)
    t = jnp.arange(seq, dtype=jnp.int32)
    ang_row = (t // GRID_W).astype(F32)[:, None] * inv[None, :]
    ang_col = (t % GRID_W).astype(F32)[:, None] * inv[None, :]
    ang = jnp.concatenate([ang_row, ang_row, ang_col, ang_col], axis=1)
    sign = np.tile(np.concatenate([-np.ones(quarter), np.ones(quarter)]), 2).astype(np.float32)
    cos = jnp.tile(jnp.cos(ang), (1, LANES // DIFF_DH))
    sin = jnp.tile(jnp.sin(ang) * sign[None, :], (1, LANES // DIFF_DH))
    pad = lp - seq
    cos = jnp.concatenate([cos, jnp.ones((pad, LANES), F32)], axis=0)
    sin = jnp.concatenate([sin, jnp.zeros((pad, LANES), F32)], axis=0)
    tab_t = jnp.concatenate([jnp.cos(ang_row), jnp.sin(ang_row), jnp.cos(ang_col), jnp.sin(ang_col)], axis=1).T
    ident = np.repeat(np.array([1.0, 0.0, 1.0, 0.0], np.float32), quarter)[:, None]
    tab_t = jnp.concatenate([tab_t, jnp.broadcast_to(ident, (DIFF_DH, pad))], axis=1)
    return cos, sin, tab_t


def kernel(x, c, ctx, c_ctx, w_ada, b_ada, ffn1_norm, ffn1_w1, ffn1_w3, ffn1_w2, mix_norm, w_in, gla_gate_w2, gla_gate_b, gla_out_norm, diff_lambda, diff_out_norm, w_branch_gla, w_branch_diff, w_out, ffn2_norm, ffn2_w1, ffn2_w3, ffn2_w2, final_norm):
    b_, seq, d = x.shape
    n_ctx = ctx.shape[1]
    assert w_ada.shape[0] == 1, "single-layer stack"
    tm = TOKEN_TILE
    assert seq % tm == 0 and seq % ATTN_TQ == 0 and n_ctx <= tm
    assert seq % GLA_BLOCK == 0 and n_ctx % GLA_BLOCK == 0 and (seq + n_ctx) % ATTN_TK == 0
    n_lat_tiles = seq // tm
    n_tiles = n_lat_tiles + 1
    lp = n_tiles * tm
    lam_init = 0.8 - 0.6 * math.exp(0.0)

    gla_qk = GLA_HEADS * (d // 8)
    gla_v = GLA_HEADS * (d // 4)
    dif = DIFF_HEADS * 2 * DIFF_DH
    splits = np.cumsum([gla_qk, gla_qk, gla_v, gla_v, 2 * GLA_RANK, dif, dif, dif, 2 * d])
    assert w_in.shape[2] == splits[-1]
    wi = w_in[0].astype(BF16)
    w_gq, w_gk, w_gv, w_gg, w_lr, w_dq, w_dk, w_dv, w_mg = jnp.split(wi, splits[:-1].tolist(), axis=1)
    w_lr = jnp.pad(w_lr, ((0, 0), (0, LANES - 2 * GLA_RANK)))
    w2cat = jnp.zeros((LANES, 2 * gla_qk), F32)
    w2cat = w2cat.at[:GLA_RANK, :gla_qk].set(gla_gate_w2[0, 0])
    w2cat = w2cat.at[GLA_RANK:2 * GLA_RANK, gla_qk:].set(gla_gate_w2[0, 1]).astype(BF16)
    b2cat = gla_gate_b[0].reshape(1, 2 * gla_qk)

    rows = -(-(b_ + 1) // 8) * 8
    c_rows = jnp.concatenate([c, c_ctx[None, :], jnp.zeros((rows - b_ - 1, d), F32)], axis=0)
    mods = _mod_table(c_rows, w_ada[0], b_ada).reshape(rows, N_MOD, d)
    modt = jnp.stack([mods[:b_], jnp.broadcast_to(mods[b_], (b_, N_MOD, d))], axis=1)
    modt = jnp.pad(modt, ((0, 0), (0, 0), (0, MOD_ROWS - N_MOD), (0, 0)))

    ctx_tile = jnp.pad(ctx, ((0, 0), (0, tm - n_ctx), (0, 0)))
    tiles = dict(n_lat_tiles=n_lat_tiles, n_tiles=n_tiles)

    x1 = _ffn(x, modt, ffn1_norm, ffn1_w1[0].astype(BF16), ffn1_w3[0].astype(BF16),
              ffn1_w2[0].astype(BF16), row0=0, ctx_tile=ctx_tile, **tiles)

    cos_t, sin_t, tab_t = _rope_tables(seq, lp)
    gq, gk, gv, sgg, logf, dq, dk, dv, smg = _proj(
        x1, modt, mix_norm, (w_gq, w_gk, w_gv, w_gg, w_lr, w2cat, b2cat), (cos_t, sin_t, tab_t),
        (w_dq.T, w_dk, w_dv.T, w_mg), **tiles)

    o_f, o_b = _gla(gq, gk, gv, logf, n_lat_blocks=seq // GLA_BLOCK, n_ctx_blocks=n_ctx // GLA_BLOCK)

    lam_p = jnp.pad(diff_lambda[0], ((0, 4), (0, LANES - DIFF_DH)))
    o_d = _attn(dq, dk, dv, lam_p, seq=seq, n_keys=seq + n_ctx, lam_init=lam_init)

    x2 = _merge(x1, modt, o_f, o_b, sgg, o_d, smg, gla_out_norm, diff_out_norm,
                w_branch_gla[0].astype(BF16), w_branch_diff[0].astype(BF16), w_out[0].astype(BF16),
                n_lat_tiles=n_lat_tiles, diff_scale=1.0 - lam_init)

    return _ffn(x2, modt, ffn2_norm, ffn2_w1[0].astype(BF16), ffn2_w3[0].astype(BF16),
                ffn2_w2[0].astype(BF16), row0=6, n_lat_tiles=n_lat_tiles, n_tiles=n_lat_tiles,
                final_gain=final_norm.reshape(1, d))
```

```python
import functools
import math

import jax
import jax.numpy as jnp
import numpy as np
from jax import lax
from jax.experimental import pallas as pl
from jax.experimental.pallas import tpu as pltpu

F32 = jnp.float32
BF16 = jnp.bfloat16

EPS = 1e-6
GRID_W = 64
N_MOD = 9
MOD_ROWS = 16
GLA_HEADS = 4
GLA_RANK = 16
GLA_TAU = 16.0
GLA_CHUNK = 64
GLA_BLOCK = 256
DIFF_HEADS = 8
DIFF_DH = 64
ROPE_BASE = 10000.0
ROPE_PAIR = DIFF_DH // 4
LANES = 128
MXU_TILE = 256
BF16_ROWS = 16
LOG2_E = 1.4426950408889634

TOKEN_TILE = 512
ATTN_TQ = 2048
ATTN_TK = 768
ATTN_QGROUP = 256
V7X_VMEM_BYTES = 64 * 1024 * 1024
VMEM_LIMIT = V7X_VMEM_BYTES * 13 // 16
ATTN_VMEM_LIMIT = V7X_VMEM_BYTES * 7 // 8


def _resident(shape):
    zeros = (0,) * len(shape)
    return pl.BlockSpec(shape, lambda *_: zeros, pipeline_mode=pl.Buffered(1))


def _params(semantics):
    return pltpu.CompilerParams(dimension_semantics=semantics, vmem_limit_bytes=VMEM_LIMIT)


def _dot(a, b):
    return jnp.dot(a, b, preferred_element_type=F32)


def _dot_nt(a, b):
    return lax.dot_general(a, b, (((1,), (1,)), ((), ())), preferred_element_type=F32)


def _dot_tn(a, b):
    return lax.dot_general(a, b, (((0,), (0,)), ((), ())), preferred_element_type=F32)


def _sigmoid(x):
    return 1.0 / (1.0 + jnp.exp(-x))


def _rms(x):
    return x * lax.rsqrt(jnp.mean(x * x, axis=-1, keepdims=True) + EPS)


def _modnorm(x, gain, shift, scale1):
    return (_rms(x) * gain) * scale1 + shift


def _mod_kernel(c_ref, w_ref, b_ref, o_ref):
    sub_layer = pl.program_id(0)
    d = c_ref.shape[1]
    c = c_ref[...]
    s = c * _sigmoid(c)
    s_parts = _bf16_terms(s, 3)
    half_gate = jnp.where(sub_layer == 1, 1.0, 0.5).astype(F32)
    for part, (add, mul) in enumerate(((0.0, 1.0), (1.0, 1.0), (0.0, half_gate))):
        cols = slice(part * d, (part + 1) * d)
        w_parts = _bf16_terms(w_ref[:, cols], 3)
        acc = jnp.zeros((c_ref.shape[0], d), F32)
        for i in range(3):
            for j in range(3 - i):
                acc = acc + _dot(s_parts[i], w_parts[j])
        o_ref[:, cols] = (acc + b_ref[:, cols] + add) * mul


def _bf16_terms(x, n):
    terms = []
    for _ in range(n):
        t = x.astype(BF16)
        terms.append(t)
        x = x - t.astype(F32)
    return terms


def _mod_table(c_rows, w_ada, b_ada):
    rows, d = c_rows.shape
    return pl.pallas_call(
        _mod_kernel,
        out_shape=jax.ShapeDtypeStruct((rows, N_MOD * d), F32),
        grid=(N_MOD // 3,),
        in_specs=[
            pl.BlockSpec((rows, d), lambda k: (0, 0)),
            pl.BlockSpec((d, 3 * d), lambda k: (0, k)),
            pl.BlockSpec((1, 3 * d), lambda k: (0, k)),
        ],
        out_specs=pl.BlockSpec((rows, 3 * d), lambda k: (0, k)),
        compiler_params=_params(("arbitrary",)),
        name="mod_table",
    )(c_rows, w_ada, b_ada)


def _ffn_kernel(x_ref, mod_ref, g_ref, w1_ref, w3_ref, w2_ref, *rest, row0, f_chunk, n_lat_tiles, final):
    if final:
        gf_ref, o_ref = rest
        x = x_ref[0]
    else:
        c_ref, o_ref = rest
        x = jnp.where(pl.program_id(1) >= n_lat_tiles, c_ref[0], x_ref[0])
    mod = mod_ref[0, 0]
    shift, scale1, gate = mod[row0:row0 + 1], mod[row0 + 1:row0 + 2], mod[row0 + 2:row0 + 3]
    h = _modnorm(x, g_ref[...], shift, scale1).astype(BF16)
    acc = jnp.zeros(x.shape, F32)
    f_total = w1_ref.shape[1]
    for f0 in range(0, f_total, f_chunk):
        f1 = min(f0 + f_chunk, f_total)
        a = _dot(h, w1_ref[:, f0:f1])
        b = _dot(h, w3_ref[:, f0:f1])
        u = (a * _sigmoid(a) * b).astype(BF16)
        acc = acc + _dot(u, w2_ref[f0:f1, :])
    y = x + gate * acc
    if final:
        y = _rms(y) * gf_ref[...]
    o_ref[0] = y


def _ffn(x, modt, gain, w1, w3, w2, *, row0, n_lat_tiles, n_tiles, ctx_tile=None, final_gain=None):
    b_, _, d = x.shape
    f = w1.shape[1]
    tm = TOKEN_TILE
    final = final_gain is not None
    assert final != (ctx_tile is not None)
    f_chunk = -(-f // (2 * MXU_TILE)) * MXU_TILE
    kern = functools.partial(_ffn_kernel, row0=row0, f_chunk=f_chunk, n_lat_tiles=n_lat_tiles, final=final)
    in_specs = [
        pl.BlockSpec((1, tm, d), lambda b, j: (b, jnp.minimum(j, n_lat_tiles - 1), 0)),
        pl.BlockSpec((1, 1, MOD_ROWS, d), lambda b, j: (b, (j >= n_lat_tiles).astype(jnp.int32), 0, 0)),
        _resident((1, d)),
        _resident((d, f)),
        _resident((d, f)),
        _resident((f, d)),
    ]
    args = [x, modt, gain, w1, w3, w2]
    if final:
        in_specs.append(_resident((1, d)))
        args.append(final_gain)
    else:
        in_specs.append(pl.BlockSpec((1, tm, d), lambda b, j: (b, 0, 0)))
        args.append(ctx_tile)
    return pl.pallas_call(
        kern,
        out_shape=jax.ShapeDtypeStruct((b_, n_tiles * tm, d), F32),
        grid=(b_, n_tiles),
        in_specs=in_specs,
        out_specs=pl.BlockSpec((1, tm, d), lambda b, j: (b, j, 0)),
        compiler_params=_params(("parallel", "parallel")),
        name="ffn_final" if final else "ffn",
    )(*args)


def _run_pieces(pieces):
    pending = None
    for matmul, tail in pieces:
        result = matmul()
        if pending is not None:
            pending[0](pending[1])
        pending = (tail, result)
    pending[0](pending[1])


def _interleave(a, b):
    n = min(len(a), len(b))
    return [p for pair in zip(a, b) for p in pair] + a[n:] + b[n:]


def _column_pieces(lhs, w_ref, tail_of):
    def piece(cols):
        return (lambda: _dot(lhs, w_ref[:, cols])), tail_of(cols)
    return [piece(slice(c0, c0 + MXU_TILE)) for c0 in range(0, w_ref.shape[1], MXU_TILE)]


def _gla_pieces(h, wq_ref, wk_ref, wv_ref, wg_ref, wlr_ref, w2_ref, b2_ref,
                q_ref, k_ref, v_ref, sg_ref, lf_ref, *, q_scale):
    lr = _dot(h, wlr_ref[...]).astype(BF16)

    def decay_tail(cols):
        def tail(z):
            z = z + b2_ref[:, cols]
            lf_ref[0, :, cols] = (jnp.minimum(z, 0.0) - jnp.log(1.0 + jnp.exp(-jnp.abs(z)))) * (1.0 / GLA_TAU)
        return tail

    def gate_tail(cols):
        def tail(gg):
            sg_ref[0, :, cols] = (gg * _sigmoid(gg)).astype(BF16)
        return tail

    def value_tail(cols):
        def tail(v):
            v_ref[0, :, cols] = v.astype(BF16)
        return tail

    def query_tail(cols):
        def tail(q):
            q_ref[0, :, cols] = q * q_scale
        return tail

    def key_tail(cols):
        def tail(k):
            k_ref[0, :, cols] = k
        return tail

    heavy = _interleave(_column_pieces(lr, w2_ref, decay_tail), _column_pieces(h, wg_ref, gate_tail))
    light = (_column_pieces(h, wv_ref, value_tail) + _column_pieces(h, wq_ref, query_tail)
             + _column_pieces(h, wk_ref, key_tail))
    return _interleave(heavy, light)


def _rope(u, cos, sin_signed, first_half):
    out = []
    for g in range(u.shape[1] // LANES):
        ug = u[:, g * LANES:(g + 1) * LANES]
        partner = jnp.where(first_half, pltpu.roll(ug, LANES - ROPE_PAIR, 1), pltpu.roll(ug, ROPE_PAIR, 1))
        out.append(ug * cos + partner * sin_signed)
    return jnp.concatenate(out, axis=1)


def _dif_pieces(h, cos_ref, sin_ref, tab_t_ref, wq_t_ref, wk_ref, wv_t_ref, wm_ref,
                q_t_ref, k_ref, v_t_ref, sm_ref, *, q_scale):
    quarter = DIFF_DH // 4
    tab = tab_t_ref[...]
    cos = cos_ref[...]
    sin_signed = sin_ref[...]
    lane = lax.broadcasted_iota(jnp.int32, cos.shape, 1)
    first_half = (lane % (2 * ROPE_PAIR)) < ROPE_PAIR

    t_rows = 2 * MXU_TILE

    def row_pieces(w_t_ref, tail_of):
        def piece(r0):
            return (lambda: _dot_nt(w_t_ref[r0:r0 + t_rows, :], h)), tail_of(r0)
        return [piece(r0) for r0 in range(0, w_t_ref.shape[0], t_rows)]

    def query_tail(r0):
        def tail(q_t):
            for base in range(0, t_rows, 2 * quarter):
                t0 = base % DIFF_DH
                c, s = tab[t0:t0 + quarter], tab[t0 + quarter:t0 + 2 * quarter]
                u1 = q_t[base:base + quarter]
                u2 = q_t[base + quarter:base + 2 * quarter]
                q_t_ref[0, r0 + base:r0 + base + quarter, :] = ((u1 * c - u2 * s) * q_scale).astype(BF16)
                q_t_ref[0, r0 + base + quarter:r0 + base + 2 * quarter, :] = (
                    (u2 * c + u1 * s) * q_scale).astype(BF16)
        return tail

    def key_tail(cols):
        def tail(k):
            k_ref[0, :, cols] = _rope(k, cos, sin_signed, first_half).astype(BF16)
        return tail

    def value_tail(r0):
        def tail(v_t):
            v_t_ref[0, r0:r0 + t_rows, :] = v_t.astype(BF16)
        return tail

    def gate_tail(cols):
        def tail(mg):
            sm_ref[0, :, cols] = _sigmoid(mg).astype(BF16)
        return tail

    rotary = _interleave(row_pieces(wq_t_ref, query_tail), _column_pieces(h, wk_ref, key_tail))
    plain = _interleave(_column_pieces(h, wm_ref, gate_tail), row_pieces(wv_t_ref, value_tail))
    return _interleave(rotary, plain)


N_GLA_IN, N_GLA_OUT, N_DIF_IN, N_DIF_OUT = 7, 5, 7, 4


def _proj_kernel(x_ref, mod_ref, g_ref, *refs, gla_q_scale, dif_q_scale):
    x = x_ref[0]
    mod = mod_ref[0, 0]
    h = _modnorm(x, g_ref[...], mod[3:4], mod[4:5]).astype(BF16)
    gla_in, refs = refs[:N_GLA_IN], refs[N_GLA_IN:]
    dif_in, refs = refs[:N_DIF_IN], refs[N_DIF_IN:]
    gla_out, dif_out = refs[:N_GLA_OUT], refs[N_GLA_OUT:]
    assert len(dif_out) == N_DIF_OUT
    _run_pieces(_interleave(_gla_pieces(h, *gla_in, *gla_out, q_scale=gla_q_scale),
                            _dif_pieces(h, *dif_in, *dif_out, q_scale=dif_q_scale)))


def _proj(xs, modt, gain, gla_w, dif_tabs, dif_w, *, n_lat_tiles, n_tiles):
    b_, lp, d = xs.shape
    tm = TOKEN_TILE
    wq, _, wv = gla_w[:3]
    qk, vv = wq.shape[1], wv.shape[1]
    wq_t, wk_d, wv_t, wm = dif_w
    assert len(gla_w) == N_GLA_IN and len(dif_tabs) + len(dif_w) == N_DIF_IN
    row = lambda width: pl.BlockSpec((1, tm, width), lambda b, j: (b, j, 0))
    col = lambda height: pl.BlockSpec((1, height, tm), lambda b, j: (b, 0, j))
    tab = pl.BlockSpec((tm, LANES), lambda b, j: (j, 0))
    kern = functools.partial(_proj_kernel, gla_q_scale=(qk // GLA_HEADS) ** -0.5,
                             dif_q_scale=DIFF_DH ** -0.5 * LOG2_E)
    return pl.pallas_call(
        kern,
        out_shape=(
            jax.ShapeDtypeStruct((b_, lp, qk), F32),
            jax.ShapeDtypeStruct((b_, lp, qk), F32),
            jax.ShapeDtypeStruct((b_, lp, vv), BF16),
            jax.ShapeDtypeStruct((b_, lp, vv), BF16),
            jax.ShapeDtypeStruct((b_, lp, 2 * qk), F32),
            jax.ShapeDtypeStruct((b_, wq_t.shape[0], lp), BF16),
            jax.ShapeDtypeStruct((b_, lp, wk_d.shape[1]), BF16),
            jax.ShapeDtypeStruct((b_, wv_t.shape[0], lp), BF16),
            jax.ShapeDtypeStruct((b_, lp, wm.shape[1]), BF16),
        ),
        grid=(b_, n_tiles),
        in_specs=[
            row(d),
            pl.BlockSpec((1, 1, MOD_ROWS, d), lambda b, j: (b, (j >= n_lat_tiles).astype(jnp.int32), 0, 0)),
            _resident((1, d)),
            *[_resident(w.shape) for w in gla_w],
            tab, tab, pl.BlockSpec((DIFF_DH, tm), lambda b, j: (0, j)),
            *[_resident(w.shape) for w in dif_w],
        ],
        out_specs=(row(qk), row(qk), row(vv), row(vv), row(2 * qk),
                   col(wq_t.shape[0]), row(wk_d.shape[1]), col(wv_t.shape[0]), row(wm.shape[1])),
        compiler_params=_params(("parallel", "parallel")),
        name="proj",
    )(xs, modt, gain, *gla_w, *dif_tabs, *dif_w)


def _gla_decays(q, k, logf, tri, ref_row, end_row):
    c = GLA_CHUNK
    hi, lo = _bf16_terms(logf, 2)
    cum = _dot(tri, hi) + _dot(tri, lo)
    n_chunks = q.shape[0] // c
    totals = [cum[a * c + end_row:a * c + end_row + 1] for a in range(n_chunks)]
    per_chunk = lambda rows: jnp.concatenate([jnp.broadcast_to(r, (c, r.shape[1])) for r in rows], axis=0)
    ref = per_chunk([cum[a * c + ref_row:a * c + ref_row + 1] for a in range(n_chunks)])
    total = per_chunk(totals)
    q_state = (q * jnp.exp(cum)).astype(BF16)
    q_in = (q * jnp.exp(cum - ref)).astype(BF16)
    k_in = (k * jnp.exp(ref - cum)).astype(BF16)
    k_out = (k * jnp.exp(total - cum)).astype(BF16)
    return q_state, q_in, k_in, k_out, [jnp.exp(t) for t in totals]


def _gla_chains(chains, chunk, n_chunks):
    for ch in chains:
        ch["attn"] = _dot_nt(ch["q_in"], ch["k_in"])
    for ch in chains:
        ch["update"] = [_dot_tn(chunk(ch["v"], a), chunk(ch["k_out"], a)) for a in range(n_chunks)]
    for ch in chains:
        state_t = ch["s_ref"][ch["hd"]]
        ch["before"] = {}
        for a in ch["order"]:
            ch["before"][a] = state_t
            state_t = state_t * ch["decays"][a] + ch["update"][a]
        ch["s_ref"][ch["hd"]] = state_t
    for ch in chains:
        ch["inter"] = jnp.concatenate(
            [_dot_nt(chunk(ch["q_state"], a), ch["before"][a].astype(BF16)) for a in range(n_chunks)], axis=0)
    for ch in chains:
        attn = jnp.where(ch["keep"], ch["attn"], 0.0).astype(BF16)
        ch["o_ref"][0, :, ch["vs"]] = (ch["inter"] + _dot(attn, ch["v"])).astype(BF16)


def _gla_kernel(qf_ref, kf_ref, vf_ref, lf_ref, qb_ref, kb_ref, vb_ref, lb_ref,
                of_ref, ob_ref, sf_ref, sb_ref):
    @pl.when(pl.program_id(1) == 0)
    def _():
        sf_ref[...] = jnp.zeros(sf_ref.shape, F32)
        sb_ref[...] = jnp.zeros(sb_ref.shape, F32)

    c = GLA_CHUNK
    rows = qf_ref.shape[1]
    n_chunks = rows // c
    row = lax.broadcasted_iota(jnp.int32, (rows, rows), 0)
    col = lax.broadcasted_iota(jnp.int32, (rows, rows), 1)
    same_chunk = (row // c) == (col // c)
    lower = same_chunk & (col <= row)
    upper = same_chunk & (col >= row)
    dirs = [
        (_gla_decays(qf_ref[0], kf_ref[0], lf_ref[0], jnp.where(lower, 1.0, 0.0).astype(BF16),
                     c // 2 - 1, c - 1), vf_ref[0], lower, sf_ref, of_ref, list(range(n_chunks))),
        (_gla_decays(qb_ref[0], kb_ref[0], lb_ref[0], jnp.where(upper, 1.0, 0.0).astype(BF16),
                     c // 2, 0), vb_ref[0], upper, sb_ref, ob_ref, list(range(n_chunks))[::-1]),
    ]
    dk = qf_ref.shape[2] // GLA_HEADS
    dv = vf_ref.shape[2] // GLA_HEADS
    chains = []
    for (q_state, q_in, k_in, k_out, decays), v, keep, s_ref, o_ref, order in dirs:
        for hd in range(GLA_HEADS):
            ks = slice(hd * dk, (hd + 1) * dk)
            vs = slice(hd * dv, (hd + 1) * dv)
            chains.append(dict(q_state=q_state[:, ks], q_in=q_in[:, ks], k_in=k_in[:, ks], k_out=k_out[:, ks],
                               decays=[d[:, ks] for d in decays], v=v[:, vs], keep=keep, s_ref=s_ref,
                               o_ref=o_ref, hd=hd, vs=vs, order=order))
    chunk = lambda x, a: x[a * c:(a + 1) * c]
    for lo in range(0, len(chains), GLA_HEADS):
        _gla_chains(chains[lo:lo + GLA_HEADS], chunk, n_chunks)


def _gla(gq, gk, gv, logf, *, n_lat_blocks, n_ctx_blocks):
    b_, lp, qk = gq.shape
    vv = gv.shape[2]
    c = GLA_BLOCK
    n = n_lat_blocks + n_ctx_blocks
    fwd = lambda b, s: (b, (s + n_lat_blocks) % n, 0)
    bwd = lambda b, s: (b, n - 1 - s, 0)
    lf_spec = pl.BlockSpec((1, c, qk), lambda b, s: (b, (s + n_lat_blocks) % n, 0))
    lb_spec = pl.BlockSpec((1, c, qk), lambda b, s: (b, n - 1 - s, 1))
    return pl.pallas_call(
        _gla_kernel,
        out_shape=(jax.ShapeDtypeStruct((b_, n * c, vv), BF16), jax.ShapeDtypeStruct((b_, n * c, vv), BF16)),
        grid=(b_, n),
        in_specs=[
            pl.BlockSpec((1, c, qk), fwd), pl.BlockSpec((1, c, qk), fwd), pl.BlockSpec((1, c, vv), fwd), lf_spec,
            pl.BlockSpec((1, c, qk), bwd), pl.BlockSpec((1, c, qk), bwd), pl.BlockSpec((1, c, vv), bwd), lb_spec,
        ],
        out_specs=(pl.BlockSpec((1, c, vv), fwd), pl.BlockSpec((1, c, vv), bwd)),
        scratch_shapes=[
            pltpu.VMEM((GLA_HEADS, vv // GLA_HEADS, qk // GLA_HEADS), F32),
            pltpu.VMEM((GLA_HEADS, vv // GLA_HEADS, qk // GLA_HEADS), F32),
        ],
        compiler_params=_params(("parallel", "arbitrary")),
        name="gla_scan",
    )(gq, gk, gv, logf, gq, gk, gv, logf)


def _attn_kernel(q_t_ref, qn_t_ref, k_ref, kn_ref, v_t_ref, lam_ref, o_ref,
                 m_ref, acc_ref, s_ref, bm_ref, s0_ref, bm0_ref, *, tk, n_kblocks, lam_init):
    tq = q_t_ref.shape[2]

    def stacked_queries(ref):
        q_t = ref[0]
        row = lax.broadcasted_iota(jnp.int32, q_t.shape, 0)
        zero = jnp.zeros_like(q_t)
        return jnp.concatenate([jnp.where(row < DIFF_DH, q_t, zero), jnp.where(row >= DIFF_DH, q_t, zero)], axis=1)

    qz_t = stacked_queries(q_t_ref)
    m_ref[...] = jnp.full(m_ref.shape, -jnp.inf, F32)
    acc_ref[...] = jnp.zeros(acc_ref.shape, F32)

    groups = list(range(2 * tq // ATTN_QGROUP))
    lanes = lambda g: slice(g * ATTN_QGROUP, (g + 1) * ATTN_QGROUP)

    def scores(kb, q_all, s_out, bm_out, g):
        s_t = _dot(kb, q_all[:, lanes(g)])
        s_out[g] = s_t
        bm_out[g] = jnp.max(s_t, axis=0, keepdims=True)

    ones_rows = jnp.ones((BF16_ROWS, tk), BF16)

    def softmax_values(vb_t, s_in, bm_in, g):
        s_t = s_in[g]
        m_prev = m_ref[g]
        m_new = jnp.maximum(m_prev, bm_in[g])
        m_ref[g] = m_new
        alpha = jnp.exp2(m_prev - m_new)
        p_t = jnp.exp2(s_t - m_new).astype(BF16)
        acc_ref[g] = alpha * acc_ref[g] + _dot(vb_t, p_t)

    def load_keys(j):
        return k_ref[0, pl.ds(pl.multiple_of(j * tk, tk), tk), :]

    def load_values(j):
        vb_t = v_t_ref[0, :, pl.ds(pl.multiple_of(j * tk, tk), tk)]
        return jnp.concatenate([vb_t, ones_rows], axis=0)

    def stage(j, parity, last=False):
        s_in, bm_in = (s0_ref, bm0_ref) if parity is None else (s_ref.at[parity], bm_ref.at[parity])
        if last:
            kb, q_all, s_out, bm_out = kn_ref[0], stacked_queries(qn_t_ref), s0_ref, bm0_ref
        else:
            nxt = 1 if parity is None else 1 - parity
            kb, q_all, s_out, bm_out = load_keys(j + 1), qz_t, s_ref.at[nxt], bm_ref.at[nxt]
        vb_t = load_values(j)
        for g in groups:
            scores(kb, q_all, s_out, bm_out, g)
            softmax_values(vb_t, s_in, bm_in, g)

    @pl.when((pl.program_id(0) == 0) & (pl.program_id(1) == 0) & (pl.program_id(2) == 0))
    def _():
        kb0 = load_keys(0)
        for g in groups:
            scores(kb0, qz_t, s0_ref, bm0_ref, g)

    assert n_kblocks >= 2
    stage(0, None)
    n_pairs = (n_kblocks - 2) // 2

    def pair(i, carry):
        stage(2 * i + 1, 1)
        stage(2 * i + 2, 0)
        return carry

    lax.fori_loop(0, n_pairs, pair, 0)
    for j in range(2 * n_pairs + 1, n_kblocks):
        stage(j, j % 2, last=j == n_kblocks - 1)

    lp = lam_ref[...]
    lam = (jnp.exp(jnp.sum(lp[0:1] * lp[1:2], axis=-1, keepdims=True))
           - jnp.exp(jnp.sum(lp[2:3] * lp[3:4], axis=-1, keepdims=True)) + lam_init)
    dv = v_t_ref.shape[1]
    acc = jnp.concatenate([acc_ref[g] for g in groups], axis=1)
    o_t = acc[:dv] / acc[dv:dv + 1]
    o_ref[0] = (o_t[:, :tq] - lam * o_t[:, tq:]).T.astype(BF16)


def _attn(dq_t, dk, dv_t, lam_p, *, seq, n_keys, lam_init):
    b_, lp, width = dk.shape
    tq, tk = ATTN_TQ, ATTN_TK
    heads = width // LANES
    n_q = seq // tq
    n_groups = 2 * tq // ATTN_QGROUP
    kern = functools.partial(_attn_kernel, tk=tk, n_kblocks=n_keys // tk, lam_init=lam_init)

    def following(b, h, i):
        n = jnp.minimum((b * heads + h) * n_q + i + 1, b_ * heads * n_q - 1)
        return n // (heads * n_q), (n // n_q) % heads, n % n_q

    def next_keys(b, h, i):
        b2, h2, _ = following(b, h, i)
        return b2, 0, h2

    return pl.pallas_call(
        kern,
        out_shape=jax.ShapeDtypeStruct((b_, seq, width), BF16),
        grid=(b_, heads, n_q),
        in_specs=[
            pl.BlockSpec((1, LANES, tq), lambda b, h, i: (b, h, i)),
            pl.BlockSpec((1, LANES, tq), following),
            pl.BlockSpec((1, lp, LANES), lambda b, h, i: (b, 0, h)),
            pl.BlockSpec((1, tk, LANES), next_keys),
            pl.BlockSpec((1, LANES, lp), lambda b, h, i: (b, h, 0)),
            pl.BlockSpec(lam_p.shape, lambda b, h, i: (0, 0)),
        ],
        out_specs=pl.BlockSpec((1, tq, LANES), lambda b, h, i: (b, i, h)),
        scratch_shapes=[
            pltpu.VMEM((n_groups, 1, ATTN_QGROUP), F32),
            pltpu.VMEM((n_groups, LANES + BF16_ROWS, ATTN_QGROUP), F32),
            pltpu.VMEM((2, n_groups, tk, ATTN_QGROUP), F32),
            pltpu.VMEM((2, n_groups, 1, ATTN_QGROUP), F32),
            pltpu.VMEM((n_groups, tk, ATTN_QGROUP), F32),
            pltpu.VMEM((n_groups, 1, ATTN_QGROUP), F32),
        ],
        compiler_params=pltpu.CompilerParams(dimension_semantics=("arbitrary", "arbitrary", "arbitrary"),
                                             vmem_limit_bytes=ATTN_VMEM_LIMIT),
        name="diff_attn",
    )(dq_t, dq_t, dk, dk, dv_t, lam_p)


def _head_rms(x, gain, width):
    parts = []
    for h0 in range(0, x.shape[1], width):
        parts.append(_rms(x[:, h0:h0 + width]))
    return jnp.concatenate(parts, axis=1) * gain


def _merge_kernel(x_ref, mod_ref, of_ref, ob_ref, sg_ref, od_ref, sm_ref, gg_ref, gd_ref,
                  wa_ref, wb_ref, wo_ref, o_ref, *, dv_gla, dv_diff, diff_scale):
    d = x_ref.shape[2]
    rows = x_ref.shape[1]
    ya = jnp.zeros((rows, d), F32)
    yb = jnp.zeros((rows, d), F32)
    out = jnp.zeros((rows, d), F32)
    for c0 in range(0, d, MXU_TILE):
        cols = slice(c0, c0 + MXU_TILE)
        o_gla = of_ref[0, :, cols].astype(F32) + ob_ref[0, :, cols].astype(F32)
        lhs = _head_rms(o_gla, gg_ref[:, cols], dv_gla) * sg_ref[0, :, cols].astype(F32)
        ya = ya + _dot(lhs.astype(BF16), wa_ref[cols, :])
    for c0 in range(0, d, MXU_TILE):
        cols = slice(c0, c0 + MXU_TILE)
        lhs = _head_rms(od_ref[0, :, cols].astype(F32), gd_ref[:, cols], dv_diff) * diff_scale
        yb = yb + _dot(lhs.astype(BF16), wb_ref[cols, :])
    for c0 in range(0, d, MXU_TILE):
        cols = slice(c0, c0 + MXU_TILE)
        gate_a = sm_ref[0, :, cols].astype(F32)
        gate_b = sm_ref[0, :, d + c0:d + c0 + MXU_TILE].astype(F32)
        merged = (gate_a * ya[:, cols] + gate_b * yb[:, cols]).astype(BF16)
        out = out + _dot(merged, wo_ref[cols, :])
    gate = mod_ref[0, 0][5:6]
    o_ref[0] = x_ref[0] + gate * out


def _merge(x1, modt, o_f, o_b, sgg, o_d, smg, g_gla, g_diff, wa, wb, wo, *, n_lat_tiles, diff_scale):
    b_, _, d = x1.shape
    tm = TOKEN_TILE
    row = lambda width: pl.BlockSpec((1, tm, width), lambda b, j: (b, j, 0))
    kern = functools.partial(_merge_kernel, dv_gla=o_f.shape[2] // GLA_HEADS,
                             dv_diff=o_d.shape[2] // DIFF_HEADS, diff_scale=diff_scale)
    return pl.pallas_call(
        kern,
        out_shape=jax.ShapeDtypeStruct((b_, n_lat_tiles * tm, d), F32),
        grid=(b_, n_lat_tiles),
        in_specs=[
            row(d),
            pl.BlockSpec((1, 1, MOD_ROWS, d), lambda b, j: (b, 0, 0, 0)),
            row(o_f.shape[2]), row(o_b.shape[2]), row(sgg.shape[2]), row(o_d.shape[2]), row(smg.shape[2]),
            _resident(g_gla.shape), _resident(g_diff.shape),
            _resident(wa.shape), _resident(wb.shape), _resident(wo.shape),
        ],
        out_specs=row(d),
        compiler_params=_params(("parallel", "parallel")),
        name="merge",
    )(x1, modt, o_f, o_b, sgg, o_d, smg, g_gla, g_diff, wa, wb, wo)


def _rope_tables(seq, lp):
    half = DIFF_DH // 2
    quarter = half // 2
    inv = ROPE_BASE ** (-np.arange(quarter, dtype=np.float32) / quarter)
    t = jnp.arange(seq, dtype=jnp.int32)
    ang_row = (t // GRID_W).astype(F32)[:, None] * inv[None, :]
    ang_col = (t % GRID_W).astype(F32)[:, None] * inv[None, :]
    ang = jnp.concatenate([ang_row, ang_row, ang_col, ang_col], axis=1)
    sign = np.tile(np.concatenate([-np.ones(quarter), np.ones(quarter)]), 2).astype(np.float32)
    cos = jnp.tile(jnp.cos(ang), (1, LANES // DIFF_DH))
    sin = jnp.tile(jnp.sin(ang) * sign[None, :], (1, LANES // DIFF_DH))
    pad = lp - seq
    cos = jnp.concatenate([cos, jnp.ones((pad, LANES), F32)], axis=0)
    sin = jnp.concatenate([sin, jnp.zeros((pad, LANES), F32)], axis=0)
    tab_t = jnp.concatenate([jnp.cos(ang_row), jnp.sin(ang_row), jnp.cos(ang_col), jnp.sin(ang_col)], axis=1).T
    ident = np.repeat(np.array([1.0, 0.0, 1.0, 0.0], np.float32), quarter)[:, None]
    tab_t = jnp.concatenate([tab_t, jnp.broadcast_to(ident, (DIFF_DH, pad))], axis=1)
    return cos, sin, tab_t


def kernel(x, c, ctx, c_ctx, w_ada, b_ada, ffn1_norm, ffn1_w1, ffn1_w3, ffn1_w2, mix_norm, w_in, gla_gate_w2, gla_gate_b, gla_out_norm, diff_lambda, diff_out_norm, w_branch_gla, w_branch_diff, w_out, ffn2_norm, ffn2_w1, ffn2_w3, ffn2_w2, final_norm):
    b_, seq, d = x.shape
    n_ctx = ctx.shape[1]
    assert w_ada.shape[0] == 1, "single-layer stack"
    tm = TOKEN_TILE
    assert seq % tm == 0 and seq % ATTN_TQ == 0 and n_ctx <= tm
    assert seq % GLA_BLOCK == 0 and n_ctx % GLA_BLOCK == 0 and (seq + n_ctx) % ATTN_TK == 0
    n_lat_tiles = seq // tm
    n_tiles = n_lat_tiles + 1
    lp = n_tiles * tm
    lam_init = 0.8 - 0.6 * math.exp(0.0)

    gla_qk = GLA_HEADS * (d // 8)
    gla_v = GLA_HEADS * (d // 4)
    dif = DIFF_HEADS * 2 * DIFF_DH
    splits = np.cumsum([gla_qk, gla_qk, gla_v, gla_v, 2 * GLA_RANK, dif, dif, dif, 2 * d])
    assert w_in.shape[2] == splits[-1]
    wi = w_in[0].astype(BF16)
    w_gq, w_gk, w_gv, w_gg, w_lr, w_dq, w_dk, w_dv, w_mg = jnp.split(wi, splits[:-1].tolist(), axis=1)
    w_lr = jnp.pad(w_lr, ((0, 0), (0, LANES - 2 * GLA_RANK)))
    w2cat = jnp.zeros((LANES, 2 * gla_qk), F32)
    w2cat = w2cat.at[:GLA_RANK, :gla_qk].set(gla_gate_w2[0, 0])
    w2cat = w2cat.at[GLA_RANK:2 * GLA_RANK, gla_qk:].set(gla_gate_w2[0, 1]).astype(BF16)
    b2cat = gla_gate_b[0].reshape(1, 2 * gla_qk)

    rows = -(-(b_ + 1) // 8) * 8
    c_rows = jnp.concatenate([c, c_ctx[None, :], jnp.zeros((rows - b_ - 1, d), F32)], axis=0)
    mods = _mod_table(c_rows, w_ada[0], b_ada).reshape(rows, N_MOD, d)
    modt = jnp.stack([mods[:b_], jnp.broadcast_to(mods[b_], (b_, N_MOD, d))], axis=1)
    modt = jnp.pad(modt, ((0, 0), (0, 0), (0, MOD_ROWS - N_MOD), (0, 0)))

    ctx_tile = jnp.pad(ctx, ((0, 0), (0, tm - n_ctx), (0, 0)))
    tiles = dict(n_lat_tiles=n_lat_tiles, n_tiles=n_tiles)

    x1 = _ffn(x, modt, ffn1_norm, ffn1_w1[0].astype(BF16), ffn1_w3[0].astype(BF16),
              ffn1_w2[0].astype(BF16), row0=0, ctx_tile=ctx_tile, **tiles)

    cos_t, sin_t, tab_t = _rope_tables(seq, lp)
    gq, gk, gv, sgg, logf, dq, dk, dv, smg = _proj(
        x1, modt, mix_norm, (w_gq, w_gk, w_gv, w_gg, w_lr, w2cat, b2cat), (cos_t, sin_t, tab_t),
        (w_dq.T, w_dk, w_dv.T, w_mg), **tiles)

    o_f, o_b = _gla(gq, gk, gv, logf, n_lat_blocks=seq // GLA_BLOCK, n_ctx_blocks=n_ctx // GLA_BLOCK)

    lam_p = jnp.pad(diff_lambda[0], ((0, 4), (0, LANES - DIFF_DH)))
    o_d = _attn(dq, dk, dv, lam_p, seq=seq, n_keys=seq + n_ctx, lam_init=lam_init)

    x2 = _merge(x1, modt, o_f, o_b, sgg, o_d, smg, gla_out_norm, diff_out_norm,
                w_branch_gla[0].astype(BF16), w_branch_diff[0].astype(BF16), w_out[0].astype(BF16),
                n_lat_tiles=n_lat_tiles, diff_scale=1.0 - lam_init)

    return _ffn(x2, modt, ffn2_norm, ffn2_w1[0].astype(BF16), ffn2_w3[0].astype(BF16),
                ffn2_w2[0].astype(BF16), row0=6, n_lat_tiles=n_lat_tiles, n_tiles=n_lat_tiles,
                final_gain=final_norm.reshape(1, d))
```
